```python
import math
import jax, jax.numpy as jnp
from jax import lax
import numpy as np

D_MODEL = 1024
BATCH = 16
SEQ = 2048
DEPTH = 1

DIFF_HEADS = 4
DIFF_QK_DIM = 64
DIFF_V_DIM = 2 * DIFF_QK_DIM
DIFF_WIDTH = DIFF_HEADS * DIFF_V_DIM
HGRN_HEADS = 4
HGRN_K_DIM = 128
HGRN_V_DIM = 128
HGRN_WIDTH = HGRN_HEADS * HGRN_V_DIM
HGRN_CHUNK = 32
MIX_WIDTH = DIFF_WIDTH + HGRN_WIDTH
DIFF_Q_COLS = DIFF_HEADS * 2 * DIFF_QK_DIM
HGRN_K_COLS = HGRN_HEADS * HGRN_K_DIM
IN_SPLIT_SIZES = (DIFF_Q_COLS, DIFF_Q_COLS, DIFF_WIDTH, HGRN_K_COLS, HGRN_K_COLS, HGRN_WIDTH, HGRN_WIDTH)
IN_WIDTH = 2 * DIFF_Q_COLS + DIFF_WIDTH + 2 * HGRN_K_COLS + 2 * HGRN_WIDTH
REL_BUCKETS = 32
REL_MAX_DIST = 128
Q_BLOCK = 128
PEER_HEADS = 8
PEER_N_KEYS = 128
PEER_N_EXPERTS = PEER_N_KEYS * PEER_N_KEYS
PEER_QUERY_DIM = 256
PEER_TOPK = 16
PEER_TOKEN_BLOCK = 128
NORM_EPS = 1e-6

kernel_name = 'hymba_diffattn_hgrn2_peer_block'


def rmsnorm(x, g):
    xf = x.astype(jnp.float32)
    y = xf * lax.rsqrt(jnp.mean(xf * xf, axis=-1, keepdims=True) + NORM_EPS)
    return (y * g.astype(jnp.float32)).astype(x.dtype)


def t5_causal_bucket(q_pos, k_pos):
    n = jnp.maximum(q_pos[:, None] - k_pos[None, :], 0)
    max_exact = REL_BUCKETS // 2
    nf = jnp.maximum(n, max_exact).astype(jnp.float32)
    large = max_exact + (jnp.log(nf / max_exact) / math.log(REL_MAX_DIST / max_exact)
                         * (REL_BUCKETS - max_exact)).astype(jnp.int32)
    large = jnp.minimum(large, REL_BUCKETS - 1)
    return jnp.where(n < max_exact, n, large)


def diff_attention(q, k, v, lam, rel_bias_table):
    B, H, _, S, dq = q.shape
    dv = v.shape[-1]
    nb = S // Q_BLOCK
    qb = q.reshape(B, H, 2, nb, Q_BLOCK, dq).transpose(3, 0, 1, 2, 4, 5)
    k_pos = jnp.arange(S)
    scale = DIFF_QK_DIM ** -0.5

    def block(args):
        qblk, bi = args
        q_pos = bi * Q_BLOCK + jnp.arange(Q_BLOCK)
        bucket = t5_causal_bucket(q_pos, k_pos)
        bias = jnp.take(rel_bias_table, bucket, axis=0).transpose(2, 0, 1).astype(jnp.float32)
        logits = jnp.einsum('bhmqd,bhmkd->bhmqk', qblk, k).astype(jnp.float32) * scale + bias[None, :, None]
        causal = k_pos[None, :] <= q_pos[:, None]
        logits = jnp.where(causal, logits, -jnp.inf)
        p = jax.nn.softmax(logits, axis=-1)
        attn = p[:, :, 0] - lam * p[:, :, 1]
        return jnp.einsum('bhqk,bhkd->bhqd', attn.astype(v.dtype), v)

    out = lax.map(block, (qb, jnp.arange(nb)))
    return out.transpose(1, 0, 3, 2, 4).reshape(B, S, H, dv)


def hgrn2_chunked(q, k, v, log_f):
    B, H, S, dk = q.shape
    dv = v.shape[-1]
    C = HGRN_CHUNK
    nc = S // C
    q, k, v, log_f = (t.reshape(B, H, nc, C, t.shape[-1]) for t in (q, k, v, log_f))
    b = jnp.cumsum(log_f, axis=3)
    b_last = b[:, :, :, -1:, :]
    q_dec = q * jnp.exp(b)
    k_inv = k * jnp.exp(-b)
    k_dec = k * jnp.exp(b_last - b)
    causal = jnp.tril(jnp.ones((C, C), dtype=bool))
    scores = jnp.where(causal, jnp.einsum('bhncd,bhnsd->bhncs', q_dec, k_inv), 0.0)
    o_intra = jnp.einsum('bhncs,bhnsv->bhncv', scores, v)
    chunk_decay = jnp.exp(b_last[:, :, :, 0, :])

    def step(state, inp):
        qd, kd, vv, cd = inp
        o = jnp.einsum('bhcd,bhdv->bhcv', qd, state)
        state = cd[..., None] * state + jnp.einsum('bhcd,bhcv->bhdv', kd, vv)
        return state, o

    xs = (jnp.moveaxis(q_dec, 2, 0), jnp.moveaxis(k_dec, 2, 0), jnp.moveaxis(v, 2, 0), jnp.moveaxis(chunk_decay, 2, 0))
    s0 = jnp.zeros((B, H, dk, dv), jnp.float32)
    _, o_inter = lax.scan(step, s0, xs)
    o = o_intra + jnp.moveaxis(o_inter, 0, 2)
    return o.reshape(B, H, S, dv)


def peer(xn, w_q, sub_keys, u, v):
    B, S, D = xn.shape
    T = B * S
    xt = xn.reshape(T, D)
    q = (xt @ w_q).reshape(T, PEER_HEADS, 2, PEER_QUERY_DIM // 2)
    s = jnp.einsum('thcd,hcnd->thcn', q, sub_keys).astype(jnp.float32)
    s1, i1 = lax.top_k(s[:, :, 0], PEER_TOPK)
    s2, i2 = lax.top_k(s[:, :, 1], PEER_TOPK)
    cand = (s1[..., :, None] + s2[..., None, :]).reshape(T, PEER_HEADS, PEER_TOPK * PEER_TOPK)
    cand_idx = (i1[..., :, None] * PEER_N_KEYS + i2[..., None, :]).reshape(T, PEER_HEADS, PEER_TOPK * PEER_TOPK)
    top_s, top_pos = lax.top_k(cand, PEER_TOPK)
    expert_idx = jnp.take_along_axis(cand_idx, top_pos, axis=-1)
    gate = jax.nn.softmax(top_s, axis=-1)
    nblk = T // PEER_TOKEN_BLOCK
    n_sel = PEER_HEADS * PEER_TOPK

    def block(args):
        xb, idx, g = args
        hidden = jnp.einsum('td,tkd->tk', xb, u[idx]).astype(jnp.float32)
        act = jax.nn.gelu(hidden, approximate=False) * g
        return jnp.einsum('tk,tkd->td', act.astype(xb.dtype), v[idx])

    out = lax.map(block, (xt.reshape(nblk, PEER_TOKEN_BLOCK, D),
                          expert_idx.reshape(nblk, PEER_TOKEN_BLOCK, n_sel),
                          gate.reshape(nblk, PEER_TOKEN_BLOCK, n_sel)))
    return out.reshape(B, S, D)


def setup_inputs(seed: int = 0) -> dict:
    key = jax.random.key(seed)
    ks = jax.random.split(key, 18)
    f32 = jnp.float32
    nrm = lambda k, shape, s: jax.random.normal(k, shape, f32) * s
    gain = lambda k, shape: 1.0 + 0.02 * jax.random.normal(k, shape, f32)
    return {
        'x': nrm(ks[0], (BATCH, SEQ, D_MODEL), 1.0),
        'norm1_g': gain(ks[1], (DEPTH, D_MODEL)),
        'w_in': nrm(ks[2], (DEPTH, D_MODEL, IN_WIDTH), D_MODEL ** -0.5),
        'diff_lambda_q1': nrm(ks[3], (DEPTH, DIFF_QK_DIM), 0.1),
        'diff_lambda_k1': nrm(ks[4], (DEPTH, DIFF_QK_DIM), 0.1),
        'diff_lambda_q2': nrm(ks[5], (DEPTH, DIFF_QK_DIM), 0.1),
        'diff_lambda_k2': nrm(ks[6], (DEPTH, DIFF_QK_DIM), 0.1),
        'diff_subln_g': gain(ks[7], (DEPTH, DIFF_V_DIM)),
        'rel_bias_table': nrm(ks[8], (REL_BUCKETS, DIFF_HEADS), 0.5),
        'hgrn_lb_logits': nrm(ks[9], (DEPTH + 1, HGRN_K_COLS), 0.1),
        'hgrn_gnorm_g': gain(ks[10], (DEPTH, HGRN_V_DIM)),
        'w_out': nrm(ks[11], (DEPTH, MIX_WIDTH, D_MODEL), MIX_WIDTH ** -0.5),
        'norm2_g': gain(ks[12], (DEPTH, D_MODEL)),
        'peer_w_q': nrm(ks[13], (DEPTH, D_MODEL, PEER_HEADS * PEER_QUERY_DIM), D_MODEL ** -0.5),
        'peer_sub_keys': nrm(ks[14], (DEPTH, PEER_HEADS, 2, PEER_N_KEYS, PEER_QUERY_DIM // 2), (PEER_QUERY_DIM // 2) ** -0.5),
        'peer_u': nrm(ks[15], (DEPTH, PEER_N_EXPERTS, D_MODEL), D_MODEL ** -0.5),
        'peer_v': nrm(ks[16], (DEPTH, PEER_N_EXPERTS, D_MODEL), PEER_HEADS ** -0.5),
        'final_norm_g': gain(ks[17], (D_MODEL,)),
    }


def reference(x, norm1_g, w_in, diff_lambda_q1, diff_lambda_k1, diff_lambda_q2, diff_lambda_k2,
              diff_subln_g, rel_bias_table, hgrn_lb_logits, hgrn_gnorm_g, w_out, norm2_g,
              peer_w_q, peer_sub_keys, peer_u, peer_v, final_norm_g):
    B, S, _ = x.shape
    split_points = [int(p) for p in np.cumsum(IN_SPLIT_SIZES)[:-1]]
    lb_all = jnp.cumsum(jax.nn.softmax(hgrn_lb_logits.astype(jnp.float32), axis=0), axis=0)
    h = x
    for l in range(DEPTH):
        hn = rmsnorm(h, norm1_g[l])
        proj = hn @ w_in[l]
        dq, dk, dvv, hq, hf, hi, hg = jnp.split(proj, split_points, axis=-1)

        q = dq.reshape(B, S, DIFF_HEADS, 2, DIFF_QK_DIM).transpose(0, 2, 3, 1, 4)
        k = dk.reshape(B, S, DIFF_HEADS, 2, DIFF_QK_DIM).transpose(0, 2, 3, 1, 4)
        v = dvv.reshape(B, S, DIFF_HEADS, DIFF_V_DIM).transpose(0, 2, 1, 3)
        lam_init = 0.8 - 0.6 * math.exp(-0.3 * l)
        lam = (jnp.exp(jnp.sum(diff_lambda_q1[l].astype(jnp.float32) * diff_lambda_k1[l].astype(jnp.float32)))
               - jnp.exp(jnp.sum(diff_lambda_q2[l].astype(jnp.float32) * diff_lambda_k2[l].astype(jnp.float32)))
               + lam_init)
        a = diff_attention(q, k, v, lam, rel_bias_table)
        a = rmsnorm(a, diff_subln_g[l]) * (1.0 - lam_init)

        heads = lambda t: t.reshape(B, S, HGRN_HEADS, -1).transpose(0, 2, 1, 3).astype(jnp.float32)
        lb = lb_all[l].reshape(HGRN_HEADS, 1, HGRN_K_DIM)
        f = lb + (1.0 - lb) * jax.nn.sigmoid(heads(hf))
        o = hgrn2_chunked(jax.nn.silu(heads(hq)), 1.0 - f, heads(hi), jnp.log(f))
        o = o.transpose(0, 2, 1, 3)
        o = rmsnorm(o, hgrn_gnorm_g[l]) * jax.nn.silu(hg.reshape(B, S, HGRN_HEADS, HGRN_V_DIM).astype(jnp.float32))

        mixed = jnp.concatenate([a.reshape(B, S, DIFF_WIDTH).astype(x.dtype),
                                 o.reshape(B, S, HGRN_WIDTH).astype(x.dtype)], axis=-1)
        h = h + mixed @ w_out[l]

        hn2 = rmsnorm(h, norm2_g[l])
        h = h + peer(hn2, peer_w_q[l], peer_sub_keys[l], peer_u[l], peer_v[l])
    return rmsnorm(h, final_norm_g)
```

```python
import functools
import math

import numpy as np
import jax
import jax.numpy as jnp
from jax import lax
from jax.experimental import pallas as pl
from jax.experimental.pallas import tpu as pltpu

F32 = jnp.float32
BF16 = jnp.bfloat16

D_MODEL = 1024
DIFF_HEADS = 4
DIFF_QK_DIM = 64
HEAD_W = 128
DIFF_WIDTH = DIFF_HEADS * HEAD_W
HGRN_HEADS = 4
HGRN_WIDTH = HGRN_HEADS * HEAD_W
HGRN_CHUNK = 32
ATTN_COLS = 3 * DIFF_WIDTH
HGRN_COLS = 4 * HGRN_WIDTH
REL_BUCKETS = 32
REL_MAX_DIST = 128
PEER_HEADS = 8
PEER_N_KEYS = 128
PEER_N_EXPERTS = PEER_N_KEYS * PEER_N_KEYS
PEER_QUERY_DIM = 256
PEER_TOPK = 16
NORM_EPS = 1e-6
LAM_INIT = 0.8 - 0.6 * math.exp(-0.3 * 0)
NEG_BIG = -1e30

VMEM_LIMIT_BYTES = 56 * 1024 * 1024


def _cparams(sem):
    return pltpu.CompilerParams(dimension_semantics=sem, vmem_limit_bytes=VMEM_LIMIT_BYTES)


def _rms(x, g):
    return x * lax.rsqrt(jnp.mean(x * x, axis=-1, keepdims=True) + NORM_EPS) * g


def _norm_proj_kernel(x_ref, g_ref, w_ref, oa_ref, oh_ref):
    y = _rms(x_ref[...], g_ref[...])
    p = jnp.dot(y.astype(BF16), w_ref[...], preferred_element_type=F32)
    oa_ref[...] = p[:, :ATTN_COLS].astype(BF16)
    oh_ref[...] = p[:, ATTN_COLS:]


def _norm_proj(x2, g, w_bf16, tb):
    t, d = x2.shape
    n = w_bf16.shape[1]
    return pl.pallas_call(
        _norm_proj_kernel,
        grid=(t // tb,),
        in_specs=[pl.BlockSpec((tb, d), lambda i: (i, 0)),
                  pl.BlockSpec((1, d), lambda i: (0, 0)),
                  pl.BlockSpec((d, n), lambda i: (0, 0))],
        out_specs=[pl.BlockSpec((tb, ATTN_COLS), lambda i: (i, 0)),
                   pl.BlockSpec((tb, HGRN_COLS), lambda i: (i, 0))],
        out_shape=[jax.ShapeDtypeStruct((t, ATTN_COLS), BF16),
                   jax.ShapeDtypeStruct((t, HGRN_COLS), F32)],
        compiler_params=_cparams(("parallel",)),
        name="norm_proj",
    )(x2, g, w_bf16)


def _t5_bucket_np(n):
    n = np.maximum(n, 0).astype(np.int32)
    max_exact = REL_BUCKETS // 2
    nf = np.maximum(n, max_exact).astype(np.float32)
    large = max_exact + (np.log(nf / np.float32(max_exact)) / np.float32(math.log(REL_MAX_DIST / max_exact))
                         * np.float32(REL_BUCKETS - max_exact)).astype(np.int32)
    large = np.minimum(large, REL_BUCKETS - 1)
    return np.where(n < max_exact, n, large).astype(np.int32)


def _bias_tiles_kernel(table_ref, bucket_ref, o_ref):
    h = pl.program_id(0)
    bucket = bucket_ref[...]
    acc = jnp.zeros(bucket.shape, F32)
    for b in range(REL_BUCKETS):
        acc = jnp.where(bucket == b, table_ref[b, h], acc)
    o_ref[...] = acc


def _bias_tiles(table, tq):
    r = np.arange(tq)[:, None]
    c = np.arange(tq)[None, :]
    tiles = np.stack([_t5_bucket_np(d * tq + r - c) for d in (0, 1)])
    tiles = np.concatenate([tiles, tiles], axis=1)
    return pl.pallas_call(
        _bias_tiles_kernel,
        grid=(DIFF_HEADS,),
        in_specs=[pl.BlockSpec(memory_space=pltpu.SMEM),
                  pl.BlockSpec((2, 2 * tq, tq), lambda h: (0, 0, 0))],
        out_specs=pl.BlockSpec((None, 2, 2 * tq, tq), lambda h: (h, 0, 0, 0)),
        out_shape=jax.ShapeDtypeStruct((DIFF_HEADS, 2, 2 * tq, tq), F32),
        compiler_params=_cparams(("arbitrary",)),
        name="bias_tiles",
    )(table, jnp.asarray(tiles))


def _diff_attn_kernel(lam_ref, table_ref, q_ref, k_ref, v_ref, bias_ref, g_ref, o_ref, *, tq):
    h = pl.program_id(1)
    i = pl.program_id(2)
    q = q_ref[...] * jnp.asarray(DIFF_QK_DIM ** -0.5, BF16)
    lane = lax.broadcasted_iota(jnp.int32, q.shape, 1)
    zero = jnp.zeros_like(q)
    q2 = jnp.concatenate([jnp.where(lane < DIFF_QK_DIM, q, zero),
                          jnp.where(lane >= DIFF_QK_DIM, q, zero)], axis=0)
    far_bias = table_ref[REL_BUCKETS - 1, h]

    def step(j, carry, bias, causal):
        m, l, acc = carry
        start = pl.multiple_of(j * tq, tq)
        kj = k_ref[pl.ds(start, tq), :]
        vj = v_ref[pl.ds(start, tq), :]
        s = lax.dot_general(q2, kj, (((1,), (1,)), ((), ())), preferred_element_type=F32) + bias
        if causal:
            r = lax.broadcasted_iota(jnp.int32, s.shape, 0) % tq
            c = lax.broadcasted_iota(jnp.int32, s.shape, 1)
            s = jnp.where(c <= r, s, NEG_BIG)
        m_new = jnp.maximum(m, jnp.max(s, axis=-1, keepdims=True))
        scale = jnp.exp(m - m_new)
        e = jnp.exp(s - m_new)
        l = l * scale + jnp.sum(e, axis=-1, keepdims=True)
        acc = acc * scale + jnp.dot(e.astype(BF16), vj, preferred_element_type=F32)
        return m_new, l, acc

    carry = (jnp.full((2 * tq, 1), NEG_BIG, F32), jnp.zeros((2 * tq, 1), F32),
             jnp.zeros((2 * tq, HEAD_W), F32))
    n_far = jnp.maximum(i - 1, 0)
    carry = lax.fori_loop(0, n_far, lambda j, c: step(j, c, far_bias, False), carry)
    carry = lax.fori_loop(n_far, i, lambda j, c: step(j, c, bias_ref[1], False), carry)
    m, l, acc = step(i, carry, bias_ref[0], True)

    lam = lam_ref[0]
    out = acc[:tq] / l[:tq] - lam * (acc[tq:] / l[tq:])
    o_ref[...] = (_rms(out, g_ref[...]) * (1.0 - LAM_INIT)).astype(o_ref.dtype)


def _diff_attn(lam, table, qkv, bias, g, batch, seq, tq):
    kern = functools.partial(_diff_attn_kernel, tq=tq)
    return pl.pallas_call(
        kern,
        grid=(batch, DIFF_HEADS, seq // tq),
        in_specs=[pl.BlockSpec(memory_space=pltpu.SMEM),
                  pl.BlockSpec(memory_space=pltpu.SMEM),
                  pl.BlockSpec((None, tq, HEAD_W), lambda b, h, i: (b, i, h)),
                  pl.BlockSpec((None, seq, HEAD_W), lambda b, h, i: (b, 0, DIFF_HEADS + h)),
                  pl.BlockSpec((None, seq, HEAD_W), lambda b, h, i: (b, 0, 2 * DIFF_HEADS + h)),
                  pl.BlockSpec((None, 2, 2 * tq, tq), lambda b, h, i: (h, 0, 0, 0)),
                  pl.BlockSpec((1, HEAD_W), lambda b, h, i: (0, 0))],
        out_specs=pl.BlockSpec((None, tq, HEAD_W), lambda b, h, i: (b, i, h)),
        out_shape=jax.ShapeDtypeStruct((batch, seq, DIFF_WIDTH), BF16),
        compiler_params=_cparams(("parallel", "parallel", "arbitrary")),
        name="diff_attn",
    )(lam, table, qkv, qkv, qkv, bias, g)


def _hgrn_kernel(hq_ref, hf_ref, hi_ref, hg_ref, lb_ref, g_ref, o_ref,
                 qd_ref, kd_ref, cd_ref, oi_ref, *, seq):
    c_len = HGRN_CHUNK
    n_chunks = seq // c_len
    lb = lb_ref[...]
    f = lb + (1.0 - lb) * jax.nn.sigmoid(hf_ref[...])
    log_f = jnp.log(f)
    kk = 1.0 - f
    hq = hq_ref[...]
    qq = hq * jax.nn.sigmoid(hq)
    vv = hi_ref[...]

    pos = lax.broadcasted_iota(jnp.int32, (seq, HEAD_W), 0) % c_len
    b = log_f
    sh = 1
    while sh < c_len:
        b = b + jnp.where(pos >= sh, pltpu.roll(b, sh, axis=0), 0.0)
        sh *= 2
    b3 = b.reshape(n_chunks, c_len, HEAD_W)
    b_last = b3[:, c_len - 1:c_len, :]
    q_dec = qq * jnp.exp(b)
    k_inv = kk * jnp.exp(-b)
    k_dec = kk * jnp.exp(b_last - b3).reshape(seq, HEAD_W)
    qd_ref[...] = q_dec.astype(BF16)
    kd_ref[...] = k_dec.astype(BF16)
    cd_ref[...] = jnp.exp(b_last.reshape(n_chunks, HEAD_W))

    blk = HEAD_W
    nb = seq // blk
    qd_b = q_dec.astype(BF16).reshape(nb, blk, HEAD_W)
    ki_b = k_inv.astype(BF16).reshape(nb, blk, HEAD_W)
    scores = jnp.einsum('nqd,nkd->nqk', qd_b, ki_b, preferred_element_type=F32)
    r = lax.broadcasted_iota(jnp.int32, (blk, blk), 0)
    c = lax.broadcasted_iota(jnp.int32, (blk, blk), 1)
    keep = (c <= r) & ((r // c_len) == (c // c_len))
    scores = jnp.where(keep[None], scores, 0.0)
    o_intra = jnp.einsum('nqk,nkv->nqv', scores.astype(BF16), vv.astype(BF16).reshape(nb, blk, HEAD_W),
                         preferred_element_type=F32).reshape(seq, HEAD_W)

    def chunk(ci, st):
        start = pl.multiple_of(ci * c_len, c_len)
        qd = qd_ref[pl.ds(start, c_len), :]
        kd = kd_ref[pl.ds(start, c_len), :]
        vc = hi_ref[pl.ds(start, c_len), :].astype(BF16)
        oi_ref[pl.ds(start, c_len), :] = lax.dot_general(
            qd, st.astype(BF16), (((1,), (1,)), ((), ())), preferred_element_type=F32)
        upd = lax.dot_general(vc, kd, (((0,), (0,)), ((), ())), preferred_element_type=F32)
        return st * cd_ref[pl.ds(ci, 1), :] + upd

    lax.fori_loop(0, n_chunks, chunk, jnp.zeros((HEAD_W, HEAD_W), F32))

    o = o_intra + oi_ref[...]
    hg = hg_ref[...]
    o_ref[...] = (_rms(o, g_ref[...]) * (hg * jax.nn.sigmoid(hg))).astype(o_ref.dtype)


def _hgrn(hin, lb, g, batch, seq):
    kern = functools.partial(_hgrn_kernel, seq=seq)
    col = lambda k: (lambda b, h: (b, 0, k * HGRN_HEADS + h))
    blk = (None, seq, HEAD_W)
    return pl.pallas_call(
        kern,
        grid=(batch, HGRN_HEADS),
        in_specs=[pl.BlockSpec(blk, col(0)), pl.BlockSpec(blk, col(1)),
                  pl.BlockSpec(blk, col(2)), pl.BlockSpec(blk, col(3)),
                  pl.BlockSpec((None, 1, HEAD_W), lambda b, h: (h, 0, 0)),
                  pl.BlockSpec((1, HEAD_W), lambda b, h: (0, 0))],
        out_specs=pl.BlockSpec(blk, lambda b, h: (b, 0, h)),
        out_shape=jax.ShapeDtypeStruct((batch, seq, HGRN_WIDTH), BF16),
        scratch_shapes=[pltpu.VMEM((seq, HEAD_W), BF16), pltpu.VMEM((seq, HEAD_W), BF16),
                        pltpu.VMEM((seq // HGRN_CHUNK, HEAD_W), F32), pltpu.VMEM((seq, HEAD_W), F32)],
        compiler_params=_cparams(("parallel", "parallel")),
        name="hgrn",
    )(hin, hin, hin, hin, lb, g)


def _out_proj_kernel(x_ref, a_ref, o_ref, wa_ref, wo_ref, g_ref, h_ref, hnt_ref):
    h = (x_ref[...]
         + jnp.dot(a_ref[...], wa_ref[...], preferred_element_type=F32)
         + jnp.dot(o_ref[...], wo_ref[...], preferred_element_type=F32))
    h_ref[...] = h
    hnt_ref[...] = _rms(h, g_ref[...]).T.astype(BF16)


def _out_proj(x2, a2, o2, wa, wo, g, tb):
    t, d = x2.shape
    return pl.pallas_call(
        _out_proj_kernel,
        grid=(t // tb,),
        in_specs=[pl.BlockSpec((tb, d), lambda i: (i, 0)),
                  pl.BlockSpec((tb, DIFF_WIDTH), lambda i: (i, 0)),
                  pl.BlockSpec((tb, HGRN_WIDTH), lambda i: (i, 0)),
                  pl.BlockSpec((DIFF_WIDTH, d), lambda i: (0, 0)),
                  pl.BlockSpec((HGRN_WIDTH, d), lambda i: (0, 0)),
                  pl.BlockSpec((1, d), lambda i: (0, 0))],
        out_specs=[pl.BlockSpec((tb, d), lambda i: (i, 0)),
                   pl.BlockSpec((d, tb), lambda i: (0, i))],
        out_shape=[jax.ShapeDtypeStruct((t, d), F32),
                   jax.ShapeDtypeStruct((d, t), BF16)],
        compiler_params=_cparams(("parallel",)),
        name="out_proj",
    )(x2, a2, o2, wa, wo, g)


def _top_values(x, k):
    rows = []
    for _ in range(k):
        m = jnp.max(x, axis=0, keepdims=True)
        rows.append(m)
        x = jnp.where(x == m, -jnp.inf, x)
    return rows


def _peer_route_kernel(hnt_ref, wqt_ref, keys_ref, tau_ref, alpha_ref, s2_ref, beta_ref, qt_ref):
    qt_ref[...] = jnp.dot(wqt_ref[...], hnt_ref[...], preferred_element_type=F32).astype(BF16)
    half = PEER_QUERY_DIM // 2

    def head(h, _):
        base = pl.multiple_of(h * PEER_QUERY_DIM, PEER_QUERY_DIM)
        s1 = jnp.dot(keys_ref[h, 0], qt_ref[pl.ds(base, half), :], preferred_element_type=F32)
        s2 = jnp.dot(keys_ref[h, 1], qt_ref[pl.ds(base + half, half), :], preferred_element_type=F32)
        a = _top_values(s1, PEER_TOPK)
        b = _top_values(s2, PEER_TOPK)
        bmat = jnp.concatenate(b, axis=0)
        cands = jnp.concatenate([a[i] + bmat[:PEER_TOPK // (i + 1)] for i in range(PEER_TOPK)], axis=0)
        thr = _top_values(cands, PEER_TOPK)[-1]
        top = a[0] + b[0]
        z = jnp.sum(jnp.where(cands >= thr, jnp.exp(cands - top), 0.0), axis=0, keepdims=True)
        tau_ref[h] = thr - s1
        alpha_ref[h] = jnp.exp(s1 - a[0]) / z
        s2_ref[h] = s2
        beta_ref[h] = jnp.exp(s2 - b[0])
        return 0

    lax.fori_loop(0, PEER_HEADS, head, 0)


def _peer_route(hnt, wqt, keys, tb):
    d, t = hnt.shape
    nq = wqt.shape[0]
    oshape = jax.ShapeDtypeStruct((PEER_HEADS, PEER_N_KEYS, t), F32)
    ospec = pl.BlockSpec((PEER_HEADS, PEER_N_KEYS, tb), lambda i: (0, 0, i))
    return pl.pallas_call(
        _peer_route_kernel,
        grid=(t // tb,),
        in_specs=[pl.BlockSpec((d, tb), lambda i: (0, i)),
                  pl.BlockSpec((nq, d), lambda i: (0, 0)),
                  pl.BlockSpec(keys.shape, lambda i: (0, 0, 0, 0))],
        out_specs=[ospec, ospec, ospec, ospec],
        out_shape=[oshape, oshape, oshape, oshape],
        scratch_shapes=[pltpu.VMEM((nq, tb), BF16)],
        compiler_params=_cparams(("parallel",)),
        name="peer_route",
    )(hnt, wqt, keys)


def _peer_dense_kernel(hnt_ref, u_ref, vt_ref, tau_ref, alpha_ref, s2_ref, beta_ref, o_ref,
                       hid_ref, act_ref, *, te, tb):
    j = pl.program_id(1)
    hid_ref[...] = jnp.dot(u_ref[...], hnt_ref[...], preferred_element_type=F32)
    rows_per_tile = te // PEER_N_KEYS
    lane_tiles = tb // 128

    def tile(lt, _, r):
        row0 = r * PEER_N_KEYS
        col0 = pl.multiple_of(lt * 128, 128)
        w = jnp.zeros((PEER_N_KEYS, 128), F32)
        for h in range(PEER_HEADS):
            tau = tau_ref[h, r:r + 1, pl.ds(col0, 128)]
            alpha = alpha_ref[h, r:r + 1, pl.ds(col0, 128)]
            sel = s2_ref[h, :, pl.ds(col0, 128)] >= tau
            w = w + jnp.where(sel, beta_ref[h, :, pl.ds(col0, 128)], 0.0) * alpha
        hid = hid_ref[row0:row0 + PEER_N_KEYS, pl.ds(col0, 128)]
        act = 0.5 * hid * (1.0 + lax.erf(hid * (2.0 ** -0.5))) * w
        act_ref[row0:row0 + PEER_N_KEYS, pl.ds(col0, 128)] = act.astype(BF16)
        return 0

    for r in range(rows_per_tile):
        lax.fori_loop(0, lane_tiles, functools.partial(tile, r=r), 0)
    contrib = jnp.dot(vt_ref[...], act_ref[...], preferred_element_type=F32)

    @pl.when(j == 0)
    def _():
        o_ref[...] = contrib

    @pl.when(j != 0)
    def _():
        o_ref[...] += contrib


def _peer_dense(hnt, u_bf16, vt_bf16, tau, alpha, s2, beta, tb, te):
    d, t = hnt.shape
    ne = u_bf16.shape[0]
    kern = functools.partial(_peer_dense_kernel, te=te, tb=tb)
    rspec = pl.BlockSpec((PEER_HEADS, PEER_N_KEYS, tb), lambda i, j: (0, 0, i))
    i1spec = pl.BlockSpec((PEER_HEADS, te // PEER_N_KEYS, tb), lambda i, j: (0, j, i))
    return pl.pallas_call(
        kern,
        grid=(t // tb, ne // te),
        in_specs=[pl.BlockSpec((d, tb), lambda i, j: (0, i)),
                  pl.BlockSpec((te, d), lambda i, j: (j, 0)),
                  pl.BlockSpec((d, te), lambda i, j: (0, j)),
                  i1spec, i1spec, rspec, rspec],
        out_specs=pl.BlockSpec((d, tb), lambda i, j: (0, i)),
        out_shape=jax.ShapeDtypeStruct((d, t), F32),
        scratch_shapes=[pltpu.VMEM((te, tb), F32), pltpu.VMEM((te, tb), BF16)],
        compiler_params=_cparams(("parallel", "arbitrary")),
        name="peer_dense",
    )(hnt, u_bf16, vt_bf16, tau, alpha, s2, beta)


def _final_norm_kernel(h_ref, pt_ref, g_ref, o_ref):
    o_ref[...] = _rms(h_ref[...] + pt_ref[...].T, g_ref[...])


def _final_norm(h2, peer_t, g, tb):
    t, d = h2.shape
    return pl.pallas_call(
        _final_norm_kernel,
        grid=(t // tb,),
        in_specs=[pl.BlockSpec((tb, d), lambda i: (i, 0)),
                  pl.BlockSpec((d, tb), lambda i: (0, i)),
                  pl.BlockSpec((1, d), lambda i: (0, 0))],
        out_specs=pl.BlockSpec((tb, d), lambda i: (i, 0)),
        out_shape=jax.ShapeDtypeStruct((t, d), F32),
        compiler_params=_cparams(("parallel",)),
        name="final_norm",
    )(h2, peer_t, g)


def _largest_divisor(n, cap, multiple):
    best = multiple
    k = multiple
    while k <= min(n, cap):
        if n % k == 0:
            best = k
        k += multiple
    return best


def kernel(x, norm1_g, w_in, diff_lambda_q1, diff_lambda_k1, diff_lambda_q2, diff_lambda_k2, diff_subln_g,
           rel_bias_table, hgrn_lb_logits, hgrn_gnorm_g, w_out, norm2_g, peer_w_q, peer_sub_keys, peer_u,
           peer_v, final_norm_g):
    batch, seq, d = x.shape
    t = batch * seq
    assert d == D_MODEL and seq % 256 == 0 and w_in.shape[0] == 1
    tb_proj = _largest_divisor(t, 512, 256)
    tq = 256
    tb_route = _largest_divisor(t, 512, 256)
    tb_peer = _largest_divisor(t, 512, 256)
    te_peer = 1024

    x2 = x.reshape(t, d)
    row = lambda v: v.reshape(1, -1).astype(F32)

    lam = (jnp.exp(jnp.sum(diff_lambda_q1[0].astype(F32) * diff_lambda_k1[0].astype(F32)))
           - jnp.exp(jnp.sum(diff_lambda_q2[0].astype(F32) * diff_lambda_k2[0].astype(F32)))
           + LAM_INIT).reshape(1)
    lb = jnp.cumsum(jax.nn.softmax(hgrn_lb_logits.astype(F32), axis=0), axis=0)[0]
    lb = lb.reshape(HGRN_HEADS, 1, HEAD_W)
    table = rel_bias_table.astype(F32)

    qkv, hin = _norm_proj(x2, row(norm1_g[0]), w_in[0].astype(BF16), tb_proj)
    bias = _bias_tiles(table, tq)
    a = _diff_attn(lam, table, qkv.reshape(batch, seq, ATTN_COLS), bias, row(diff_subln_g[0]), batch, seq, tq)
    o = _hgrn(hin.reshape(batch, seq, HGRN_COLS), lb, row(hgrn_gnorm_g[0]), batch, seq)
    w_o = w_out[0].astype(BF16)
    h2, hnt = _out_proj(x2, a.reshape(t, DIFF_WIDTH), o.reshape(t, HGRN_WIDTH),
                        w_o[:DIFF_WIDTH], w_o[DIFF_WIDTH:], row(norm2_g[0]), tb_proj)
    tau, alpha, s2, beta = _peer_route(hnt, peer_w_q[0].T.astype(BF16), peer_sub_keys[0].astype(BF16), tb_route)
    peer_t = _peer_dense(hnt, peer_u[0].astype(BF16), peer_v[0].T.astype(BF16), tau, alpha, s2, beta,
                         tb_peer, te_peer)
    out = _final_norm(h2, peer_t, row(final_norm_g), tb_proj)
    return out.reshape(batch, seq, d)
```

```python
import functools
import math

import numpy as np
import jax
import jax.numpy as jnp
from jax import lax
from jax.experimental import pallas as pl
from jax.experimental.pallas import tpu as pltpu

F32 = jnp.float32
BF16 = jnp.bfloat16

D_MODEL = 1024
DIFF_HEADS = 4
DIFF_QK_DIM = 64
LANES = 128
MXU_WIDTH = 256
HEAD_W = 128
DIFF_WIDTH = DIFF_HEADS * HEAD_W
HGRN_HEADS = 4
HGRN_WIDTH = HGRN_HEADS * HEAD_W
HGRN_CHUNK = 32
ATTN_COLS = 3 * DIFF_WIDTH
HGRN_COLS = 4 * HGRN_WIDTH
REL_BUCKETS = 32
REL_MAX_DIST = 128
PEER_HEADS = 8
PEER_N_KEYS = 128
PEER_N_EXPERTS = PEER_N_KEYS * PEER_N_KEYS
PEER_QUERY_DIM = 256
PEER_TOPK = 16
NORM_EPS = 1e-6
LAM_INIT = 0.8 - 0.6 * math.exp(-0.3 * 0)
NEG_BIG = -1e30

VMEM_LIMIT_BYTES = 56 * 1024 * 1024


def _cparams(sem):
    return pltpu.CompilerParams(dimension_semantics=sem, vmem_limit_bytes=VMEM_LIMIT_BYTES)


def _rms(x, g):
    return x * lax.rsqrt(jnp.mean(x * x, axis=-1, keepdims=True) + NORM_EPS) * g


def _norm_proj_kernel(x_ref, g_ref, w_ref, oa_ref, oh_ref):
    y = _rms(x_ref[...], g_ref[...])
    p = jnp.dot(y.astype(BF16), w_ref[...], preferred_element_type=F32)
    oa_ref[...] = p[:, :ATTN_COLS].astype(BF16)
    oh_ref[...] = p[:, ATTN_COLS:]


def _norm_proj(x2, g, w_bf16, tb):
    t, d = x2.shape
    n = w_bf16.shape[1]
    return pl.pallas_call(
        _norm_proj_kernel,
        grid=(t // tb,),
        in_specs=[pl.BlockSpec((tb, d), lambda i: (i, 0)),
                  pl.BlockSpec((1, d), lambda i: (0, 0)),
                  pl.BlockSpec((d, n), lambda i: (0, 0))],
        out_specs=[pl.BlockSpec((tb, ATTN_COLS), lambda i: (i, 0)),
                   pl.BlockSpec((tb, HGRN_COLS), lambda i: (i, 0))],
        out_shape=[jax.ShapeDtypeStruct((t, ATTN_COLS), BF16),
                   jax.ShapeDtypeStruct((t, HGRN_COLS), F32)],
        compiler_params=_cparams(("parallel",)),
        name="norm_proj",
    )(x2, g, w_bf16)


def _t5_bucket_np(n):
    n = np.maximum(n, 0).astype(np.int32)
    max_exact = REL_BUCKETS // 2
    nf = np.maximum(n, max_exact).astype(np.float32)
    large = max_exact + (np.log(nf / np.float32(max_exact)) / np.float32(math.log(REL_MAX_DIST / max_exact))
                         * np.float32(REL_BUCKETS - max_exact)).astype(np.int32)
    large = np.minimum(large, REL_BUCKETS - 1)
    return np.where(n < max_exact, n, large).astype(np.int32)


def _bias_tiles_kernel(table_ref, bucket_ref, o_ref):
    h = pl.program_id(0)
    bucket = bucket_ref[...]
    acc = jnp.zeros(bucket.shape, F32)
    for b in range(REL_BUCKETS):
        acc = jnp.where(bucket == b, table_ref[b, h], acc)
    o_ref[...] = acc


def _bias_tiles(table, tq):
    r = np.arange(tq)[:, None]
    c = np.arange(tq)[None, :]
    tiles = np.stack([_t5_bucket_np(d * tq + r - c) for d in (0, 1)])
    tiles = np.concatenate([tiles, tiles], axis=1)
    return pl.pallas_call(
        _bias_tiles_kernel,
        grid=(DIFF_HEADS,),
        in_specs=[pl.BlockSpec(memory_space=pltpu.SMEM),
                  pl.BlockSpec((2, 2 * tq, tq), lambda h: (0, 0, 0))],
        out_specs=pl.BlockSpec((None, 2, 2 * tq, tq), lambda h: (h, 0, 0, 0)),
        out_shape=jax.ShapeDtypeStruct((DIFF_HEADS, 2, 2 * tq, tq), F32),
        compiler_params=_cparams(("arbitrary",)),
        name="bias_tiles",
    )(table, jnp.asarray(tiles))


def _diff_attn_kernel(lam_ref, table_ref, q_ref, k_ref, v_ref, bias_ref, g_ref, o_ref, *, tq):
    h = pl.program_id(1)
    i = pl.program_id(2)
    q = q_ref[...] * jnp.asarray(DIFF_QK_DIM ** -0.5, BF16)
    lane = lax.broadcasted_iota(jnp.int32, q.shape, 1)
    zero = jnp.zeros_like(q)
    q2 = jnp.concatenate([jnp.where(lane < DIFF_QK_DIM, q, zero),
                          jnp.where(lane >= DIFF_QK_DIM, q, zero)], axis=0)
    far_bias = table_ref[REL_BUCKETS - 1, h]

    def step(j, carry, bias, causal):
        m, l, acc = carry
        start = pl.multiple_of(j * tq, tq)
        kj = k_ref[pl.ds(start, tq), :]
        vj = v_ref[pl.ds(start, tq), :]
        s = lax.dot_general(q2, kj, (((1,), (1,)), ((), ())), preferred_element_type=F32) + bias
        if causal:
            r = lax.broadcasted_iota(jnp.int32, s.shape, 0) % tq
            c = lax.broadcasted_iota(jnp.int32, s.shape, 1)
            s = jnp.where(c <= r, s, NEG_BIG)
        m_new = jnp.maximum(m, jnp.max(s, axis=-1, keepdims=True))
        scale = jnp.exp(m - m_new)
        e = jnp.exp(s - m_new)
        l = l * scale + jnp.sum(e, axis=-1, keepdims=True)
        acc = acc * scale + jnp.dot(e.astype(BF16), vj, preferred_element_type=F32)
        return m_new, l, acc

    carry = (jnp.full((2 * tq, 1), NEG_BIG, F32), jnp.zeros((2 * tq, 1), F32),
             jnp.zeros((2 * tq, HEAD_W), F32))
    n_far = jnp.maximum(i - 1, 0)
    carry = lax.fori_loop(0, n_far, lambda j, c: step(j, c, far_bias, False), carry)
    carry = lax.fori_loop(n_far, i, lambda j, c: step(j, c, bias_ref[1], False), carry)
    m, l, acc = step(i, carry, bias_ref[0], True)

    lam = lam_ref[0]
    out = acc[:tq] / l[:tq] - lam * (acc[tq:] / l[tq:])
    o_ref[...] = (_rms(out, g_ref[...]) * (1.0 - LAM_INIT)).astype(o_ref.dtype)


def _diff_attn(lam, table, qkv, bias, g, batch, seq, tq):
    kern = functools.partial(_diff_attn_kernel, tq=tq)
    return pl.pallas_call(
        kern,
        grid=(batch, DIFF_HEADS, seq // tq),
        in_specs=[pl.BlockSpec(memory_space=pltpu.SMEM),
                  pl.BlockSpec(memory_space=pltpu.SMEM),
                  pl.BlockSpec((None, tq, HEAD_W), lambda b, h, i: (b, i, h)),
                  pl.BlockSpec((None, seq, HEAD_W), lambda b, h, i: (b, 0, DIFF_HEADS + h)),
                  pl.BlockSpec((None, seq, HEAD_W), lambda b, h, i: (b, 0, 2 * DIFF_HEADS + h)),
                  pl.BlockSpec((None, 2, 2 * tq, tq), lambda b, h, i: (h, 0, 0, 0)),
                  pl.BlockSpec((1, HEAD_W), lambda b, h, i: (0, 0))],
        out_specs=pl.BlockSpec((None, tq, HEAD_W), lambda b, h, i: (b, i, h)),
        out_shape=jax.ShapeDtypeStruct((batch, seq, DIFF_WIDTH), BF16),
        compiler_params=_cparams(("parallel", "parallel", "arbitrary")),
        name="diff_attn",
    )(lam, table, qkv, qkv, qkv, bias, g)


def _hgrn_kernel(hq_ref, hf_ref, hi_ref, hg_ref, lb_ref, g_ref, o_ref,
                 qd_ref, kd_ref, cd_ref, oi_ref, *, seq):
    c_len = HGRN_CHUNK
    n_chunks = seq // c_len
    lb = lb_ref[...]
    f = lb + (1.0 - lb) * jax.nn.sigmoid(hf_ref[...])
    log_f = jnp.log(f)
    kk = 1.0 - f
    hq = hq_ref[...]
    qq = hq * jax.nn.sigmoid(hq)
    vv = hi_ref[...]

    pos = lax.broadcasted_iota(jnp.int32, (seq, HEAD_W), 0) % c_len
    b = log_f
    sh = 1
    while sh < c_len:
        b = b + jnp.where(pos >= sh, pltpu.roll(b, sh, axis=0), 0.0)
        sh *= 2
    b3 = b.reshape(n_chunks, c_len, HEAD_W)
    b_last = b3[:, c_len - 1:c_len, :]
    q_dec = qq * jnp.exp(b)
    k_inv = kk * jnp.exp(-b)
    k_dec = kk * jnp.exp(b_last - b3).reshape(seq, HEAD_W)
    qd_ref[...] = q_dec.astype(BF16)
    kd_ref[...] = k_dec.astype(BF16)
    cd_ref[...] = jnp.exp(b_last.reshape(n_chunks, HEAD_W))

    blk = HEAD_W
    nb = seq // blk
    qd_b = q_dec.astype(BF16).reshape(nb, blk, HEAD_W)
    ki_b = k_inv.astype(BF16).reshape(nb, blk, HEAD_W)
    scores = jnp.einsum('nqd,nkd->nqk', qd_b, ki_b, preferred_element_type=F32)
    r = lax.broadcasted_iota(jnp.int32, (blk, blk), 0)
    c = lax.broadcasted_iota(jnp.int32, (blk, blk), 1)
    keep = (c <= r) & ((r // c_len) == (c // c_len))
    scores = jnp.where(keep[None], scores, 0.0)
    o_intra = jnp.einsum('nqk,nkv->nqv', scores.astype(BF16), vv.astype(BF16).reshape(nb, blk, HEAD_W),
                         preferred_element_type=F32).reshape(seq, HEAD_W)

    def chunk(ci, st):
        start = pl.multiple_of(ci * c_len, c_len)
        qd = qd_ref[pl.ds(start, c_len), :]
        kd = kd_ref[pl.ds(start, c_len), :]
        vc = hi_ref[pl.ds(start, c_len), :].astype(BF16)
        oi_ref[pl.ds(start, c_len), :] = lax.dot_general(
            qd, st.astype(BF16), (((1,), (1,)), ((), ())), preferred_element_type=F32)
        upd = lax.dot_general(vc, kd, (((0,), (0,)), ((), ())), preferred_element_type=F32)
        return st * cd_ref[pl.ds(ci, 1), :] + upd

    lax.fori_loop(0, n_chunks, chunk, jnp.zeros((HEAD_W, HEAD_W), F32))

    o = o_intra + oi_ref[...]
    hg = hg_ref[...]
    o_ref[...] = (_rms(o, g_ref[...]) * (hg * jax.nn.sigmoid(hg))).astype(o_ref.dtype)


def _hgrn(hin, lb, g, batch, seq):
    kern = functools.partial(_hgrn_kernel, seq=seq)
    col = lambda k: (lambda b, h: (b, 0, k * HGRN_HEADS + h))
    blk = (None, seq, HEAD_W)
    return pl.pallas_call(
        kern,
        grid=(batch, HGRN_HEADS),
        in_specs=[pl.BlockSpec(blk, col(0)), pl.BlockSpec(blk, col(1)),
                  pl.BlockSpec(blk, col(2)), pl.BlockSpec(blk, col(3)),
                  pl.BlockSpec((None, 1, HEAD_W), lambda b, h: (h, 0, 0)),
                  pl.BlockSpec((1, HEAD_W), lambda b, h: (0, 0))],
        out_specs=pl.BlockSpec(blk, lambda b, h: (b, 0, h)),
        out_shape=jax.ShapeDtypeStruct((batch, seq, HGRN_WIDTH), BF16),
        scratch_shapes=[pltpu.VMEM((seq, HEAD_W), BF16), pltpu.VMEM((seq, HEAD_W), BF16),
                        pltpu.VMEM((seq // HGRN_CHUNK, HEAD_W), F32), pltpu.VMEM((seq, HEAD_W), F32)],
        compiler_params=_cparams(("parallel", "parallel")),
        name="hgrn",
    )(hin, hin, hin, hin, lb, g)


def _out_proj_kernel(x_ref, a_ref, o_ref, wa_ref, wo_ref, g_ref, h_ref, hnt_ref):
    h = (x_ref[...]
         + jnp.dot(a_ref[...], wa_ref[...], preferred_element_type=F32)
         + jnp.dot(o_ref[...], wo_ref[...], preferred_element_type=F32))
    h_ref[...] = h
    hnt_ref[...] = _rms(h, g_ref[...]).T.astype(BF16)


def _out_proj(x2, a2, o2, wa, wo, g, tb):
    t, d = x2.shape
    return pl.pallas_call(
        _out_proj_kernel,
        grid=(t // tb,),
        in_specs=[pl.BlockSpec((tb, d), lambda i: (i, 0)),
                  pl.BlockSpec((tb, DIFF_WIDTH), lambda i: (i, 0)),
                  pl.BlockSpec((tb, HGRN_WIDTH), lambda i: (i, 0)),
                  pl.BlockSpec((DIFF_WIDTH, d), lambda i: (0, 0)),
                  pl.BlockSpec((HGRN_WIDTH, d), lambda i: (0, 0)),
                  pl.BlockSpec((1, d), lambda i: (0, 0))],
        out_specs=[pl.BlockSpec((tb, d), lambda i: (i, 0)),
                   pl.BlockSpec((d, tb), lambda i: (0, i))],
        out_shape=[jax.ShapeDtypeStruct((t, d), F32),
                   jax.ShapeDtypeStruct((d, t), BF16)],
        compiler_params=_cparams(("parallel",)),
        name="out_proj",
    )(x2, a2, o2, wa, wo, g)


def _top_values(x, k):
    rows = []
    rank = jnp.full(x.shape, float(k), F32)
    for r in range(k):
        m = jnp.max(x, axis=0, keepdims=True)
        rows.append(m)
        hit = x == m
        rank = jnp.where(hit, float(r), rank)
        x = jnp.where(hit, -jnp.inf, x)
    return rows, rank


def _peer_route_kernel(hnt_ref, wqt_ref, keys_ref, r1_ref, alpha_ref, m_ref, beta_ref, qt_ref):
    qt_ref[...] = jnp.dot(wqt_ref[...], hnt_ref[...], preferred_element_type=F32).astype(BF16)
    half = PEER_QUERY_DIM // 2

    def head(h, _):
        base = pl.multiple_of(h * PEER_QUERY_DIM, PEER_QUERY_DIM)
        s1 = jnp.dot(keys_ref[h, 0], qt_ref[pl.ds(base, half), :], preferred_element_type=F32)
        s2 = jnp.dot(keys_ref[h, 1], qt_ref[pl.ds(base + half, half), :], preferred_element_type=F32)
        a, rank1 = _top_values(s1, PEER_TOPK)
        b, rank2 = _top_values(s2, PEER_TOPK)
        amat = jnp.concatenate(a, axis=0)
        bmat = jnp.concatenate(b, axis=0)
        cands = jnp.concatenate([a[i] + bmat[:PEER_TOPK // (i + 1)] for i in range(PEER_TOPK)], axis=0)
        thr = _top_values(cands, PEER_TOPK)[0][-1]
        top = a[0] + b[0]
        z = jnp.sum(jnp.where(cands >= thr, jnp.exp(cands - top), 0.0), axis=0, keepdims=True)
        m = jnp.zeros(s2.shape, F32)
        for l in range(PEER_TOPK):
            n_sel = jnp.sum(jnp.where(amat + b[l] >= thr, 1.0, 0.0), axis=0, keepdims=True)
            m = jnp.where(rank2 == float(l), n_sel, m)
        r1_ref[h] = rank1
        alpha_ref[h] = 0.5 * jnp.exp(s1 - a[0]) / z
        beta = jnp.exp(s2 - b[0])
        for lt in range(s2.shape[1] // LANES):
            m_ref[h, lt] = m[:, lt * LANES:(lt + 1) * LANES].astype(BF16)
            beta_ref[h, lt] = beta[:, lt * LANES:(lt + 1) * LANES].astype(BF16)
        return 0

    lax.fori_loop(0, PEER_HEADS, head, 0)


def _peer_route(hnt, wqt, keys, tb):
    d, t = hnt.shape
    nq = wqt.shape[0]
    oshape = jax.ShapeDtypeStruct((PEER_HEADS, PEER_N_KEYS, t), F32)
    ospec = pl.BlockSpec((PEER_HEADS, PEER_N_KEYS, tb), lambda i: (0, 0, i))
    tshape = jax.ShapeDtypeStruct((PEER_HEADS, t // LANES, PEER_N_KEYS, LANES), BF16)
    tspec = pl.BlockSpec((PEER_HEADS, tb // LANES, PEER_N_KEYS, LANES), lambda i: (0, i, 0, 0))
    return pl.pallas_call(
        _peer_route_kernel,
        grid=(t // tb,),
        in_specs=[pl.BlockSpec((d, tb), lambda i: (0, i)),
                  pl.BlockSpec((nq, d), lambda i: (0, 0)),
                  pl.BlockSpec(keys.shape, lambda i: (0, 0, 0, 0))],
        out_specs=[ospec, ospec, tspec, tspec],
        out_shape=[oshape, oshape, tshape, tshape],
        scratch_shapes=[pltpu.VMEM((nq, tb), BF16)],
        compiler_params=_cparams(("parallel",)),
        name="peer_route",
    )(hnt, wqt, keys)


def _peer_dense_kernel(hnt_ref, u_ref, vt_ref, r1_ref, alpha_ref, r1n_ref, alphan_ref, m_ref, beta_ref, o_ref,
                       hid_ref, act_ref, w_ref, *, te, tb):
    j = pl.program_id(1)
    slot = j % 2
    rows_per_tile = te // PEER_N_KEYS
    lane_tiles = tb // LANES
    packed_rows = 16
    group = MXU_WIDTH // LANES
    n_groups = lane_tiles // group

    def row_bcast(ref, h, r, lt):
        tile16 = jnp.broadcast_to(ref[h, r:r + 1, lt * LANES:(lt + 1) * LANES], (packed_rows, LANES)).astype(BF16)
        return jnp.broadcast_to(tile16[None], (PEER_N_KEYS // packed_rows, packed_rows, LANES)
                                ).reshape(PEER_N_KEYS, LANES)

    def hidden(g):
        cols = slice(g * MXU_WIDTH, (g + 1) * MXU_WIDTH)
        hid = jnp.dot(u_ref[...], hnt_ref[:, cols], preferred_element_type=F32)
        for k in range(group):
            hid_ref[slot, g * group + k] = hid[:, k * LANES:(k + 1) * LANES]

    zero = jnp.zeros((PEER_N_KEYS, LANES), BF16)

    def routing_weights(rank_ref, gate_ref, dst_slot):
        for lt in range(lane_tiles):
            for r in range(rows_per_tile):
                w = None
                for h in range(PEER_HEADS):
                    sel = row_bcast(rank_ref, h, r, lt) < m_ref[h, lt]
                    term = jnp.where(sel, beta_ref[h, lt], zero) * row_bcast(gate_ref, h, r, lt)
                    w = term if w is None else w + term
                w_ref[dst_slot, lt, r * PEER_N_KEYS:(r + 1) * PEER_N_KEYS, :] = w

    def activations(g):
        for lt in range(g * group, (g + 1) * group):
            hid_t = hid_ref[slot, lt]
            gelu2 = hid_t * (1.0 + lax.erf(hid_t * (2.0 ** -0.5)))
            act_ref[slot, lt] = gelu2.astype(BF16) * w_ref[slot, lt]

    def project(g):
        cols = slice(g * MXU_WIDTH, (g + 1) * MXU_WIDTH)
        act_g = jnp.concatenate([act_ref[slot, g * group + k] for k in range(group)], axis=1)
        o_ref[:, cols] += jnp.dot(vt_ref[...], act_g, preferred_element_type=F32)

    @pl.when(j == 0)
    def _():
        o_ref[...] = jnp.zeros(o_ref.shape, o_ref.dtype)
        routing_weights(r1_ref, alpha_ref, 0)

    routing_weights(r1n_ref, alphan_ref, 1 - slot)
    hidden(0)
    for g in range(n_groups):
        if g + 1 < n_groups:
            hidden(g + 1)
        activations(g)
        project(g)


def _peer_dense(hnt, u_bf16, vt_bf16, rank1, alpha, n_sel, beta, tb, te):
    d, t = hnt.shape
    ne = u_bf16.shape[0]
    kern = functools.partial(_peer_dense_kernel, te=te, tb=tb)
    rspec = pl.BlockSpec((PEER_HEADS, tb // LANES, PEER_N_KEYS, LANES), lambda i, j: (0, i, 0, 0))
    n_tiles = ne // te
    i1spec = pl.BlockSpec((PEER_HEADS, te // PEER_N_KEYS, tb), lambda i, j: (0, j, i))
    i1next = pl.BlockSpec((PEER_HEADS, te // PEER_N_KEYS, tb), lambda i, j: (0, jnp.minimum(j + 1, n_tiles - 1), i))
    tile_shape = (tb // LANES, te, LANES)
    return pl.pallas_call(
        kern,
        grid=(t // tb, n_tiles),
        in_specs=[pl.BlockSpec((d, tb), lambda i, j: (0, i)),
                  pl.BlockSpec((te, d), lambda i, j: (j, 0)),
                  pl.BlockSpec((d, te), lambda i, j: (0, j)),
                  i1spec, i1spec, i1next, i1next, rspec, rspec],
        out_specs=pl.BlockSpec((d, tb), lambda i, j: (0, i)),
        out_shape=jax.ShapeDtypeStruct((d, t), F32),
        scratch_shapes=[pltpu.VMEM((2,) + tile_shape, F32), pltpu.VMEM((2,) + tile_shape, BF16),
                        pltpu.VMEM((2,) + tile_shape, BF16)],
        compiler_params=_cparams(("parallel", "arbitrary")),
        name="peer_dense",
    )(hnt, u_bf16, vt_bf16, rank1, alpha, rank1, alpha, n_sel, beta)


def _final_norm_kernel(h_ref, pt_ref, g_ref, o_ref):
    o_ref[...] = _rms(h_ref[...] + pt_ref[...].T, g_ref[...])


def _final_norm(h2, peer_t, g, tb):
    t, d = h2.shape
    return pl.pallas_call(
        _final_norm_kernel,
        grid=(t // tb,),
        in_specs=[pl.BlockSpec((tb, d), lambda i: (i, 0)),
                  pl.BlockSpec((d, tb), lambda i: (0, i)),
                  pl.BlockSpec((1, d), lambda i: (0, 0))],
        out_specs=pl.BlockSpec((tb, d), lambda i: (i, 0)),
        out_shape=jax.ShapeDtypeStruct((t, d), F32),
        compiler_params=_cparams(("parallel",)),
        name="final_norm",
    )(h2, peer_t, g)


def _largest_divisor(n, cap, multiple):
    best = multiple
    k = multiple
    while k <= min(n, cap):
        if n % k == 0:
            best = k
        k += multiple
    return best


def kernel(x, norm1_g, w_in, diff_lambda_q1, diff_lambda_k1, diff_lambda_q2, diff_lambda_k2, diff_subln_g,
           rel_bias_table, hgrn_lb_logits, hgrn_gnorm_g, w_out, norm2_g, peer_w_q, peer_sub_keys, peer_u,
           peer_v, final_norm_g):
    batch, seq, d = x.shape
    t = batch * seq
    assert d == D_MODEL and seq % 256 == 0 and w_in.shape[0] == 1
    tb_proj = _largest_divisor(t, 512, 256)
    tq = 256
    tb_route = _largest_divisor(t, 512, 256)
    tb_peer = _largest_divisor(t, 512, 256)
    te_peer = 1024

    x2 = x.reshape(t, d)
    row = lambda v: v.reshape(1, -1).astype(F32)

    lam = (jnp.exp(jnp.sum(diff_lambda_q1[0].astype(F32) * diff_lambda_k1[0].astype(F32)))
           - jnp.exp(jnp.sum(diff_lambda_q2[0].astype(F32) * diff_lambda_k2[0].astype(F32)))
           + LAM_INIT).reshape(1)
    lb = jnp.cumsum(jax.nn.softmax(hgrn_lb_logits.astype(F32), axis=0), axis=0)[0]
    lb = lb.reshape(HGRN_HEADS, 1, HEAD_W)
    table = rel_bias_table.astype(F32)

    qkv, hin = _norm_proj(x2, row(norm1_g[0]), w_in[0].astype(BF16), tb_proj)
    bias = _bias_tiles(table, tq)
    a = _diff_attn(lam, table, qkv.reshape(batch, seq, ATTN_COLS), bias, row(diff_subln_g[0]), batch, seq, tq)
    o = _hgrn(hin.reshape(batch, seq, HGRN_COLS), lb, row(hgrn_gnorm_g[0]), batch, seq)
    w_o = w_out[0].astype(BF16)
    h2, hnt = _out_proj(x2, a.reshape(t, DIFF_WIDTH), o.reshape(t, HGRN_WIDTH),
                        w_o[:DIFF_WIDTH], w_o[DIFF_WIDTH:], row(norm2_g[0]), tb_proj)
    rank1, alpha, n_sel, beta = _peer_route(hnt, peer_w_q[0].T.astype(BF16), peer_sub_keys[0].astype(BF16), tb_route)
    peer_t = _peer_dense(hnt, peer_u[0].astype(BF16), peer_v[0].T.astype(BF16), rank1, alpha, n_sel, beta,
                         tb_peer, te_peer)
    out = _final_norm(h2, peer_t, row(final_norm_g), tb_proj)
    return out.reshape(batch, seq, d)
```

```python
import functools
import math

import numpy as np
import jax
import jax.numpy as jnp
from jax import lax
from jax.experimental import pallas as pl
from jax.experimental.pallas import tpu as pltpu

F32 = jnp.float32
BF16 = jnp.bfloat16

D_MODEL = 1024
DIFF_HEADS = 4
DIFF_QK_DIM = 64
LANES = 128
MXU_WIDTH = 256
SECTION_ROWS = MXU_WIDTH
HEAD_W = 128
DIFF_WIDTH = DIFF_HEADS * HEAD_W
HGRN_HEADS = 4
HGRN_WIDTH = HGRN_HEADS * HEAD_W
HGRN_CHUNK = 32
ATTN_COLS = 3 * DIFF_WIDTH
HGRN_COLS = 4 * HGRN_WIDTH
REL_BUCKETS = 32
REL_MAX_DIST = 128
PEER_HEADS = 8
PEER_N_KEYS = 128
PEER_N_EXPERTS = PEER_N_KEYS * PEER_N_KEYS
PEER_QUERY_DIM = 256
PEER_TOPK = 16
NORM_EPS = 1e-6
LAM_INIT = 0.8 - 0.6 * math.exp(-0.3 * 0)
NEG_BIG = -1e30

VMEM_LIMIT_BYTES = 56 * 1024 * 1024


def _cparams(sem):
    return pltpu.CompilerParams(dimension_semantics=sem, vmem_limit_bytes=VMEM_LIMIT_BYTES)


def _rms(x, g):
    return x * lax.rsqrt(jnp.mean(x * x, axis=-1, keepdims=True) + NORM_EPS) * g


def _norm_proj_kernel(x_ref, g_ref, w_ref, oa_ref, oh_ref):
    y = _rms(x_ref[...], g_ref[...])
    p = jnp.dot(y.astype(BF16), w_ref[...], preferred_element_type=F32)
    oa_ref[...] = p[:, :ATTN_COLS].astype(BF16)
    oh_ref[...] = p[:, ATTN_COLS:]


def _norm_proj(x2, g, w_bf16, tb):
    t, d = x2.shape
    n = w_bf16.shape[1]
    return pl.pallas_call(
        _norm_proj_kernel,
        grid=(t // tb,),
        in_specs=[pl.BlockSpec((tb, d), lambda i: (i, 0)),
                  pl.BlockSpec((1, d), lambda i: (0, 0)),
                  pl.BlockSpec((d, n), lambda i: (0, 0))],
        out_specs=[pl.BlockSpec((tb, ATTN_COLS), lambda i: (i, 0)),
                   pl.BlockSpec((tb, HGRN_COLS), lambda i: (i, 0))],
        out_shape=[jax.ShapeDtypeStruct((t, ATTN_COLS), BF16),
                   jax.ShapeDtypeStruct((t, HGRN_COLS), F32)],
        compiler_params=_cparams(("parallel",)),
        name="norm_proj",
    )(x2, g, w_bf16)


def _t5_bucket_np(n):
    n = np.maximum(n, 0).astype(np.int32)
    max_exact = REL_BUCKETS // 2
    nf = np.maximum(n, max_exact).astype(np.float32)
    large = max_exact + (np.log(nf / np.float32(max_exact)) / np.float32(math.log(REL_MAX_DIST / max_exact))
                         * np.float32(REL_BUCKETS - max_exact)).astype(np.int32)
    large = np.minimum(large, REL_BUCKETS - 1)
    return np.where(n < max_exact, n, large).astype(np.int32)


def _bias_tiles_kernel(table_ref, bucket_ref, o_ref):
    h = pl.program_id(0)
    bucket = bucket_ref[...]
    acc = jnp.zeros(bucket.shape, F32)
    for b in range(REL_BUCKETS):
        acc = jnp.where(bucket == b, table_ref[b, h], acc)
    o_ref[...] = acc


def _bias_tiles(table, tq):
    r = np.arange(tq)[:, None]
    c = np.arange(tq)[None, :]
    tiles = np.stack([_t5_bucket_np(d * tq + r - c) for d in (0, 1)])
    tiles = np.concatenate([tiles, tiles], axis=1)
    return pl.pallas_call(
        _bias_tiles_kernel,
        grid=(DIFF_HEADS,),
        in_specs=[pl.BlockSpec(memory_space=pltpu.SMEM),
                  pl.BlockSpec((2, 2 * tq, tq), lambda h: (0, 0, 0))],
        out_specs=pl.BlockSpec((None, 2, 2 * tq, tq), lambda h: (h, 0, 0, 0)),
        out_shape=jax.ShapeDtypeStruct((DIFF_HEADS, 2, 2 * tq, tq), F32),
        compiler_params=_cparams(("arbitrary",)),
        name="bias_tiles",
    )(table, jnp.asarray(tiles))


def _diff_attn_kernel(lam_ref, table_ref, q_ref, k_ref, v_ref, bias_ref, g_ref, o_ref,
                      s_ref, mx_ref, l_ref, acc_ref, *, tq):
    h = pl.program_id(1)
    i = pl.program_id(2)
    q = q_ref[...] * jnp.asarray(DIFF_QK_DIM ** -0.5, BF16)
    lane = lax.broadcasted_iota(jnp.int32, q.shape, 1)
    zero = jnp.zeros_like(q)
    q2 = jnp.concatenate([jnp.where(lane < DIFF_QK_DIM, q, zero),
                          jnp.where(lane >= DIFF_QK_DIM, q, zero)], axis=0)
    far_bias = table_ref[REL_BUCKETS - 1, h]
    lane_halves = tq // LANES

    def lanewise(op, x):
        out = x[:, :LANES]
        for k in range(1, lane_halves):
            out = op(out, x[:, k * LANES:(k + 1) * LANES])
        return out

    def scores(j, bias, causal):
        start = pl.multiple_of(j * tq, tq)
        s = lax.dot_general(q2, k_ref[pl.ds(start, tq), :], (((1,), (1,)), ((), ())),
                            preferred_element_type=F32) + bias
        if causal:
            r = lax.broadcasted_iota(jnp.int32, s.shape, 0) % tq
            c = lax.broadcasted_iota(jnp.int32, s.shape, 1)
            s = jnp.where(c <= r, s, NEG_BIG)
        s_ref[j] = s
        mx_ref[...] = jnp.maximum(mx_ref[...], lanewise(jnp.maximum, s))

    mx_ref[...] = jnp.full(mx_ref.shape, NEG_BIG, F32)
    l_ref[...] = jnp.zeros(l_ref.shape, F32)
    acc_ref[...] = jnp.zeros(acc_ref.shape, F32)

    def far_pair(p, carry):
        start = pl.multiple_of(p * 2 * tq, 2 * tq)
        s = lax.dot_general(q2, k_ref[pl.ds(start, 2 * tq), :], (((1,), (1,)), ((), ())),
                            preferred_element_type=F32) + far_bias
        s_ref[2 * p] = s[:, :tq]
        s_ref[2 * p + 1] = s[:, tq:]
        mx_ref[...] = jnp.maximum(mx_ref[...], jnp.maximum(lanewise(jnp.maximum, s[:, :tq]),
                                                           lanewise(jnp.maximum, s[:, tq:])))
        return carry

    n_far = jnp.maximum(i - 1, 0)
    lax.fori_loop(0, n_far // 2, far_pair, 0)

    @pl.when(n_far % 2 == 1)
    def _():
        scores(n_far - 1, far_bias, False)

    @pl.when(i >= 1)
    def _():
        scores(i - 1, bias_ref[1], False)

    scores(i, bias_ref[0], True)
    m = jnp.max(mx_ref[...], axis=-1, keepdims=True)

    def weigh_pair(p, carry):
        start = pl.multiple_of(p * 2 * tq, 2 * tq)
        e = jnp.exp(jnp.concatenate([s_ref[2 * p], s_ref[2 * p + 1]], axis=1) - m)
        l_ref[...] += lanewise(jnp.add, e[:, :tq]) + lanewise(jnp.add, e[:, tq:])
        acc_ref[...] += jnp.dot(e.astype(BF16), v_ref[pl.ds(start, 2 * tq), :], preferred_element_type=F32)
        return carry

    n_blocks = i + 1
    lax.fori_loop(0, n_blocks // 2, weigh_pair, 0)

    @pl.when(n_blocks % 2 == 1)
    def _():
        start = pl.multiple_of(i * tq, tq)
        e = jnp.exp(s_ref[i] - m)
        l_ref[...] += lanewise(jnp.add, e)
        acc_ref[...] += jnp.dot(e.astype(BF16), v_ref[pl.ds(start, tq), :], preferred_element_type=F32)
    l = jnp.sum(l_ref[...], axis=-1, keepdims=True)
    acc = acc_ref[...]
    lam = lam_ref[0]
    out = acc[:tq] / l[:tq] - lam * (acc[tq:] / l[tq:])
    o_ref[...] = (_rms(out, g_ref[...]) * (1.0 - LAM_INIT)).astype(o_ref.dtype)


def _diff_attn(lam, table, qkv, bias, g, batch, seq, tq):
    kern = functools.partial(_diff_attn_kernel, tq=tq)
    return pl.pallas_call(
        kern,
        grid=(batch, DIFF_HEADS, seq // tq),
        in_specs=[pl.BlockSpec(memory_space=pltpu.SMEM),
                  pl.BlockSpec(memory_space=pltpu.SMEM),
                  pl.BlockSpec((None, tq, HEAD_W), lambda b, h, i: (b, i, h)),
                  pl.BlockSpec((None, seq, HEAD_W), lambda b, h, i: (b, 0, DIFF_HEADS + h)),
                  pl.BlockSpec((None, seq, HEAD_W), lambda b, h, i: (b, 0, 2 * DIFF_HEADS + h)),
                  pl.BlockSpec((None, 2, 2 * tq, tq), lambda b, h, i: (h, 0, 0, 0)),
                  pl.BlockSpec((1, HEAD_W), lambda b, h, i: (0, 0))],
        out_specs=pl.BlockSpec((None, tq, HEAD_W), lambda b, h, i: (b, i, h)),
        out_shape=jax.ShapeDtypeStruct((batch, seq, DIFF_WIDTH), BF16),
        scratch_shapes=[pltpu.VMEM((seq // tq, 2 * tq, tq), F32),
                        pltpu.VMEM((2 * tq, LANES), F32),
                        pltpu.VMEM((2 * tq, LANES), F32),
                        pltpu.VMEM((2 * tq, HEAD_W), F32)],
        compiler_params=_cparams(("parallel", "parallel", "arbitrary")),
        name="diff_attn",
    )(lam, table, qkv, qkv, qkv, bias, g)


def _hgrn_kernel(hq_ref, hf_ref, hi_ref, hg_ref, lb_ref, g_ref, o_ref,
                 qd_ref, kd_ref, cd_ref, oi_ref, upd_ref, prev_ref, *, seq):
    c_len = HGRN_CHUNK
    n_chunks = seq // c_len
    lb = lb_ref[...]
    f = lb + (1.0 - lb) * jax.nn.sigmoid(hf_ref[...])
    log_f = jnp.log(f)
    kk = 1.0 - f
    hq = hq_ref[...]
    qq = hq * jax.nn.sigmoid(hq)
    vv = hi_ref[...]

    pos = lax.broadcasted_iota(jnp.int32, (seq, HEAD_W), 0) % c_len
    b = log_f
    sh = 1
    while sh < c_len:
        b = b + jnp.where(pos >= sh, pltpu.roll(b, sh, axis=0), 0.0)
        sh *= 2
    b3 = b.reshape(n_chunks, c_len, HEAD_W)
    b_last = b3[:, c_len - 1:c_len, :]
    q_dec = qq * jnp.exp(b)
    k_inv = kk * jnp.exp(-b)
    k_dec = kk * jnp.exp(b_last - b3).reshape(seq, HEAD_W)
    qd_ref[...] = q_dec.astype(BF16)
    kd_ref[...] = k_dec.astype(BF16)
    cd_ref[...] = jnp.exp(b_last.reshape(n_chunks, HEAD_W))

    blk = HEAD_W
    nb = seq // blk
    qd_b = q_dec.astype(BF16).reshape(nb, blk, HEAD_W)
    ki_b = k_inv.astype(BF16).reshape(nb, blk, HEAD_W)
    scores = jnp.einsum('nqd,nkd->nqk', qd_b, ki_b, preferred_element_type=F32)
    r = lax.broadcasted_iota(jnp.int32, (blk, blk), 0)
    c = lax.broadcasted_iota(jnp.int32, (blk, blk), 1)
    keep = (c <= r) & ((r // c_len) == (c // c_len))
    scores = jnp.where(keep[None], scores, 0.0)
    o_intra = jnp.einsum('nqk,nkv->nqv', scores.astype(BF16), vv.astype(BF16).reshape(nb, blk, HEAD_W),
                         preferred_element_type=F32).reshape(seq, HEAD_W)

    unroll = 4

    def rows_of(ci):
        return pl.ds(pl.multiple_of(ci * c_len, c_len), c_len)

    def updates(g, carry):
        for u in range(unroll):
            ci = g * unroll + u
            vc = hi_ref[rows_of(ci), :].astype(BF16)
            upd_ref[ci] = lax.dot_general(vc, kd_ref[rows_of(ci), :], (((0,), (0,)), ((), ())),
                                          preferred_element_type=F32)
        return carry

    lax.fori_loop(0, n_chunks // unroll, updates, 0)

    def recur(ci, st):
        prev_ref[ci] = st.astype(BF16)
        return st * cd_ref[pl.ds(ci, 1), :] + upd_ref[ci]

    lax.fori_loop(0, n_chunks, recur, jnp.zeros((HEAD_W, HEAD_W), F32))

    def reads(g, carry):
        for u in range(unroll):
            ci = g * unroll + u
            oi_ref[rows_of(ci), :] = lax.dot_general(qd_ref[rows_of(ci), :], prev_ref[ci],
                                                     (((1,), (1,)), ((), ())), preferred_element_type=F32)
        return carry

    lax.fori_loop(0, n_chunks // unroll, reads, 0)

    o = o_intra + oi_ref[...]
    hg = hg_ref[...]
    o_ref[...] = (_rms(o, g_ref[...]) * (hg * jax.nn.sigmoid(hg))).astype(o_ref.dtype)


def _hgrn(hin, lb, g, batch, seq):
    kern = functools.partial(_hgrn_kernel, seq=seq)
    col = lambda k: (lambda b, h: (b, 0, k * HGRN_HEADS + h))
    blk = (None, seq, HEAD_W)
    return pl.pallas_call(
        kern,
        grid=(batch, HGRN_HEADS),
        in_specs=[pl.BlockSpec(blk, col(0)), pl.BlockSpec(blk, col(1)),
                  pl.BlockSpec(blk, col(2)), pl.BlockSpec(blk, col(3)),
                  pl.BlockSpec((None, 1, HEAD_W), lambda b, h: (h, 0, 0)),
                  pl.BlockSpec((1, HEAD_W), lambda b, h: (0, 0))],
        out_specs=pl.BlockSpec(blk, lambda b, h: (b, 0, h)),
        out_shape=jax.ShapeDtypeStruct((batch, seq, HGRN_WIDTH), BF16),
        scratch_shapes=[pltpu.VMEM((seq, HEAD_W), BF16), pltpu.VMEM((seq, HEAD_W), BF16),
                        pltpu.VMEM((seq // HGRN_CHUNK, HEAD_W), F32), pltpu.VMEM((seq, HEAD_W), F32),
                        pltpu.VMEM((seq // HGRN_CHUNK, HEAD_W, HEAD_W), F32),
                        pltpu.VMEM((seq // HGRN_CHUNK, HEAD_W, HEAD_W), BF16)],
        compiler_params=_cparams(("parallel", "parallel")),
        name="hgrn",
    )(hin, hin, hin, hin, lb, g)


def _out_proj_kernel(x_ref, a_ref, o_ref, wa_ref, wo_ref, g_ref, h_ref, hnt_ref):
    h = (x_ref[...]
         + jnp.dot(a_ref[...], wa_ref[...], preferred_element_type=F32)
         + jnp.dot(o_ref[...], wo_ref[...], preferred_element_type=F32))
    h_ref[...] = h
    hnt_ref[...] = _rms(h, g_ref[...]).T.astype(BF16)


def _out_proj(x2, a2, o2, wa, wo, g, tb):
    t, d = x2.shape
    return pl.pallas_call(
        _out_proj_kernel,
        grid=(t // tb,),
        in_specs=[pl.BlockSpec((tb, d), lambda i: (i, 0)),
                  pl.BlockSpec((tb, DIFF_WIDTH), lambda i: (i, 0)),
                  pl.BlockSpec((tb, HGRN_WIDTH), lambda i: (i, 0)),
                  pl.BlockSpec((DIFF_WIDTH, d), lambda i: (0, 0)),
                  pl.BlockSpec((HGRN_WIDTH, d), lambda i: (0, 0)),
                  pl.BlockSpec((1, d), lambda i: (0, 0))],
        out_specs=[pl.BlockSpec((tb, d), lambda i: (i, 0)),
                   pl.BlockSpec((d, tb), lambda i: (0, i))],
        out_shape=[jax.ShapeDtypeStruct((t, d), F32),
                   jax.ShapeDtypeStruct((d, t), BF16)],
        compiler_params=_cparams(("parallel",)),
        name="out_proj",
    )(x2, a2, o2, wa, wo, g)


KNOCK_BASE = 2.0 ** 126
KNOCK_STEP = 2.0 ** 104


def _top_values(x, k):
    row = lax.broadcasted_iota(jnp.int32, (k, x.shape[1]), 0)
    top = jnp.zeros((k, x.shape[1]), F32)
    for r in range(k):
        m = jnp.max(x, axis=0, keepdims=True)
        top = jnp.where(row == r, m, top)
        x = jnp.where(x == m, -(KNOCK_BASE + r * KNOCK_STEP), x)
    return top, x


def _bf16_row_tile(row, n_rows):
    packed_rows = 16
    tile16 = jnp.broadcast_to(row, (packed_rows, row.shape[1])).astype(BF16)
    return jnp.broadcast_to(tile16[None], (n_rows // packed_rows, packed_rows, row.shape[1])
                            ).reshape(n_rows, row.shape[1])


def _decode_rank(coded, k):
    return jnp.where(coded <= -KNOCK_BASE, (-coded - KNOCK_BASE) * (1.0 / KNOCK_STEP), float(k))


def _pair_candidates(amat, bmat):
    k = PEER_TOPK
    sub = lax.broadcasted_iota(jnp.int32, (8, amat.shape[1]), 0)
    b_lo, b_hi = bmat[:8], bmat[8:]
    pieces = [amat[0:1] + b_lo, amat[0:1] + b_hi, amat[1:2] + b_lo]
    for i in range(2, 8):
        pieces.append(jnp.where(sub < k // (i + 1), amat[i:i + 1] + b_lo, -jnp.inf))
    pieces.append(amat[8:] + bmat[0:1])
    return jnp.concatenate(pieces, axis=0)


def _peer_route_kernel(hnt_ref, wqt_ref, keys_ref, r1_ref, alpha_ref, m_ref, beta_ref, qt_ref, s_ref):
    qt_ref[...] = jnp.dot(wqt_ref[...], hnt_ref[...], preferred_element_type=F32).astype(BF16)
    half = PEER_QUERY_DIM // 2
    lane_tiles = hnt_ref.shape[1] // LANES
    k = PEER_TOPK

    def head(h, carry):
        base = pl.multiple_of(h * PEER_QUERY_DIM, PEER_QUERY_DIM)
        for c in range(2):
            s = jnp.dot(keys_ref[h, c], qt_ref[pl.ds(base + c * half, half), :], preferred_element_type=F32)
            for lt in range(lane_tiles):
                s_ref[c, lt] = s[:, lt * LANES:(lt + 1) * LANES]

        def tile(lt, carry):
            cols = pl.ds(pl.multiple_of(lt * LANES, LANES), LANES)
            s1 = s_ref[0, lt]
            s2 = s_ref[1, lt]
            amat, coded1 = _top_values(s1, k)
            bmat, coded2 = _top_values(s2, k)
            cands = _pair_candidates(amat, bmat)
            thr = _top_values(cands, k)[0][k - 1:k]
            top = amat[0:1] + bmat[0:1]
            z = jnp.sum(jnp.where(cands >= thr, jnp.exp(cands - top), 0.0), axis=0, keepdims=True)
            rank2 = _decode_rank(coded2, k).astype(BF16)
            m = jnp.zeros(rank2.shape, BF16)
            for l in range(k):
                n_sel = jnp.sum(jnp.where(amat + bmat[l:l + 1] >= thr, 1.0, 0.0), axis=0, keepdims=True)
                m = jnp.where(rank2 == l, _bf16_row_tile(n_sel, rank2.shape[0]), m)
            r1_ref[h, :, cols] = _decode_rank(coded1, k)
            alpha_ref[h, :, cols] = jnp.exp(s1 - amat[0:1]) * (0.5 / z)
            m_ref[h, lt] = m
            beta_ref[h, lt] = jnp.exp(s2 - bmat[0:1]).astype(BF16)
            return carry

        return lax.fori_loop(0, lane_tiles, tile, carry)

    lax.fori_loop(0, PEER_HEADS, head, 0)


def _peer_route(hnt, wqt, keys, tb):
    d, t = hnt.shape
    nq = wqt.shape[0]
    oshape = jax.ShapeDtypeStruct((PEER_HEADS, PEER_N_KEYS, t), F32)
    ospec = pl.BlockSpec((PEER_HEADS, PEER_N_KEYS, tb), lambda i: (0, 0, i))
    tshape = jax.ShapeDtypeStruct((PEER_HEADS, t // LANES, PEER_N_KEYS, LANES), BF16)
    tspec = pl.BlockSpec((PEER_HEADS, tb // LANES, PEER_N_KEYS, LANES), lambda i: (0, i, 0, 0))
    return pl.pallas_call(
        _peer_route_kernel,
        grid=(t // tb,),
        in_specs=[pl.BlockSpec((d, tb), lambda i: (0, i)),
                  pl.BlockSpec((nq, d), lambda i: (0, 0)),
                  pl.BlockSpec(keys.shape, lambda i: (0, 0, 0, 0))],
        out_specs=[ospec, ospec, tspec, tspec],
        out_shape=[oshape, oshape, tshape, tshape],
        scratch_shapes=[pltpu.VMEM((nq, tb), BF16), pltpu.VMEM((2, tb // LANES, PEER_N_KEYS, LANES), F32)],
        compiler_params=_cparams(("parallel",)),
        name="peer_route",
    )(hnt, wqt, keys)


def _peer_dense_kernel(hnt_ref, u_ref, vt_ref, r1_ref, alpha_ref, m_ref, beta_ref, o_ref,
                       hid_ref, act_ref, *, te, tb, n_tiles):
    j = pl.program_id(1)
    rows_per_tile = te // PEER_N_KEYS
    lane_tiles = tb // LANES
    zero = jnp.zeros((PEER_N_KEYS, LANES), BF16)

    def row_bcast(ref, h, r, lt):
        return _bf16_row_tile(ref[h, r:r + 1, lt * LANES:(lt + 1) * LANES], PEER_N_KEYS)

    def hidden():
        slot = j % 2
        hid = jnp.dot(u_ref[...], hnt_ref[...], preferred_element_type=F32)
        for lt in range(lane_tiles):
            hid_ref[slot, lt] = hid[:, lt * LANES:(lt + 1) * LANES]

    def activations():
        slot = (j + 1) % 2
        for lt in range(lane_tiles):
            for r in range(rows_per_tile):
                w = None
                for h in range(PEER_HEADS):
                    sel = row_bcast(r1_ref, h, r, lt) < m_ref[h, lt]
                    term = jnp.where(sel, beta_ref[h, lt], zero) * row_bcast(alpha_ref, h, r, lt)
                    w = term if w is None else w + term
                rows = slice(r * PEER_N_KEYS, (r + 1) * PEER_N_KEYS)
                hid_t = hid_ref[slot, lt, rows, :]
                gelu2 = hid_t * (1.0 + lax.erf(hid_t * (2.0 ** -0.5)))
                act_ref[slot, lt, rows, :] = gelu2.astype(BF16) * w

    def project():
        slot = j % 2
        act = jnp.concatenate([act_ref[slot, lt] for lt in range(lane_tiles)], axis=1)
        o_ref[...] += jnp.dot(vt_ref[...], act, preferred_element_type=F32)

    @pl.when(jnp.logical_and(j >= 2, j < n_tiles))
    def _():
        hidden()
        activations()
        project()

    @pl.when(j == 0)
    def _():
        o_ref[...] = jnp.zeros(o_ref.shape, o_ref.dtype)
        hidden()

    @pl.when(j == 1)
    def _():
        hidden()
        activations()

    @pl.when(j == n_tiles)
    def _():
        activations()
        project()

    @pl.when(j == n_tiles + 1)
    def _():
        project()


def _peer_dense(hnt, u_bf16, vt_bf16, rank1, alpha, n_sel, beta, tb, te):
    d, t = hnt.shape
    n_tiles = u_bf16.shape[0] // te
    assert n_tiles >= 2
    kern = functools.partial(_peer_dense_kernel, te=te, tb=tb, n_tiles=n_tiles)
    last = n_tiles - 1
    rspec = pl.BlockSpec((PEER_HEADS, tb // LANES, PEER_N_KEYS, LANES), lambda i, j: (0, i, 0, 0))
    i1spec = pl.BlockSpec((PEER_HEADS, te // PEER_N_KEYS, tb), lambda i, j: (0, jnp.clip(j - 1, 0, last), i))
    stage_shape = (2, tb // LANES, te, LANES)
    return pl.pallas_call(
        kern,
        grid=(t // tb, n_tiles + 2),
        in_specs=[pl.BlockSpec((d, tb), lambda i, j: (0, i)),
                  pl.BlockSpec((te, d), lambda i, j: (jnp.minimum(j, last), 0)),
                  pl.BlockSpec((d, te), lambda i, j: (0, jnp.clip(j - 2, 0, last))),
                  i1spec, i1spec, rspec, rspec],
        out_specs=pl.BlockSpec((d, tb), lambda i, j: (0, i)),
        out_shape=jax.ShapeDtypeStruct((d, t), F32),
        scratch_shapes=[pltpu.VMEM(stage_shape, F32), pltpu.VMEM(stage_shape, BF16)],
        compiler_params=_cparams(("parallel", "arbitrary")),
        name="peer_dense",
    )(hnt, u_bf16, vt_bf16, rank1, alpha, n_sel, beta)


def _final_norm_kernel(h_ref, pt_ref, g_ref, o_ref):
    o_ref[...] = _rms(h_ref[...] + pt_ref[...].T, g_ref[...])


def _final_norm(h2, peer_t, g, tb):
    t, d = h2.shape
    return pl.pallas_call(
        _final_norm_kernel,
        grid=(t // tb,),
        in_specs=[pl.BlockSpec((tb, d), lambda i: (i, 0)),
                  pl.BlockSpec((d, tb), lambda i: (0, i)),
                  pl.BlockSpec((1, d), lambda i: (0, 0))],
        out_specs=pl.BlockSpec((tb, d), lambda i: (i, 0)),
        out_shape=jax.ShapeDtypeStruct((t, d), F32),
        compiler_params=_cparams(("parallel",)),
        name="final_norm",
    )(h2, peer_t, g)


def _largest_divisor(n, cap, multiple):
    best = multiple
    k = multiple
    while k <= min(n, cap):
        if n % k == 0:
            best = k
        k += multiple
    return best


def kernel(x, norm1_g, w_in, diff_lambda_q1, diff_lambda_k1, diff_lambda_q2, diff_lambda_k2, diff_subln_g,
           rel_bias_table, hgrn_lb_logits, hgrn_gnorm_g, w_out, norm2_g, peer_w_q, peer_sub_keys, peer_u,
           peer_v, final_norm_g):
    batch, seq, d = x.shape
    t = batch * seq
    assert d == D_MODEL and seq % 256 == 0 and w_in.shape[0] == 1
    tb_proj = _largest_divisor(t, 512, 256)
    tq = 256
    tb_route = _largest_divisor(t, 512, 256)
    tb_peer = _largest_divisor(t, 512, 512)
    te_peer = 1024

    x2 = x.reshape(t, d)
    row = lambda v: v.reshape(1, -1).astype(F32)

    lam = (jnp.exp(jnp.sum(diff_lambda_q1[0].astype(F32) * diff_lambda_k1[0].astype(F32)))
           - jnp.exp(jnp.sum(diff_lambda_q2[0].astype(F32) * diff_lambda_k2[0].astype(F32)))
           + LAM_INIT).reshape(1)
    lb = jnp.cumsum(jax.nn.softmax(hgrn_lb_logits.astype(F32), axis=0), axis=0)[0]
    lb = lb.reshape(HGRN_HEADS, 1, HEAD_W)
    table = rel_bias_table.astype(F32)

    qkv, hin = _norm_proj(x2, row(norm1_g[0]), w_in[0].astype(BF16), tb_proj)
    bias = _bias_tiles(table, tq)
    a = _diff_attn(lam, table, qkv.reshape(batch, seq, ATTN_COLS), bias, row(diff_subln_g[0]), batch, seq, tq)
    o = _hgrn(hin.reshape(batch, seq, HGRN_COLS), lb, row(hgrn_gnorm_g[0]), batch, seq)
    w_o = w_out[0].astype(BF16)
    h2, hnt = _out_proj(x2, a.reshape(t, DIFF_WIDTH), o.reshape(t, HGRN_WIDTH),
                        w_o[:DIFF_WIDTH], w_o[DIFF_WIDTH:], row(norm2_g[0]), tb_proj)
    rank1, alpha, n_sel, beta = _peer_route(hnt, peer_w_q[0].T.astype(BF16), peer_sub_keys[0].astype(BF16), tb_route)
    peer_t = _peer_dense(hnt, peer_u[0].astype(BF16), peer_v[0].T.astype(BF16), rank1, alpha, n_sel, beta,
                         tb_peer, te_peer)
    out = _final_norm(h2, peer_t, row(final_norm_g), tb_proj)
    return out.reshape(batch, seq, d)
```

```python
import functools
import math

import numpy as np
import jax
import jax.numpy as jnp
from jax import lax
from jax.experimental import pallas as pl
from jax.experimental.pallas import tpu as pltpu

F32 = jnp.float32
BF16 = jnp.bfloat16

D_MODEL = 1024
DIFF_HEADS = 4
DIFF_QK_DIM = 64
LANES = 128
MXU_WIDTH = 256
SECTION_ROWS = MXU_WIDTH
HEAD_W = 128
DIFF_WIDTH = DIFF_HEADS * HEAD_W
HGRN_HEADS = 4
HGRN_WIDTH = HGRN_HEADS * HEAD_W
HGRN_CHUNK = 32
ATTN_COLS = 3 * DIFF_WIDTH
HGRN_COLS = 4 * HGRN_WIDTH
REL_BUCKETS = 32
REL_MAX_DIST = 128
PEER_HEADS = 8
PEER_N_KEYS = 128
PEER_N_EXPERTS = PEER_N_KEYS * PEER_N_KEYS
PEER_QUERY_DIM = 256
PEER_TOPK = 16
NORM_EPS = 1e-6
LAM_INIT = 0.8 - 0.6 * math.exp(-0.3 * 0)
NEG_BIG = -1e30

VMEM_LIMIT_BYTES = 56 * 1024 * 1024


def _cparams(sem):
    return pltpu.CompilerParams(dimension_semantics=sem, vmem_limit_bytes=VMEM_LIMIT_BYTES)


def _rms(x, g):
    return x * lax.rsqrt(jnp.mean(x * x, axis=-1, keepdims=True) + NORM_EPS) * g


def _norm_proj_kernel(x_ref, g_ref, w_ref, oa_ref, oh_ref):
    y = _rms(x_ref[...], g_ref[...])
    p = jnp.dot(y.astype(BF16), w_ref[...], preferred_element_type=F32)
    oa_ref[...] = p[:, :ATTN_COLS].astype(BF16)
    oh_ref[...] = p[:, ATTN_COLS:]


def _norm_proj(x2, g, w_bf16, tb):
    t, d = x2.shape
    n = w_bf16.shape[1]
    return pl.pallas_call(
        _norm_proj_kernel,
        grid=(t // tb,),
        in_specs=[pl.BlockSpec((tb, d), lambda i: (i, 0)),
                  pl.BlockSpec((1, d), lambda i: (0, 0)),
                  pl.BlockSpec((d, n), lambda i: (0, 0))],
        out_specs=[pl.BlockSpec((tb, ATTN_COLS), lambda i: (i, 0)),
                   pl.BlockSpec((tb, HGRN_COLS), lambda i: (i, 0))],
        out_shape=[jax.ShapeDtypeStruct((t, ATTN_COLS), BF16),
                   jax.ShapeDtypeStruct((t, HGRN_COLS), F32)],
        compiler_params=_cparams(("parallel",)),
        name="norm_proj",
    )(x2, g, w_bf16)


def _t5_bucket_np(n):
    n = np.maximum(n, 0).astype(np.int32)
    max_exact = REL_BUCKETS // 2
    nf = np.maximum(n, max_exact).astype(np.float32)
    large = max_exact + (np.log(nf / np.float32(max_exact)) / np.float32(math.log(REL_MAX_DIST / max_exact))
                         * np.float32(REL_BUCKETS - max_exact)).astype(np.int32)
    large = np.minimum(large, REL_BUCKETS - 1)
    return np.where(n < max_exact, n, large).astype(np.int32)


def _bias_tiles_kernel(table_ref, bucket_ref, o_ref):
    h = pl.program_id(0)
    bucket = bucket_ref[...]
    acc = jnp.zeros(bucket.shape, F32)
    for b in range(REL_BUCKETS):
        acc = jnp.where(bucket == b, table_ref[b, h], acc)
    o_ref[...] = acc


def _bias_tiles(table, tq):
    r = np.arange(tq)[:, None]
    c = np.arange(tq)[None, :]
    tiles = np.stack([_t5_bucket_np(d * tq + r - c) for d in (0, 1)])
    tiles = np.concatenate([tiles, tiles], axis=1)
    return pl.pallas_call(
        _bias_tiles_kernel,
        grid=(DIFF_HEADS,),
        in_specs=[pl.BlockSpec(memory_space=pltpu.SMEM),
                  pl.BlockSpec((2, 2 * tq, tq), lambda h: (0, 0, 0))],
        out_specs=pl.BlockSpec((None, 2, 2 * tq, tq), lambda h: (h, 0, 0, 0)),
        out_shape=jax.ShapeDtypeStruct((DIFF_HEADS, 2, 2 * tq, tq), F32),
        compiler_params=_cparams(("arbitrary",)),
        name="bias_tiles",
    )(table, jnp.asarray(tiles))


def _diff_attn_kernel(lam_ref, table_ref, q_ref, k_ref, v_ref, bias_ref, g_ref, o_ref,
                      s_ref, mx_ref, l_ref, acc_ref, *, tq):
    h = pl.program_id(1)
    i = pl.program_id(2)
    q = q_ref[...] * jnp.asarray(DIFF_QK_DIM ** -0.5, BF16)
    lane = lax.broadcasted_iota(jnp.int32, q.shape, 1)
    zero = jnp.zeros_like(q)
    q2 = jnp.concatenate([jnp.where(lane < DIFF_QK_DIM, q, zero),
                          jnp.where(lane >= DIFF_QK_DIM, q, zero)], axis=0)
    far_bias = table_ref[REL_BUCKETS - 1, h]
    lane_halves = tq // LANES

    def lanewise(op, x):
        out = x[:, :LANES]
        for k in range(1, lane_halves):
            out = op(out, x[:, k * LANES:(k + 1) * LANES])
        return out

    def scores(j, bias, causal):
        start = pl.multiple_of(j * tq, tq)
        s = lax.dot_general(q2, k_ref[pl.ds(start, tq), :], (((1,), (1,)), ((), ())),
                            preferred_element_type=F32) + bias
        if causal:
            r = lax.broadcasted_iota(jnp.int32, s.shape, 0) % tq
            c = lax.broadcasted_iota(jnp.int32, s.shape, 1)
            s = jnp.where(c <= r, s, NEG_BIG)
        s_ref[j] = s
        mx_ref[...] = jnp.maximum(mx_ref[...], lanewise(jnp.maximum, s))

    mx_ref[...] = jnp.full(mx_ref.shape, NEG_BIG, F32)
    l_ref[...] = jnp.zeros(l_ref.shape, F32)
    acc_ref[...] = jnp.zeros(acc_ref.shape, F32)

    def far_pair(p, carry):
        start = pl.multiple_of(p * 2 * tq, 2 * tq)
        s = lax.dot_general(q2, k_ref[pl.ds(start, 2 * tq), :], (((1,), (1,)), ((), ())),
                            preferred_element_type=F32) + far_bias
        s_ref[2 * p] = s[:, :tq]
        s_ref[2 * p + 1] = s[:, tq:]
        mx_ref[...] = jnp.maximum(mx_ref[...], jnp.maximum(lanewise(jnp.maximum, s[:, :tq]),
                                                           lanewise(jnp.maximum, s[:, tq:])))
        return carry

    n_far = jnp.maximum(i - 1, 0)
    lax.fori_loop(0, n_far // 2, far_pair, 0)

    @pl.when(n_far % 2 == 1)
    def _():
        scores(n_far - 1, far_bias, False)

    @pl.when(i >= 1)
    def _():
        scores(i - 1, bias_ref[1], False)

    scores(i, bias_ref[0], True)
    m = jnp.max(mx_ref[...], axis=-1, keepdims=True)

    def weigh_pair(p, carry):
        start = pl.multiple_of(p * 2 * tq, 2 * tq)
        e = jnp.exp(jnp.concatenate([s_ref[2 * p], s_ref[2 * p + 1]], axis=1) - m)
        l_ref[...] += lanewise(jnp.add, e[:, :tq]) + lanewise(jnp.add, e[:, tq:])
        acc_ref[...] += jnp.dot(e.astype(BF16), v_ref[pl.ds(start, 2 * tq), :], preferred_element_type=F32)
        return carry

    n_blocks = i + 1
    lax.fori_loop(0, n_blocks // 2, weigh_pair, 0)

    @pl.when(n_blocks % 2 == 1)
    def _():
        start = pl.multiple_of(i * tq, tq)
        e = jnp.exp(s_ref[i] - m)
        l_ref[...] += lanewise(jnp.add, e)
        acc_ref[...] += jnp.dot(e.astype(BF16), v_ref[pl.ds(start, tq), :], preferred_element_type=F32)
    l = jnp.sum(l_ref[...], axis=-1, keepdims=True)
    acc = acc_ref[...]
    lam = lam_ref[0]
    out = acc[:tq] / l[:tq] - lam * (acc[tq:] / l[tq:])
    o_ref[...] = (_rms(out, g_ref[...]) * (1.0 - LAM_INIT)).astype(o_ref.dtype)


def _diff_attn(lam, table, qkv, bias, g, batch, seq, tq):
    kern = functools.partial(_diff_attn_kernel, tq=tq)
    return pl.pallas_call(
        kern,
        grid=(batch, DIFF_HEADS, seq // tq),
        in_specs=[pl.BlockSpec(memory_space=pltpu.SMEM),
                  pl.BlockSpec(memory_space=pltpu.SMEM),
                  pl.BlockSpec((None, tq, HEAD_W), lambda b, h, i: (b, i, h)),
                  pl.BlockSpec((None, seq, HEAD_W), lambda b, h, i: (b, 0, DIFF_HEADS + h)),
                  pl.BlockSpec((None, seq, HEAD_W), lambda b, h, i: (b, 0, 2 * DIFF_HEADS + h)),
                  pl.BlockSpec((None, 2, 2 * tq, tq), lambda b, h, i: (h, 0, 0, 0)),
                  pl.BlockSpec((1, HEAD_W), lambda b, h, i: (0, 0))],
        out_specs=pl.BlockSpec((None, tq, HEAD_W), lambda b, h, i: (b, i, h)),
        out_shape=jax.ShapeDtypeStruct((batch, seq, DIFF_WIDTH), BF16),
        scratch_shapes=[pltpu.VMEM((seq // tq, 2 * tq, tq), F32),
                        pltpu.VMEM((2 * tq, LANES), F32),
                        pltpu.VMEM((2 * tq, LANES), F32),
                        pltpu.VMEM((2 * tq, HEAD_W), F32)],
        compiler_params=_cparams(("parallel", "parallel", "arbitrary")),
        name="diff_attn",
    )(lam, table, qkv, qkv, qkv, bias, g)


def _hgrn_kernel(hq_ref, hf_ref, hi_ref, hg_ref, lb_ref, g_ref, o_ref,
                 qd_ref, kd_ref, cd_ref, oi_ref, upd_ref, prev_ref, *, seq):
    c_len = HGRN_CHUNK
    n_chunks = seq // c_len
    lb = lb_ref[...]
    f = lb + (1.0 - lb) * jax.nn.sigmoid(hf_ref[...])
    log_f = jnp.log(f)
    kk = 1.0 - f
    hq = hq_ref[...]
    qq = hq * jax.nn.sigmoid(hq)
    vv = hi_ref[...]

    pos = lax.broadcasted_iota(jnp.int32, (seq, HEAD_W), 0) % c_len
    b = log_f
    sh = 1
    while sh < c_len:
        b = b + jnp.where(pos >= sh, pltpu.roll(b, sh, axis=0), 0.0)
        sh *= 2
    b3 = b.reshape(n_chunks, c_len, HEAD_W)
    b_last = b3[:, c_len - 1:c_len, :]
    q_dec = qq * jnp.exp(b)
    k_inv = kk * jnp.exp(-b)
    k_dec = kk * jnp.exp(b_last - b3).reshape(seq, HEAD_W)
    qd_ref[...] = q_dec.astype(BF16)
    kd_ref[...] = k_dec.astype(BF16)
    cd_ref[...] = jnp.exp(b_last.reshape(n_chunks, HEAD_W))

    blk = HEAD_W
    nb = seq // blk
    qd_b = q_dec.astype(BF16).reshape(nb, blk, HEAD_W)
    ki_b = k_inv.astype(BF16).reshape(nb, blk, HEAD_W)
    scores = jnp.einsum('nqd,nkd->nqk', qd_b, ki_b, preferred_element_type=F32)
    r = lax.broadcasted_iota(jnp.int32, (blk, blk), 0)
    c = lax.broadcasted_iota(jnp.int32, (blk, blk), 1)
    keep = (c <= r) & ((r // c_len) == (c // c_len))
    scores = jnp.where(keep[None], scores, 0.0)
    o_intra = jnp.einsum('nqk,nkv->nqv', scores.astype(BF16), vv.astype(BF16).reshape(nb, blk, HEAD_W),
                         preferred_element_type=F32).reshape(seq, HEAD_W)

    unroll = 4

    def rows_of(ci):
        return pl.ds(pl.multiple_of(ci * c_len, c_len), c_len)

    def updates(g, carry):
        for u in range(unroll):
            ci = g * unroll + u
            vc = hi_ref[rows_of(ci), :].astype(BF16)
            upd_ref[ci] = lax.dot_general(vc, kd_ref[rows_of(ci), :], (((0,), (0,)), ((), ())),
                                          preferred_element_type=F32)
        return carry

    lax.fori_loop(0, n_chunks // unroll, updates, 0)

    def recur(ci, st):
        prev_ref[ci] = st.astype(BF16)
        return st * cd_ref[pl.ds(ci, 1), :] + upd_ref[ci]

    lax.fori_loop(0, n_chunks, recur, jnp.zeros((HEAD_W, HEAD_W), F32))

    def reads(g, carry):
        for u in range(unroll):
            ci = g * unroll + u
            oi_ref[rows_of(ci), :] = lax.dot_general(qd_ref[rows_of(ci), :], prev_ref[ci],
                                                     (((1,), (1,)), ((), ())), preferred_element_type=F32)
        return carry

    lax.fori_loop(0, n_chunks // unroll, reads, 0)

    o = o_intra + oi_ref[...]
    hg = hg_ref[...]
    o_ref[...] = (_rms(o, g_ref[...]) * (hg * jax.nn.sigmoid(hg))).astype(o_ref.dtype)


def _hgrn(hin, lb, g, batch, seq):
    kern = functools.partial(_hgrn_kernel, seq=seq)
    col = lambda k: (lambda b, h: (b, 0, k * HGRN_HEADS + h))
    blk = (None, seq, HEAD_W)
    return pl.pallas_call(
        kern,
        grid=(batch, HGRN_HEADS),
        in_specs=[pl.BlockSpec(blk, col(0)), pl.BlockSpec(blk, col(1)),
                  pl.BlockSpec(blk, col(2)), pl.BlockSpec(blk, col(3)),
                  pl.BlockSpec((None, 1, HEAD_W), lambda b, h: (h, 0, 0)),
                  pl.BlockSpec((1, HEAD_W), lambda b, h: (0, 0))],
        out_specs=pl.BlockSpec(blk, lambda b, h: (b, 0, h)),
        out_shape=jax.ShapeDtypeStruct((batch, seq, HGRN_WIDTH), BF16),
        scratch_shapes=[pltpu.VMEM((seq, HEAD_W), BF16), pltpu.VMEM((seq, HEAD_W), BF16),
                        pltpu.VMEM((seq // HGRN_CHUNK, HEAD_W), F32), pltpu.VMEM((seq, HEAD_W), F32),
                        pltpu.VMEM((seq // HGRN_CHUNK, HEAD_W, HEAD_W), F32),
                        pltpu.VMEM((seq // HGRN_CHUNK, HEAD_W, HEAD_W), BF16)],
        compiler_params=_cparams(("parallel", "parallel")),
        name="hgrn",
    )(hin, hin, hin, hin, lb, g)


def _out_proj_kernel(x_ref, a_ref, o_ref, wa_ref, wo_ref, g_ref, h_ref, hnt_ref):
    h = (x_ref[...]
         + jnp.dot(a_ref[...], wa_ref[...], preferred_element_type=F32)
         + jnp.dot(o_ref[...], wo_ref[...], preferred_element_type=F32))
    h_ref[...] = h
    hnt_ref[...] = _rms(h, g_ref[...]).T.astype(BF16)


def _out_proj(x2, a2, o2, wa, wo, g, tb):
    t, d = x2.shape
    return pl.pallas_call(
        _out_proj_kernel,
        grid=(t // tb,),
        in_specs=[pl.BlockSpec((tb, d), lambda i: (i, 0)),
                  pl.BlockSpec((tb, DIFF_WIDTH), lambda i: (i, 0)),
                  pl.BlockSpec((tb, HGRN_WIDTH), lambda i: (i, 0)),
                  pl.BlockSpec((DIFF_WIDTH, d), lambda i: (0, 0)),
                  pl.BlockSpec((HGRN_WIDTH, d), lambda i: (0, 0)),
                  pl.BlockSpec((1, d), lambda i: (0, 0))],
        out_specs=[pl.BlockSpec((tb, d), lambda i: (i, 0)),
                   pl.BlockSpec((d, tb), lambda i: (0, i))],
        out_shape=[jax.ShapeDtypeStruct((t, d), F32),
                   jax.ShapeDtypeStruct((d, t), BF16)],
        compiler_params=_cparams(("parallel",)),
        name="out_proj",
    )(x2, a2, o2, wa, wo, g)


KNOCK_BASE = 2.0 ** 126
KNOCK_STEP = 2.0 ** 104


def _top_values(x, k):
    row = lax.broadcasted_iota(jnp.int32, (k, x.shape[1]), 0)
    top = jnp.zeros((k, x.shape[1]), F32)
    for r in range(k):
        m = jnp.max(x, axis=0, keepdims=True)
        top = jnp.where(row == r, m, top)
        x = jnp.where(x == m, -(KNOCK_BASE + r * KNOCK_STEP), x)
    return top, x


def _bf16_row_tile(row, n_rows):
    packed_rows = 16
    tile16 = jnp.broadcast_to(row, (packed_rows, row.shape[1])).astype(BF16)
    return jnp.broadcast_to(tile16[None], (n_rows // packed_rows, packed_rows, row.shape[1])
                            ).reshape(n_rows, row.shape[1])


def _decode_rank(coded, k):
    return jnp.where(coded <= -KNOCK_BASE, (-coded - KNOCK_BASE) * (1.0 / KNOCK_STEP), float(k))


def _pair_candidates(amat, bmat):
    k = PEER_TOPK
    sub = lax.broadcasted_iota(jnp.int32, (8, amat.shape[1]), 0)
    b_lo, b_hi = bmat[:8], bmat[8:]
    pieces = [amat[0:1] + b_lo, amat[0:1] + b_hi, amat[1:2] + b_lo]
    for i in range(2, 8):
        pieces.append(jnp.where(sub < k // (i + 1), amat[i:i + 1] + b_lo, -jnp.inf))
    pieces.append(amat[8:] + bmat[0:1])
    return jnp.concatenate(pieces, axis=0)


def _peer_route_kernel(hnt_ref, wqt_ref, keys_ref, r1_ref, alpha_ref, m_ref, beta_ref, qt_ref, s_ref):
    qt_ref[...] = jnp.dot(wqt_ref[...], hnt_ref[...], preferred_element_type=F32).astype(BF16)
    half = PEER_QUERY_DIM // 2
    lane_tiles = hnt_ref.shape[1] // LANES
    k = PEER_TOPK

    def head(h, carry):
        base = pl.multiple_of(h * PEER_QUERY_DIM, PEER_QUERY_DIM)
        for c in range(2):
            s = jnp.dot(keys_ref[h, c], qt_ref[pl.ds(base + c * half, half), :], preferred_element_type=F32)
            for lt in range(lane_tiles):
                s_ref[c, lt] = s[:, lt * LANES:(lt + 1) * LANES]

        def tile(lt, carry):
            cols = pl.ds(pl.multiple_of(lt * LANES, LANES), LANES)
            s1 = s_ref[0, lt]
            s2 = s_ref[1, lt]
            amat, coded1 = _top_values(s1, k)
            bmat, coded2 = _top_values(s2, k)
            cands = _pair_candidates(amat, bmat)
            thr = _top_values(cands, k)[0][k - 1:k]
            top = amat[0:1] + bmat[0:1]
            z = jnp.sum(jnp.where(cands >= thr, jnp.exp(cands - top), 0.0), axis=0, keepdims=True)
            rank2 = _decode_rank(coded2, k).astype(BF16)
            m = jnp.zeros(rank2.shape, BF16)
            for l in range(k):
                n_sel = jnp.sum(jnp.where(amat + bmat[l:l + 1] >= thr, 1.0, 0.0), axis=0, keepdims=True)
                m = jnp.where(rank2 == l, _bf16_row_tile(n_sel, rank2.shape[0]), m)
            r1_ref[h, :, cols] = _decode_rank(coded1, k)
            alpha_ref[h, :, cols] = jnp.exp(s1 - amat[0:1]) * (0.5 / z)
            m_ref[h, lt] = m
            beta_ref[h, lt] = jnp.exp(s2 - bmat[0:1]).astype(BF16)
            return carry

        return lax.fori_loop(0, lane_tiles, tile, carry)

    lax.fori_loop(0, PEER_HEADS, head, 0)


def _peer_route(hnt, wqt, keys, tb):
    d, t = hnt.shape
    nq = wqt.shape[0]
    oshape = jax.ShapeDtypeStruct((PEER_HEADS, PEER_N_KEYS, t), F32)
    ospec = pl.BlockSpec((PEER_HEADS, PEER_N_KEYS, tb), lambda i: (0, 0, i))
    tshape = jax.ShapeDtypeStruct((PEER_HEADS, t // LANES, PEER_N_KEYS, LANES), BF16)
    tspec = pl.BlockSpec((PEER_HEADS, tb // LANES, PEER_N_KEYS, LANES), lambda i: (0, i, 0, 0))
    return pl.pallas_call(
        _peer_route_kernel,
        grid=(t // tb,),
        in_specs=[pl.BlockSpec((d, tb), lambda i: (0, i)),
                  pl.BlockSpec((nq, d), lambda i: (0, 0)),
                  pl.BlockSpec(keys.shape, lambda i: (0, 0, 0, 0))],
        out_specs=[ospec, ospec, tspec, tspec],
        out_shape=[oshape, oshape, tshape, tshape],
        scratch_shapes=[pltpu.VMEM((nq, tb), BF16), pltpu.VMEM((2, tb // LANES, PEER_N_KEYS, LANES), F32)],
        compiler_params=_cparams(("parallel",)),
        name="peer_route",
    )(hnt, wqt, keys)


def _peer_dense_kernel(hnt_ref, u_ref, vt_ref, r1_ref, alpha_ref, m_ref, beta_ref, o_ref,
                       hid_ref, act_ref, *, te, tb, n_tiles):
    j = pl.program_id(1)
    rows_per_tile = te // PEER_N_KEYS
    lane_tiles = tb // LANES
    zero = jnp.zeros((PEER_N_KEYS, LANES), BF16)

    def row_bcast(ref, h, r, lt):
        return _bf16_row_tile(ref[h, r:r + 1, lt * LANES:(lt + 1) * LANES], PEER_N_KEYS)

    def hidden():
        slot = j % 2
        hid = jnp.dot(u_ref[...], hnt_ref[...], preferred_element_type=F32)
        for lt in range(lane_tiles):
            hid_ref[slot, lt] = hid[:, lt * LANES:(lt + 1) * LANES]

    def activations():
        slot = (j + 1) % 2
        for lt in range(lane_tiles):
            for r in range(rows_per_tile):
                w = None
                for h in range(PEER_HEADS):
                    sel = row_bcast(r1_ref, h, r, lt) < m_ref[h, lt]
                    term = jnp.where(sel, beta_ref[h, lt], zero) * row_bcast(alpha_ref, h, r, lt)
                    w = term if w is None else w + term
                rows = slice(r * PEER_N_KEYS, (r + 1) * PEER_N_KEYS)
                hid_t = hid_ref[slot, lt, rows, :]
                gelu2 = hid_t * (1.0 + lax.erf(hid_t * (2.0 ** -0.5)))
                act_ref[slot, lt, rows, :] = gelu2.astype(BF16) * w

    def project():
        slot = j % 2
        act = jnp.concatenate([act_ref[slot, lt] for lt in range(lane_tiles)], axis=1)
        o_ref[...] += jnp.dot(vt_ref[...], act, preferred_element_type=F32)

    @pl.when(jnp.logical_and(j >= 2, j < n_tiles))
    def _():
        hidden()
        activations()
        project()

    @pl.when(j == 0)
    def _():
        o_ref[...] = jnp.zeros(o_ref.shape, o_ref.dtype)
        hidden()

    @pl.when(j == 1)
    def _():
        hidden()
        activations()

    @pl.when(j == n_tiles)
    def _():
        activations()
        project()

    @pl.when(j == n_tiles + 1)
    def _():
        project()


def _peer_dense(hnt, u_bf16, vt_bf16, rank1, alpha, n_sel, beta, tb, te):
    d, t = hnt.shape
    n_tiles = u_bf16.shape[0] // te
    assert n_tiles >= 2
    kern = functools.partial(_peer_dense_kernel, te=te, tb=tb, n_tiles=n_tiles)
    last = n_tiles - 1
    rspec = pl.BlockSpec((PEER_HEADS, tb // LANES, PEER_N_KEYS, LANES), lambda i, j: (0, i, 0, 0))
    i1spec = pl.BlockSpec((PEER_HEADS, te // PEER_N_KEYS, tb), lambda i, j: (0, jnp.clip(j - 1, 0, last), i))
    stage_shape = (2, tb // LANES, te, LANES)
    return pl.pallas_call(
        kern,
        grid=(t // tb, n_tiles + 2),
        in_specs=[pl.BlockSpec((d, tb), lambda i, j: (0, i)),
                  pl.BlockSpec((te, d), lambda i, j: (jnp.minimum(j, last), 0)),
                  pl.BlockSpec((d, te), lambda i, j: (0, jnp.clip(j - 2, 0, last))),
                  i1spec, i1spec, rspec, rspec],
        out_specs=pl.BlockSpec((d, tb), lambda i, j: (0, i)),
        out_shape=jax.ShapeDtypeStruct((d, t), F32),
        scratch_shapes=[pltpu.VMEM(stage_shape, F32), pltpu.VMEM(stage_shape, BF16)],
        compiler_params=_cparams(("parallel", "arbitrary")),
        name="peer_dense",
    )(hnt, u_bf16, vt_bf16, rank1, alpha, n_sel, beta)


def _final_norm_kernel(h_ref, pt_ref, g_ref, o_ref):
    o_ref[...] = _rms(h_ref[...] + pt_ref[...].T, g_ref[...])


def _final_norm(h2, peer_t, g, tb):
    t, d = h2.shape
    return pl.pallas_call(
        _final_norm_kernel,
        grid=(t // tb,),
        in_specs=[pl.BlockSpec((tb, d), lambda i: (i, 0)),
                  pl.BlockSpec((d, tb), lambda i: (0, i)),
                  pl.BlockSpec((1, d), lambda i: (0, 0))],
        out_specs=pl.BlockSpec((tb, d), lambda i: (i, 0)),
        out_shape=jax.ShapeDtypeStruct((t, d), F32),
        compiler_params=_cparams(("parallel",)),
        name="final_norm",
    )(h2, peer_t, g)


def _largest_divisor(n, cap, multiple):
    best = multiple
    k = multiple
    while k <= min(n, cap):
        if n % k == 0:
            best = k
        k += multiple
    return best


def kernel(x, norm1_g, w_in, diff_lambda_q1, diff_lambda_k1, diff_lambda_q2, diff_lambda_k2, diff_subln_g,
           rel_bias_table, hgrn_lb_logits, hgrn_gnorm_g, w_out, norm2_g, peer_w_q, peer_sub_keys, peer_u,
           peer_v, final_norm_g):
    batch, seq, d = x.shape
    t = batch * seq
    assert d == D_MODEL and seq % 256 == 0 and w_in.shape[0] == 1
    tb_proj = _largest_divisor(t, 512, 256)
    tq = 256
    tb_route = _largest_divisor(t, 512, 256)
    tb_peer = _largest_divisor(t, 1024, 512)
    te_peer = 1024

    x2 = x.reshape(t, d)
    row = lambda v: v.reshape(1, -1).astype(F32)

    lam = (jnp.exp(jnp.sum(diff_lambda_q1[0].astype(F32) * diff_lambda_k1[0].astype(F32)))
           - jnp.exp(jnp.sum(diff_lambda_q2[0].astype(F32) * diff_lambda_k2[0].astype(F32)))
           + LAM_INIT).reshape(1)
    lb = jnp.cumsum(jax.nn.softmax(hgrn_lb_logits.astype(F32), axis=0), axis=0)[0]
    lb = lb.reshape(HGRN_HEADS, 1, HEAD_W)
    table = rel_bias_table.astype(F32)

    qkv, hin = _norm_proj(x2, row(norm1_g[0]), w_in[0].astype(BF16), tb_proj)
    bias = _bias_tiles(table, tq)
    a = _diff_attn(lam, table, qkv.reshape(batch, seq, ATTN_COLS), bias, row(diff_subln_g[0]), batch, seq, tq)
    o = _hgrn(hin.reshape(batch, seq, HGRN_COLS), lb, row(hgrn_gnorm_g[0]), batch, seq)
    w_o = w_out[0].astype(BF16)
    h2, hnt = _out_proj(x2, a.reshape(t, DIFF_WIDTH), o.reshape(t, HGRN_WIDTH),
                        w_o[:DIFF_WIDTH], w_o[DIFF_WIDTH:], row(norm2_g[0]), tb_proj)
    rank1, alpha, n_sel, beta = _peer_route(hnt, peer_w_q[0].T.astype(BF16), peer_sub_keys[0].astype(BF16), tb_route)
    peer_t = _peer_dense(hnt, peer_u[0].astype(BF16), peer_v[0].T.astype(BF16), rank1, alpha, n_sel, beta,
                         tb_peer, te_peer)
    out = _final_norm(h2, peer_t, row(final_norm_g), tb_proj)
    return out.reshape(batch, seq, d)
```

```python
import functools
import math

import numpy as np
import jax
import jax.numpy as jnp
from jax import lax
from jax.experimental import pallas as pl
from jax.experimental.pallas import tpu as pltpu

F32 = jnp.float32
BF16 = jnp.bfloat16

D_MODEL = 1024
DIFF_HEADS = 4
DIFF_QK_DIM = 64
LANES = 128
MXU_WIDTH = 256
SECTION_ROWS = MXU_WIDTH
HEAD_W = 128
DIFF_WIDTH = DIFF_HEADS * HEAD_W
HGRN_HEADS = 4
HGRN_WIDTH = HGRN_HEADS * HEAD_W
HGRN_CHUNK = 32
ATTN_COLS = 3 * DIFF_WIDTH
HGRN_COLS = 4 * HGRN_WIDTH
REL_BUCKETS = 32
REL_MAX_DIST = 128
PEER_HEADS = 8
PEER_N_KEYS = 128
PEER_N_EXPERTS = PEER_N_KEYS * PEER_N_KEYS
PEER_QUERY_DIM = 256
PEER_TOPK = 16
NORM_EPS = 1e-6
LAM_INIT = 0.8 - 0.6 * math.exp(-0.3 * 0)
NEG_BIG = -1e30

VMEM_LIMIT_BYTES = 56 * 1024 * 1024


def _cparams(sem):
    return pltpu.CompilerParams(dimension_semantics=sem, vmem_limit_bytes=VMEM_LIMIT_BYTES)


def _rms(x, g):
    return x * lax.rsqrt(jnp.mean(x * x, axis=-1, keepdims=True) + NORM_EPS) * g


def _norm_proj_kernel(x_ref, g_ref, w_ref, oa_ref, oh_ref):
    y = _rms(x_ref[...], g_ref[...])
    p = jnp.dot(y.astype(BF16), w_ref[...], preferred_element_type=F32)
    oa_ref[...] = p[:, :ATTN_COLS].astype(BF16)
    oh_ref[...] = p[:, ATTN_COLS:]


def _norm_proj(x2, g, w_bf16, tb):
    t, d = x2.shape
    n = w_bf16.shape[1]
    return pl.pallas_call(
        _norm_proj_kernel,
        grid=(t // tb,),
        in_specs=[pl.BlockSpec((tb, d), lambda i: (i, 0)),
                  pl.BlockSpec((1, d), lambda i: (0, 0)),
                  pl.BlockSpec((d, n), lambda i: (0, 0))],
        out_specs=[pl.BlockSpec((tb, ATTN_COLS), lambda i: (i, 0)),
                   pl.BlockSpec((tb, HGRN_COLS), lambda i: (i, 0))],
        out_shape=[jax.ShapeDtypeStruct((t, ATTN_COLS), BF16),
                   jax.ShapeDtypeStruct((t, HGRN_COLS), F32)],
        compiler_params=_cparams(("parallel",)),
        name="norm_proj",
    )(x2, g, w_bf16)


def _t5_bucket_np(n):
    n = np.maximum(n, 0).astype(np.int32)
    max_exact = REL_BUCKETS // 2
    nf = np.maximum(n, max_exact).astype(np.float32)
    large = max_exact + (np.log(nf / np.float32(max_exact)) / np.float32(math.log(REL_MAX_DIST / max_exact))
                         * np.float32(REL_BUCKETS - max_exact)).astype(np.int32)
    large = np.minimum(large, REL_BUCKETS - 1)
    return np.where(n < max_exact, n, large).astype(np.int32)


def _bias_tiles_kernel(table_ref, bucket_ref, o_ref):
    h = pl.program_id(0)
    bucket = bucket_ref[...]
    acc = jnp.zeros(bucket.shape, F32)
    for b in range(REL_BUCKETS):
        acc = jnp.where(bucket == b, table_ref[b, h], acc)
    o_ref[...] = acc


def _bias_tiles(table, tq):
    r = np.arange(tq)[:, None]
    c = np.arange(tq)[None, :]
    tiles = np.stack([_t5_bucket_np(d * tq + r - c) for d in (0, 1)])
    tiles = np.concatenate([tiles, tiles], axis=1)
    return pl.pallas_call(
        _bias_tiles_kernel,
        grid=(DIFF_HEADS,),
        in_specs=[pl.BlockSpec(memory_space=pltpu.SMEM),
                  pl.BlockSpec((2, 2 * tq, tq), lambda h: (0, 0, 0))],
        out_specs=pl.BlockSpec((None, 2, 2 * tq, tq), lambda h: (h, 0, 0, 0)),
        out_shape=jax.ShapeDtypeStruct((DIFF_HEADS, 2, 2 * tq, tq), F32),
        compiler_params=_cparams(("arbitrary",)),
        name="bias_tiles",
    )(table, jnp.asarray(tiles))


def _diff_attn_kernel(lam_ref, table_ref, q_ref, k_ref, v_ref, bias_ref, g_ref, o_ref,
                      s_ref, mx_ref, l_ref, acc_ref, *, tq):
    h = pl.program_id(1)
    i = pl.program_id(2)
    q = q_ref[...] * jnp.asarray(DIFF_QK_DIM ** -0.5, BF16)
    lane = lax.broadcasted_iota(jnp.int32, q.shape, 1)
    zero = jnp.zeros_like(q)
    q2 = jnp.concatenate([jnp.where(lane < DIFF_QK_DIM, q, zero),
                          jnp.where(lane >= DIFF_QK_DIM, q, zero)], axis=0)
    far_bias = table_ref[REL_BUCKETS - 1, h]
    lane_halves = tq // LANES

    def lanewise(op, x):
        out = x[:, :LANES]
        for k in range(1, lane_halves):
            out = op(out, x[:, k * LANES:(k + 1) * LANES])
        return out

    def scores(j, bias, causal):
        start = pl.multiple_of(j * tq, tq)
        s = lax.dot_general(q2, k_ref[pl.ds(start, tq), :], (((1,), (1,)), ((), ())),
                            preferred_element_type=F32) + bias
        if causal:
            r = lax.broadcasted_iota(jnp.int32, s.shape, 0) % tq
            c = lax.broadcasted_iota(jnp.int32, s.shape, 1)
            s = jnp.where(c <= r, s, NEG_BIG)
        s_ref[j] = s
        mx_ref[...] = jnp.maximum(mx_ref[...], lanewise(jnp.maximum, s))

    mx_ref[...] = jnp.full(mx_ref.shape, NEG_BIG, F32)
    l_ref[...] = jnp.zeros(l_ref.shape, F32)
    acc_ref[...] = jnp.zeros(acc_ref.shape, F32)

    def far_pair(p, carry):
        start = pl.multiple_of(p * 2 * tq, 2 * tq)
        s = lax.dot_general(q2, k_ref[pl.ds(start, 2 * tq), :], (((1,), (1,)), ((), ())),
                            preferred_element_type=F32) + far_bias
        s_ref[2 * p] = s[:, :tq]
        s_ref[2 * p + 1] = s[:, tq:]
        mx_ref[...] = jnp.maximum(mx_ref[...], jnp.maximum(lanewise(jnp.maximum, s[:, :tq]),
                                                           lanewise(jnp.maximum, s[:, tq:])))
        return carry

    n_far = jnp.maximum(i - 1, 0)
    lax.fori_loop(0, n_far // 2, far_pair, 0)

    @pl.when(n_far % 2 == 1)
    def _():
        scores(n_far - 1, far_bias, False)

    @pl.when(i >= 1)
    def _():
        scores(i - 1, bias_ref[1], False)

    scores(i, bias_ref[0], True)
    m = jnp.max(mx_ref[...], axis=-1, keepdims=True)

    def weigh_pair(p, carry):
        start = pl.multiple_of(p * 2 * tq, 2 * tq)
        e = jnp.exp(jnp.concatenate([s_ref[2 * p], s_ref[2 * p + 1]], axis=1) - m)
        l_ref[...] += lanewise(jnp.add, e[:, :tq]) + lanewise(jnp.add, e[:, tq:])
        acc_ref[...] += jnp.dot(e.astype(BF16), v_ref[pl.ds(start, 2 * tq), :], preferred_element_type=F32)
        return carry

    n_blocks = i + 1
    lax.fori_loop(0, n_blocks // 2, weigh_pair, 0)

    @pl.when(n_blocks % 2 == 1)
    def _():
        start = pl.multiple_of(i * tq, tq)
        e = jnp.exp(s_ref[i] - m)
        l_ref[...] += lanewise(jnp.add, e)
        acc_ref[...] += jnp.dot(e.astype(BF16), v_ref[pl.ds(start, tq), :], preferred_element_type=F32)
    l = jnp.sum(l_ref[...], axis=-1, keepdims=True)
    acc = acc_ref[...]
    lam = lam_ref[0]
    out = acc[:tq] / l[:tq] - lam * (acc[tq:] / l[tq:])
    o_ref[...] = (_rms(out, g_ref[...]) * (1.0 - LAM_INIT)).astype(o_ref.dtype)


def _diff_attn(lam, table, qkv, bias, g, batch, seq, tq):
    kern = functools.partial(_diff_attn_kernel, tq=tq)
    return pl.pallas_call(
        kern,
        grid=(batch, DIFF_HEADS, seq // tq),
        in_specs=[pl.BlockSpec(memory_space=pltpu.SMEM),
                  pl.BlockSpec(memory_space=pltpu.SMEM),
                  pl.BlockSpec((None, tq, HEAD_W), lambda b, h, i: (b, i, h)),
                  pl.BlockSpec((None, seq, HEAD_W), lambda b, h, i: (b, 0, DIFF_HEADS + h)),
                  pl.BlockSpec((None, seq, HEAD_W), lambda b, h, i: (b, 0, 2 * DIFF_HEADS + h)),
                  pl.BlockSpec((None, 2, 2 * tq, tq), lambda b, h, i: (h, 0, 0, 0)),
                  pl.BlockSpec((1, HEAD_W), lambda b, h, i: (0, 0))],
        out_specs=pl.BlockSpec((None, tq, HEAD_W), lambda b, h, i: (b, i, h)),
        out_shape=jax.ShapeDtypeStruct((batch, seq, DIFF_WIDTH), BF16),
        scratch_shapes=[pltpu.VMEM((seq // tq, 2 * tq, tq), F32),
                        pltpu.VMEM((2 * tq, LANES), F32),
                        pltpu.VMEM((2 * tq, LANES), F32),
                        pltpu.VMEM((2 * tq, HEAD_W), F32)],
        compiler_params=_cparams(("parallel", "parallel", "arbitrary")),
        name="diff_attn",
    )(lam, table, qkv, qkv, qkv, bias, g)


def _hgrn_kernel(hq_ref, hf_ref, hi_ref, hg_ref, lb_ref, g_ref, o_ref,
                 qd_ref, kd_ref, cd_ref, oi_ref, upd_ref, prev_ref, *, seq):
    c_len = HGRN_CHUNK
    n_chunks = seq // c_len
    lb = lb_ref[...]
    f = lb + (1.0 - lb) * jax.nn.sigmoid(hf_ref[...])
    log_f = jnp.log(f)
    kk = 1.0 - f
    hq = hq_ref[...]
    qq = hq * jax.nn.sigmoid(hq)
    vv = hi_ref[...]

    pos = lax.broadcasted_iota(jnp.int32, (seq, HEAD_W), 0) % c_len
    b = log_f
    sh = 1
    while sh < c_len:
        b = b + jnp.where(pos >= sh, pltpu.roll(b, sh, axis=0), 0.0)
        sh *= 2
    b3 = b.reshape(n_chunks, c_len, HEAD_W)
    b_last = b3[:, c_len - 1:c_len, :]
    q_dec = qq * jnp.exp(b)
    k_inv = kk * jnp.exp(-b)
    k_dec = kk * jnp.exp(b_last - b3).reshape(seq, HEAD_W)
    qd_ref[...] = q_dec.astype(BF16)
    kd_ref[...] = k_dec.astype(BF16)
    cd_ref[...] = jnp.exp(b_last.reshape(n_chunks, HEAD_W))

    blk = HEAD_W
    nb = seq // blk
    qd_b = q_dec.astype(BF16).reshape(nb, blk, HEAD_W)
    ki_b = k_inv.astype(BF16).reshape(nb, blk, HEAD_W)
    scores = jnp.einsum('nqd,nkd->nqk', qd_b, ki_b, preferred_element_type=F32)
    r = lax.broadcasted_iota(jnp.int32, (blk, blk), 0)
    c = lax.broadcasted_iota(jnp.int32, (blk, blk), 1)
    keep = (c <= r) & ((r // c_len) == (c // c_len))
    scores = jnp.where(keep[None], scores, 0.0)
    o_intra = jnp.einsum('nqk,nkv->nqv', scores.astype(BF16), vv.astype(BF16).reshape(nb, blk, HEAD_W),
                         preferred_element_type=F32).reshape(seq, HEAD_W)

    unroll = 4

    def rows_of(ci):
        return pl.ds(pl.multiple_of(ci * c_len, c_len), c_len)

    def updates(g, carry):
        for u in range(unroll):
            ci = g * unroll + u
            vc = hi_ref[rows_of(ci), :].astype(BF16)
            upd_ref[ci] = lax.dot_general(vc, kd_ref[rows_of(ci), :], (((0,), (0,)), ((), ())),
                                          preferred_element_type=F32)
        return carry

    lax.fori_loop(0, n_chunks // unroll, updates, 0)

    def recur(ci, st):
        prev_ref[ci] = st.astype(BF16)
        return st * cd_ref[pl.ds(ci, 1), :] + upd_ref[ci]

    lax.fori_loop(0, n_chunks, recur, jnp.zeros((HEAD_W, HEAD_W), F32))

    def reads(g, carry):
        for u in range(unroll):
            ci = g * unroll + u
            oi_ref[rows_of(ci), :] = lax.dot_general(qd_ref[rows_of(ci), :], prev_ref[ci],
                                                     (((1,), (1,)), ((), ())), preferred_element_type=F32)
        return carry

    lax.fori_loop(0, n_chunks // unroll, reads, 0)

    o = o_intra + oi_ref[...]
    hg = hg_ref[...]
    o_ref[...] = (_rms(o, g_ref[...]) * (hg * jax.nn.sigmoid(hg))).astype(o_ref.dtype)


def _hgrn(hin, lb, g, batch, seq):
    kern = functools.partial(_hgrn_kernel, seq=seq)
    col = lambda k: (lambda b, h: (b, 0, k * HGRN_HEADS + h))
    blk = (None, seq, HEAD_W)
    return pl.pallas_call(
        kern,
        grid=(batch, HGRN_HEADS),
        in_specs=[pl.BlockSpec(blk, col(0)), pl.BlockSpec(blk, col(1)),
                  pl.BlockSpec(blk, col(2)), pl.BlockSpec(blk, col(3)),
                  pl.BlockSpec((None, 1, HEAD_W), lambda b, h: (h, 0, 0)),
                  pl.BlockSpec((1, HEAD_W), lambda b, h: (0, 0))],
        out_specs=pl.BlockSpec(blk, lambda b, h: (b, 0, h)),
        out_shape=jax.ShapeDtypeStruct((batch, seq, HGRN_WIDTH), BF16),
        scratch_shapes=[pltpu.VMEM((seq, HEAD_W), BF16), pltpu.VMEM((seq, HEAD_W), BF16),
                        pltpu.VMEM((seq // HGRN_CHUNK, HEAD_W), F32), pltpu.VMEM((seq, HEAD_W), F32),
                        pltpu.VMEM((seq // HGRN_CHUNK, HEAD_W, HEAD_W), F32),
                        pltpu.VMEM((seq // HGRN_CHUNK, HEAD_W, HEAD_W), BF16)],
        compiler_params=_cparams(("parallel", "parallel")),
        name="hgrn",
    )(hin, hin, hin, hin, lb, g)


def _out_proj_kernel(x_ref, a_ref, o_ref, wa_ref, wo_ref, g_ref, h_ref, hnt_ref):
    h = (x_ref[...]
         + jnp.dot(a_ref[...], wa_ref[...], preferred_element_type=F32)
         + jnp.dot(o_ref[...], wo_ref[...], preferred_element_type=F32))
    h_ref[...] = h
    hnt_ref[...] = _rms(h, g_ref[...]).T.astype(BF16)


def _out_proj(x2, a2, o2, wa, wo, g, tb):
    t, d = x2.shape
    return pl.pallas_call(
        _out_proj_kernel,
        grid=(t // tb,),
        in_specs=[pl.BlockSpec((tb, d), lambda i: (i, 0)),
                  pl.BlockSpec((tb, DIFF_WIDTH), lambda i: (i, 0)),
                  pl.BlockSpec((tb, HGRN_WIDTH), lambda i: (i, 0)),
                  pl.BlockSpec((DIFF_WIDTH, d), lambda i: (0, 0)),
                  pl.BlockSpec((HGRN_WIDTH, d), lambda i: (0, 0)),
                  pl.BlockSpec((1, d), lambda i: (0, 0))],
        out_specs=[pl.BlockSpec((tb, d), lambda i: (i, 0)),
                   pl.BlockSpec((d, tb), lambda i: (0, i))],
        out_shape=[jax.ShapeDtypeStruct((t, d), F32),
                   jax.ShapeDtypeStruct((d, t), BF16)],
        compiler_params=_cparams(("parallel",)),
        name="out_proj",
    )(x2, a2, o2, wa, wo, g)


SUBLANES = 8
TAU_GUARD = 2.0 ** -21


def _oddeven_merge(lo, hi, r):
    step = r * 2
    if step < hi - lo:
        yield from _oddeven_merge(lo, hi, step)
        yield from _oddeven_merge(lo + r, hi, step)
        yield from [(i, i + r) for i in range(lo + r, hi - r, step)]
    else:
        yield (lo, lo + r)


def _oddeven_merge_sort(lo, hi):
    if hi - lo >= 1:
        mid = lo + (hi - lo) // 2
        yield from _oddeven_merge_sort(lo, mid)
        yield from _oddeven_merge_sort(mid + 1, hi)
        yield from _oddeven_merge(lo, hi, 1)


def _bitonic_merge(n):
    d = n // 2
    while d >= 1:
        yield from [(i, i + d) for i in range(n) if (i // d) % 2 == 0]
        d //= 2


SORT16 = tuple(_oddeven_merge_sort(0, PEER_TOPK - 1))
MERGE16 = tuple(_bitonic_merge(PEER_TOPK))


def _compare_exchange(v, pairs):
    for i, j in pairs:
        v[i], v[j] = jnp.maximum(v[i], v[j]), jnp.minimum(v[i], v[j])


def _top16_sorted(x):
    v = [x[SUBLANES * k:SUBLANES * (k + 1)] for k in range(PEER_TOPK)]
    _compare_exchange(v, SORT16)
    shift = SUBLANES // 2
    while shift >= 1:
        v = [jnp.maximum(v[k], pltpu.roll(v[PEER_TOPK - 1 - k], shift, axis=0)) for k in range(PEER_TOPK)]
        _compare_exchange(v, MERGE16)
        shift //= 2
    return v


def _rows_on_sublanes(rep, start):
    sub = lax.broadcasted_iota(jnp.int32, rep[0].shape, 0)
    out = rep[start]
    for s in range(1, SUBLANES):
        out = jnp.where(sub == s, rep[start + s], out)
    return out


def _pair_candidates(a_rep, b_rep):
    k = PEER_TOPK
    sub = lax.broadcasted_iota(jnp.int32, a_rep[0].shape, 0)
    b_lo = _rows_on_sublanes(b_rep, 0)
    b_hi = _rows_on_sublanes(b_rep, SUBLANES)
    a_hi = _rows_on_sublanes(a_rep, SUBLANES)
    pieces = [a_rep[0] + b_lo, a_rep[0] + b_hi, a_rep[1] + b_lo]
    for i in range(2, SUBLANES):
        pieces.append(jnp.where(sub < k // (i + 1), a_rep[i] + b_lo, -jnp.inf))
    pieces.append(a_hi + b_rep[0])
    return jnp.concatenate(pieces, axis=0)


def _kth_largest(x, k):
    for r in range(k):
        m = jnp.max(x, axis=0, keepdims=True)
        if r + 1 < k:
            x = jnp.where(x == m, -jnp.inf, x)
    return m


def _peer_route_kernel(hnt_ref, wqt_ref, keys_ref, tau_ref, alpha_ref, s2_ref, beta_ref, qt_ref, s_ref):
    qt_ref[...] = jnp.dot(wqt_ref[...], hnt_ref[...], preferred_element_type=F32).astype(BF16)
    half = PEER_QUERY_DIM // 2
    lane_tiles = hnt_ref.shape[1] // LANES
    k = PEER_TOPK

    def head(h, carry):
        base = pl.multiple_of(h * PEER_QUERY_DIM, PEER_QUERY_DIM)
        for c in range(2):
            s = jnp.dot(keys_ref[h, c], qt_ref[pl.ds(base + c * half, half), :], preferred_element_type=F32)
            for lt in range(lane_tiles):
                s_ref[c, lt] = s[:, lt * LANES:(lt + 1) * LANES]

        def tile(lt, carry):
            cols = pl.ds(pl.multiple_of(lt * LANES, LANES), LANES)
            s1 = s_ref[0, lt]
            s2 = s_ref[1, lt]
            a_rep = _top16_sorted(s1)
            b_rep = _top16_sorted(s2)
            cands = _pair_candidates(a_rep, b_rep)
            thr = _kth_largest(cands, k)
            top = a_rep[0][0:1] + b_rep[0][0:1]
            z = jnp.sum(jnp.where(cands >= thr, jnp.exp(cands - top), 0.0), axis=0, keepdims=True)
            tau_ref[h, :, cols] = (thr - s1) - (jnp.abs(thr) + jnp.abs(s1)) * TAU_GUARD
            alpha_ref[h, :, cols] = jnp.exp(s1 - a_rep[0][0:1]) * (0.5 / z)
            s2_ref[h, lt] = s2
            beta_ref[h, lt] = jnp.exp(s2 - b_rep[0][0:1])
            return carry

        return lax.fori_loop(0, lane_tiles, tile, carry)

    lax.fori_loop(0, PEER_HEADS, head, 0)


def _peer_route(hnt, wqt, keys, tb):
    d, t = hnt.shape
    nq = wqt.shape[0]
    oshape = jax.ShapeDtypeStruct((PEER_HEADS, PEER_N_KEYS, t), F32)
    ospec = pl.BlockSpec((PEER_HEADS, PEER_N_KEYS, tb), lambda i: (0, 0, i))
    tshape = jax.ShapeDtypeStruct((PEER_HEADS, t // LANES, PEER_N_KEYS, LANES), F32)
    tspec = pl.BlockSpec((PEER_HEADS, tb // LANES, PEER_N_KEYS, LANES), lambda i: (0, i, 0, 0))
    return pl.pallas_call(
        _peer_route_kernel,
        grid=(t // tb,),
        in_specs=[pl.BlockSpec((d, tb), lambda i: (0, i)),
                  pl.BlockSpec((nq, d), lambda i: (0, 0)),
                  pl.BlockSpec(keys.shape, lambda i: (0, 0, 0, 0))],
        out_specs=[ospec, ospec, tspec, tspec],
        out_shape=[oshape, oshape, tshape, tshape],
        scratch_shapes=[pltpu.VMEM((nq, tb), BF16), pltpu.VMEM((2, tb // LANES, PEER_N_KEYS, LANES), F32)],
        compiler_params=_cparams(("parallel",)),
        name="peer_route",
    )(hnt, wqt, keys)


def _peer_dense_kernel(hnt_ref, u_ref, vt_ref, tau_ref, alpha_ref, s2_ref, beta_ref, o_ref,
                       hid_ref, act_ref, *, te, tb, n_tiles):
    j = pl.program_id(1)
    rows_per_tile = te // PEER_N_KEYS
    lane_tiles = tb // LANES

    def hidden():
        slot = j % 2
        hid = jnp.dot(u_ref[...], hnt_ref[...], preferred_element_type=F32)
        for lt in range(lane_tiles):
            hid_ref[slot, lt] = hid[:, lt * LANES:(lt + 1) * LANES]

    def activations():
        slot = (j + 1) % 2
        for lt in range(lane_tiles):
            for r in range(rows_per_tile):
                w = None
                cols = slice(lt * LANES, (lt + 1) * LANES)
                for h in range(PEER_HEADS):
                    sel = s2_ref[h, lt] >= tau_ref[h, r:r + 1, cols]
                    term = jnp.where(sel, beta_ref[h, lt], 0.0) * alpha_ref[h, r:r + 1, cols]
                    w = term if w is None else w + term
                rows = slice(r * PEER_N_KEYS, (r + 1) * PEER_N_KEYS)
                hid_t = hid_ref[slot, lt, rows, :]
                gelu2 = hid_t * (1.0 + lax.erf(hid_t * (2.0 ** -0.5)))
                act_ref[slot, lt, rows, :] = (gelu2 * w).astype(BF16)

    def project():
        slot = j % 2
        act = jnp.concatenate([act_ref[slot, lt] for lt in range(lane_tiles)], axis=1)
        o_ref[...] += jnp.dot(vt_ref[...], act, preferred_element_type=F32)

    @pl.when(jnp.logical_and(j >= 2, j < n_tiles))
    def _():
        hidden()
        activations()
        project()

    @pl.when(j == 0)
    def _():
        o_ref[...] = jnp.zeros(o_ref.shape, o_ref.dtype)
        hidden()

    @pl.when(j == 1)
    def _():
        hidden()
        activations()

    @pl.when(j == n_tiles)
    def _():
        activations()
        project()

    @pl.when(j == n_tiles + 1)
    def _():
        project()


def _peer_dense(hnt, u_bf16, vt_bf16, tau, alpha, s2, beta, tb, te):
    d, t = hnt.shape
    n_tiles = u_bf16.shape[0] // te
    assert n_tiles >= 2
    kern = functools.partial(_peer_dense_kernel, te=te, tb=tb, n_tiles=n_tiles)
    last = n_tiles - 1
    rspec = pl.BlockSpec((PEER_HEADS, tb // LANES, PEER_N_KEYS, LANES), lambda i, j: (0, i, 0, 0))
    i1spec = pl.BlockSpec((PEER_HEADS, te // PEER_N_KEYS, tb), lambda i, j: (0, jnp.clip(j - 1, 0, last), i))
    stage_shape = (2, tb // LANES, te, LANES)
    return pl.pallas_call(
        kern,
        grid=(t // tb, n_tiles + 2),
        in_specs=[pl.BlockSpec((d, tb), lambda i, j: (0, i)),
                  pl.BlockSpec((te, d), lambda i, j: (jnp.minimum(j, last), 0)),
                  pl.BlockSpec((d, te), lambda i, j: (0, jnp.clip(j - 2, 0, last))),
                  i1spec, i1spec, rspec, rspec],
        out_specs=pl.BlockSpec((d, tb), lambda i, j: (0, i)),
        out_shape=jax.ShapeDtypeStruct((d, t), F32),
        scratch_shapes=[pltpu.VMEM(stage_shape, F32), pltpu.VMEM(stage_shape, BF16)],
        compiler_params=_cparams(("parallel", "arbitrary")),
        name="peer_dense",
    )(hnt, u_bf16, vt_bf16, tau, alpha, s2, beta)


def _final_norm_kernel(h_ref, pt_ref, g_ref, o_ref):
    o_ref[...] = _rms(h_ref[...] + pt_ref[...].T, g_ref[...])


def _final_norm(h2, peer_t, g, tb):
    t, d = h2.shape
    return pl.pallas_call(
        _final_norm_kernel,
        grid=(t // tb,),
        in_specs=[pl.BlockSpec((tb, d), lambda i: (i, 0)),
                  pl.BlockSpec((d, tb), lambda i: (0, i)),
                  pl.BlockSpec((1, d), lambda i: (0, 0))],
        out_specs=pl.BlockSpec((tb, d), lambda i: (i, 0)),
        out_shape=jax.ShapeDtypeStruct((t, d), F32),
        compiler_params=_cparams(("parallel",)),
        name="final_norm",
    )(h2, peer_t, g)


def _largest_divisor(n, cap, multiple):
    best = multiple
    k = multiple
    while k <= min(n, cap):
        if n % k == 0:
            best = k
        k += multiple
    return best


def kernel(x, norm1_g, w_in, diff_lambda_q1, diff_lambda_k1, diff_lambda_q2, diff_lambda_k2, diff_subln_g,
           rel_bias_table, hgrn_lb_logits, hgrn_gnorm_g, w_out, norm2_g, peer_w_q, peer_sub_keys, peer_u,
           peer_v, final_norm_g):
    batch, seq, d = x.shape
    t = batch * seq
    assert d == D_MODEL and seq % 256 == 0 and w_in.shape[0] == 1
    tb_proj = _largest_divisor(t, 512, 256)
    tq = 256
    tb_route = _largest_divisor(t, 512, 256)
    tb_peer = _largest_divisor(t, 1024, 512)
    te_peer = 1024

    x2 = x.reshape(t, d)
    row = lambda v: v.reshape(1, -1).astype(F32)

    lam = (jnp.exp(jnp.sum(diff_lambda_q1[0].astype(F32) * diff_lambda_k1[0].astype(F32)))
           - jnp.exp(jnp.sum(diff_lambda_q2[0].astype(F32) * diff_lambda_k2[0].astype(F32)))
           + LAM_INIT).reshape(1)
    lb = jnp.cumsum(jax.nn.softmax(hgrn_lb_logits.astype(F32), axis=0), axis=0)[0]
    lb = lb.reshape(HGRN_HEADS, 1, HEAD_W)
    table = rel_bias_table.astype(F32)

    qkv, hin = _norm_proj(x2, row(norm1_g[0]), w_in[0].astype(BF16), tb_proj)
    bias = _bias_tiles(table, tq)
    a = _diff_attn(lam, table, qkv.reshape(batch, seq, ATTN_COLS), bias, row(diff_subln_g[0]), batch, seq, tq)
    o = _hgrn(hin.reshape(batch, seq, HGRN_COLS), lb, row(hgrn_gnorm_g[0]), batch, seq)
    w_o = w_out[0].astype(BF16)
    h2, hnt = _out_proj(x2, a.reshape(t, DIFF_WIDTH), o.reshape(t, HGRN_WIDTH),
                        w_o[:DIFF_WIDTH], w_o[DIFF_WIDTH:], row(norm2_g[0]), tb_proj)
    tau, alpha, s2, beta = _peer_route(hnt, peer_w_q[0].T.astype(BF16), peer_sub_keys[0].astype(BF16), tb_route)
    peer_t = _peer_dense(hnt, peer_u[0].astype(BF16), peer_v[0].T.astype(BF16), tau, alpha, s2, beta,
                         tb_peer, te_peer)
    out = _final_norm(h2, peer_t, row(final_norm_g), tb_proj)
    return out.reshape(batch, seq, d)
```

```python
import functools
import math

import numpy as np
import jax
import jax.numpy as jnp
from jax import lax
from jax.experimental import pallas as pl
from jax.experimental.pallas import tpu as pltpu

F32 = jnp.float32
BF16 = jnp.bfloat16

D_MODEL = 1024
DIFF_HEADS = 4
DIFF_QK_DIM = 64
LANES = 128
MXU_WIDTH = 256
SECTION_ROWS = MXU_WIDTH
HEAD_W = 128
DIFF_WIDTH = DIFF_HEADS * HEAD_W
HGRN_HEADS = 4
HGRN_WIDTH = HGRN_HEADS * HEAD_W
HGRN_CHUNK = 32
ATTN_COLS = 3 * DIFF_WIDTH
HGRN_COLS = 4 * HGRN_WIDTH
REL_BUCKETS = 32
REL_MAX_DIST = 128
PEER_HEADS = 8
PEER_N_KEYS = 128
PEER_N_EXPERTS = PEER_N_KEYS * PEER_N_KEYS
PEER_QUERY_DIM = 256
PEER_TOPK = 16
NORM_EPS = 1e-6
LAM_INIT = 0.8 - 0.6 * math.exp(-0.3 * 0)
NEG_BIG = -1e30

VMEM_LIMIT_BYTES = 56 * 1024 * 1024


def _cparams(sem):
    return pltpu.CompilerParams(dimension_semantics=sem, vmem_limit_bytes=VMEM_LIMIT_BYTES)


def _rms(x, g):
    return x * lax.rsqrt(jnp.mean(x * x, axis=-1, keepdims=True) + NORM_EPS) * g


def _norm_proj_kernel(x_ref, g_ref, w_ref, oa_ref, oh_ref):
    y = _rms(x_ref[...], g_ref[...])
    p = jnp.dot(y.astype(BF16), w_ref[...], preferred_element_type=F32)
    oa_ref[...] = p[:, :ATTN_COLS].astype(BF16)
    oh_ref[...] = p[:, ATTN_COLS:]


def _norm_proj(x2, g, w_bf16, tb):
    t, d = x2.shape
    n = w_bf16.shape[1]
    return pl.pallas_call(
        _norm_proj_kernel,
        grid=(t // tb,),
        in_specs=[pl.BlockSpec((tb, d), lambda i: (i, 0)),
                  pl.BlockSpec((1, d), lambda i: (0, 0)),
                  pl.BlockSpec((d, n), lambda i: (0, 0))],
        out_specs=[pl.BlockSpec((tb, ATTN_COLS), lambda i: (i, 0)),
                   pl.BlockSpec((tb, HGRN_COLS), lambda i: (i, 0))],
        out_shape=[jax.ShapeDtypeStruct((t, ATTN_COLS), BF16),
                   jax.ShapeDtypeStruct((t, HGRN_COLS), F32)],
        compiler_params=_cparams(("parallel",)),
        name="norm_proj",
    )(x2, g, w_bf16)


def _t5_bucket_np(n):
    n = np.maximum(n, 0).astype(np.int32)
    max_exact = REL_BUCKETS // 2
    nf = np.maximum(n, max_exact).astype(np.float32)
    large = max_exact + (np.log(nf / np.float32(max_exact)) / np.float32(math.log(REL_MAX_DIST / max_exact))
                         * np.float32(REL_BUCKETS - max_exact)).astype(np.int32)
    large = np.minimum(large, REL_BUCKETS - 1)
    return np.where(n < max_exact, n, large).astype(np.int32)


def _bias_tiles_kernel(table_ref, bucket_ref, o_ref):
    h = pl.program_id(0)
    bucket = bucket_ref[...]
    acc = jnp.zeros(bucket.shape, F32)
    for b in range(REL_BUCKETS):
        acc = jnp.where(bucket == b, table_ref[b, h], acc)
    o_ref[...] = acc


def _bias_tiles(table, tq):
    r = np.arange(tq)[:, None]
    c = np.arange(tq)[None, :]
    tiles = np.stack([_t5_bucket_np(d * tq + r - c) for d in (0, 1)])
    tiles = np.concatenate([tiles, tiles], axis=1)
    return pl.pallas_call(
        _bias_tiles_kernel,
        grid=(DIFF_HEADS,),
        in_specs=[pl.BlockSpec(memory_space=pltpu.SMEM),
                  pl.BlockSpec((2, 2 * tq, tq), lambda h: (0, 0, 0))],
        out_specs=pl.BlockSpec((None, 2, 2 * tq, tq), lambda h: (h, 0, 0, 0)),
        out_shape=jax.ShapeDtypeStruct((DIFF_HEADS, 2, 2 * tq, tq), F32),
        compiler_params=_cparams(("arbitrary",)),
        name="bias_tiles",
    )(table, jnp.asarray(tiles))


def _diff_attn_kernel(lam_ref, table_ref, q_ref, k_ref, v_ref, bias_ref, g_ref, o_ref,
                      s_ref, mx_ref, l_ref, acc_ref, *, tq):
    h = pl.program_id(1)
    i = pl.program_id(2)
    q = q_ref[...] * jnp.asarray(DIFF_QK_DIM ** -0.5, BF16)
    lane = lax.broadcasted_iota(jnp.int32, q.shape, 1)
    zero = jnp.zeros_like(q)
    q2 = jnp.concatenate([jnp.where(lane < DIFF_QK_DIM, q, zero),
                          jnp.where(lane >= DIFF_QK_DIM, q, zero)], axis=0)
    far_bias = table_ref[REL_BUCKETS - 1, h]
    lane_halves = tq // LANES

    def lanewise(op, x):
        out = x[:, :LANES]
        for k in range(1, lane_halves):
            out = op(out, x[:, k * LANES:(k + 1) * LANES])
        return out

    def scores(j, bias, causal):
        start = pl.multiple_of(j * tq, tq)
        s = lax.dot_general(q2, k_ref[pl.ds(start, tq), :], (((1,), (1,)), ((), ())),
                            preferred_element_type=F32) + bias
        if causal:
            r = lax.broadcasted_iota(jnp.int32, s.shape, 0) % tq
            c = lax.broadcasted_iota(jnp.int32, s.shape, 1)
            s = jnp.where(c <= r, s, NEG_BIG)
        s_ref[j] = s
        mx_ref[...] = jnp.maximum(mx_ref[...], lanewise(jnp.maximum, s))

    mx_ref[...] = jnp.full(mx_ref.shape, NEG_BIG, F32)
    l_ref[...] = jnp.zeros(l_ref.shape, F32)
    acc_ref[...] = jnp.zeros(acc_ref.shape, F32)

    def far_pair(p, carry):
        start = pl.multiple_of(p * 2 * tq, 2 * tq)
        s = lax.dot_general(q2, k_ref[pl.ds(start, 2 * tq), :], (((1,), (1,)), ((), ())),
                            preferred_element_type=F32) + far_bias
        s_ref[2 * p] = s[:, :tq]
        s_ref[2 * p + 1] = s[:, tq:]
        mx_ref[...] = jnp.maximum(mx_ref[...], jnp.maximum(lanewise(jnp.maximum, s[:, :tq]),
                                                           lanewise(jnp.maximum, s[:, tq:])))
        return carry

    n_far = jnp.maximum(i - 1, 0)
    lax.fori_loop(0, n_far // 2, far_pair, 0)

    @pl.when(n_far % 2 == 1)
    def _():
        scores(n_far - 1, far_bias, False)

    @pl.when(i >= 1)
    def _():
        scores(i - 1, bias_ref[1], False)

    scores(i, bias_ref[0], True)
    m = jnp.max(mx_ref[...], axis=-1, keepdims=True)

    def weigh_pair(p, carry):
        start = pl.multiple_of(p * 2 * tq, 2 * tq)
        e = jnp.exp(jnp.concatenate([s_ref[2 * p], s_ref[2 * p + 1]], axis=1) - m)
        l_ref[...] += lanewise(jnp.add, e[:, :tq]) + lanewise(jnp.add, e[:, tq:])
        acc_ref[...] += jnp.dot(e.astype(BF16), v_ref[pl.ds(start, 2 * tq), :], preferred_element_type=F32)
        return carry

    n_blocks = i + 1
    lax.fori_loop(0, n_blocks // 2, weigh_pair, 0)

    @pl.when(n_blocks % 2 == 1)
    def _():
        start = pl.multiple_of(i * tq, tq)
        e = jnp.exp(s_ref[i] - m)
        l_ref[...] += lanewise(jnp.add, e)
        acc_ref[...] += jnp.dot(e.astype(BF16), v_ref[pl.ds(start, tq), :], preferred_element_type=F32)
    l = jnp.sum(l_ref[...], axis=-1, keepdims=True)
    acc = acc_ref[...]
    lam = lam_ref[0]
    out = acc[:tq] / l[:tq] - lam * (acc[tq:] / l[tq:])
    o_ref[...] = (_rms(out, g_ref[...]) * (1.0 - LAM_INIT)).astype(o_ref.dtype)


def _diff_attn(lam, table, qkv, bias, g, batch, seq, tq):
    kern = functools.partial(_diff_attn_kernel, tq=tq)
    return pl.pallas_call(
        kern,
        grid=(batch, DIFF_HEADS, seq // tq),
        in_specs=[pl.BlockSpec(memory_space=pltpu.SMEM),
                  pl.BlockSpec(memory_space=pltpu.SMEM),
                  pl.BlockSpec((None, tq, HEAD_W), lambda b, h, i: (b, i, h)),
                  pl.BlockSpec((None, seq, HEAD_W), lambda b, h, i: (b, 0, DIFF_HEADS + h)),
                  pl.BlockSpec((None, seq, HEAD_W), lambda b, h, i: (b, 0, 2 * DIFF_HEADS + h)),
                  pl.BlockSpec((None, 2, 2 * tq, tq), lambda b, h, i: (h, 0, 0, 0)),
                  pl.BlockSpec((1, HEAD_W), lambda b, h, i: (0, 0))],
        out_specs=pl.BlockSpec((None, tq, HEAD_W), lambda b, h, i: (b, i, h)),
        out_shape=jax.ShapeDtypeStruct((batch, seq, DIFF_WIDTH), BF16),
        scratch_shapes=[pltpu.VMEM((seq // tq, 2 * tq, tq), F32),
                        pltpu.VMEM((2 * tq, LANES), F32),
                        pltpu.VMEM((2 * tq, LANES), F32),
                        pltpu.VMEM((2 * tq, HEAD_W), F32)],
        compiler_params=_cparams(("parallel", "parallel", "arbitrary")),
        name="diff_attn",
    )(lam, table, qkv, qkv, qkv, bias, g)


def _hgrn_kernel(hq_ref, hf_ref, hi_ref, hg_ref, lb_ref, g_ref, o_ref,
                 qd_ref, kd_ref, cd_ref, oi_ref, upd_ref, prev_ref, *, seq):
    c_len = HGRN_CHUNK
    n_chunks = seq // c_len
    lb = lb_ref[...]
    f = lb + (1.0 - lb) * jax.nn.sigmoid(hf_ref[...])
    log_f = jnp.log(f)
    kk = 1.0 - f
    hq = hq_ref[...]
    qq = hq * jax.nn.sigmoid(hq)
    vv = hi_ref[...]

    pos = lax.broadcasted_iota(jnp.int32, (seq, HEAD_W), 0) % c_len
    b = log_f
    sh = 1
    while sh < c_len:
        b = b + jnp.where(pos >= sh, pltpu.roll(b, sh, axis=0), 0.0)
        sh *= 2
    b3 = b.reshape(n_chunks, c_len, HEAD_W)
    b_last = b3[:, c_len - 1:c_len, :]
    q_dec = qq * jnp.exp(b)
    k_inv = kk * jnp.exp(-b)
    k_dec = kk * jnp.exp(b_last - b3).reshape(seq, HEAD_W)
    qd_ref[...] = q_dec.astype(BF16)
    kd_ref[...] = k_dec.astype(BF16)
    cd_ref[...] = jnp.exp(b_last.reshape(n_chunks, HEAD_W))

    blk = HEAD_W
    nb = seq // blk
    qd_b = q_dec.astype(BF16).reshape(nb, blk, HEAD_W)
    ki_b = k_inv.astype(BF16).reshape(nb, blk, HEAD_W)
    scores = jnp.einsum('nqd,nkd->nqk', qd_b, ki_b, preferred_element_type=F32)
    r = lax.broadcasted_iota(jnp.int32, (blk, blk), 0)
    c = lax.broadcasted_iota(jnp.int32, (blk, blk), 1)
    keep = (c <= r) & ((r // c_len) == (c // c_len))
    scores = jnp.where(keep[None], scores, 0.0)
    o_intra = jnp.einsum('nqk,nkv->nqv', scores.astype(BF16), vv.astype(BF16).reshape(nb, blk, HEAD_W),
                         preferred_element_type=F32).reshape(seq, HEAD_W)

    unroll = 4

    def rows_of(ci):
        return pl.ds(pl.multiple_of(ci * c_len, c_len), c_len)

    def updates(g, carry):
        for u in range(unroll):
            ci = g * unroll + u
            vc = hi_ref[rows_of(ci), :].astype(BF16)
            upd_ref[ci] = lax.dot_general(vc, kd_ref[rows_of(ci), :], (((0,), (0,)), ((), ())),
                                          preferred_element_type=F32)
        return carry

    lax.fori_loop(0, n_chunks // unroll, updates, 0)

    def recur(ci, st):
        prev_ref[ci] = st.astype(BF16)
        return st * cd_ref[pl.ds(ci, 1), :] + upd_ref[ci]

    lax.fori_loop(0, n_chunks, recur, jnp.zeros((HEAD_W, HEAD_W), F32))

    def reads(g, carry):
        for u in range(unroll):
            ci = g * unroll + u
            oi_ref[rows_of(ci), :] = lax.dot_general(qd_ref[rows_of(ci), :], prev_ref[ci],
                                                     (((1,), (1,)), ((), ())), preferred_element_type=F32)
        return carry

    lax.fori_loop(0, n_chunks // unroll, reads, 0)

    o = o_intra + oi_ref[...]
    hg = hg_ref[...]
    o_ref[...] = (_rms(o, g_ref[...]) * (hg * jax.nn.sigmoid(hg))).astype(o_ref.dtype)


def _hgrn(hin, lb, g, batch, seq):
    kern = functools.partial(_hgrn_kernel, seq=seq)
    col = lambda k: (lambda b, h: (b, 0, k * HGRN_HEADS + h))
    blk = (None, seq, HEAD_W)
    return pl.pallas_call(
        kern,
        grid=(batch, HGRN_HEADS),
        in_specs=[pl.BlockSpec(blk, col(0)), pl.BlockSpec(blk, col(1)),
                  pl.BlockSpec(blk, col(2)), pl.BlockSpec(blk, col(3)),
                  pl.BlockSpec((None, 1, HEAD_W), lambda b, h: (h, 0, 0)),
                  pl.BlockSpec((1, HEAD_W), lambda b, h: (0, 0))],
        out_specs=pl.BlockSpec(blk, lambda b, h: (b, 0, h)),
        out_shape=jax.ShapeDtypeStruct((batch, seq, HGRN_WIDTH), BF16),
        scratch_shapes=[pltpu.VMEM((seq, HEAD_W), BF16), pltpu.VMEM((seq, HEAD_W), BF16),
                        pltpu.VMEM((seq // HGRN_CHUNK, HEAD_W), F32), pltpu.VMEM((seq, HEAD_W), F32),
                        pltpu.VMEM((seq // HGRN_CHUNK, HEAD_W, HEAD_W), F32),
                        pltpu.VMEM((seq // HGRN_CHUNK, HEAD_W, HEAD_W), BF16)],
        compiler_params=_cparams(("parallel", "parallel")),
        name="hgrn",
    )(hin, hin, hin, hin, lb, g)


def _out_proj_kernel(x_ref, a_ref, o_ref, wa_ref, wo_ref, g_ref, h_ref, hnt_ref):
    h = (x_ref[...]
         + jnp.dot(a_ref[...], wa_ref[...], preferred_element_type=F32)
         + jnp.dot(o_ref[...], wo_ref[...], preferred_element_type=F32))
    h_ref[...] = h
    hnt_ref[...] = _rms(h, g_ref[...]).T.astype(BF16)


def _out_proj(x2, a2, o2, wa, wo, g, tb):
    t, d = x2.shape
    return pl.pallas_call(
        _out_proj_kernel,
        grid=(t // tb,),
        in_specs=[pl.BlockSpec((tb, d), lambda i: (i, 0)),
                  pl.BlockSpec((tb, DIFF_WIDTH), lambda i: (i, 0)),
                  pl.BlockSpec((tb, HGRN_WIDTH), lambda i: (i, 0)),
                  pl.BlockSpec((DIFF_WIDTH, d), lambda i: (0, 0)),
                  pl.BlockSpec((HGRN_WIDTH, d), lambda i: (0, 0)),
                  pl.BlockSpec((1, d), lambda i: (0, 0))],
        out_specs=[pl.BlockSpec((tb, d), lambda i: (i, 0)),
                   pl.BlockSpec((d, tb), lambda i: (0, i))],
        out_shape=[jax.ShapeDtypeStruct((t, d), F32),
                   jax.ShapeDtypeStruct((d, t), BF16)],
        compiler_params=_cparams(("parallel",)),
        name="out_proj",
    )(x2, a2, o2, wa, wo, g)


SUBLANES = 8
TAU_GUARD = 2.0 ** -21


def _oddeven_merge(lo, hi, r):
    step = r * 2
    if step < hi - lo:
        yield from _oddeven_merge(lo, hi, step)
        yield from _oddeven_merge(lo + r, hi, step)
        yield from [(i, i + r) for i in range(lo + r, hi - r, step)]
    else:
        yield (lo, lo + r)


def _oddeven_merge_sort(lo, hi):
    if hi - lo >= 1:
        mid = lo + (hi - lo) // 2
        yield from _oddeven_merge_sort(lo, mid)
        yield from _oddeven_merge_sort(mid + 1, hi)
        yield from _oddeven_merge(lo, hi, 1)


def _bitonic_merge(n):
    d = n // 2
    while d >= 1:
        yield from [(i, i + d) for i in range(n) if (i // d) % 2 == 0]
        d //= 2


SORT16 = tuple(_oddeven_merge_sort(0, PEER_TOPK - 1))
MERGE16 = tuple(_bitonic_merge(PEER_TOPK))


def _compare_exchange(v, pairs):
    for i, j in pairs:
        v[i], v[j] = jnp.maximum(v[i], v[j]), jnp.minimum(v[i], v[j])


def _top16_sorted(x):
    v = [x[SUBLANES * k:SUBLANES * (k + 1)] for k in range(PEER_TOPK)]
    _compare_exchange(v, SORT16)
    shift = SUBLANES // 2
    while shift >= 1:
        v = [jnp.maximum(v[k], pltpu.roll(v[PEER_TOPK - 1 - k], shift, axis=0)) for k in range(PEER_TOPK)]
        _compare_exchange(v, MERGE16)
        shift //= 2
    return v


def _rows_on_sublanes(rep, start):
    sub = lax.broadcasted_iota(jnp.int32, rep[0].shape, 0)
    out = rep[start]
    for s in range(1, SUBLANES):
        out = jnp.where(sub == s, rep[start + s], out)
    return out


def _pair_candidates(a_rep, b_rep):
    k = PEER_TOPK
    sub = lax.broadcasted_iota(jnp.int32, a_rep[0].shape, 0)
    b_lo = _rows_on_sublanes(b_rep, 0)
    b_hi = _rows_on_sublanes(b_rep, SUBLANES)
    a_hi = _rows_on_sublanes(a_rep, SUBLANES)
    pieces = [a_rep[0] + b_lo, a_rep[0] + b_hi, a_rep[1] + b_lo]
    for i in range(2, SUBLANES):
        pieces.append(jnp.where(sub < k // (i + 1), a_rep[i] + b_lo, -jnp.inf))
    pieces.append(a_hi + b_rep[0])
    return jnp.concatenate(pieces, axis=0)


def _kth_largest(x, k):
    for r in range(k):
        m = jnp.max(x, axis=0, keepdims=True)
        if r + 1 < k:
            x = jnp.where(x == m, -jnp.inf, x)
    return m


def _peer_route_kernel(hnt_ref, wqt_ref, keys_ref, tau_ref, alpha_ref, s2_ref, beta_ref, qt_ref, s_ref):
    qt_ref[...] = jnp.dot(wqt_ref[...], hnt_ref[...], preferred_element_type=F32).astype(BF16)
    half = PEER_QUERY_DIM // 2
    lane_tiles = hnt_ref.shape[1] // LANES
    k = PEER_TOPK

    def head(h, carry):
        base = pl.multiple_of(h * PEER_QUERY_DIM, PEER_QUERY_DIM)
        for c in range(2):
            s = jnp.dot(keys_ref[h, c], qt_ref[pl.ds(base + c * half, half), :], preferred_element_type=F32)
            for lt in range(lane_tiles):
                s_ref[c, lt] = s[:, lt * LANES:(lt + 1) * LANES]

        def tile(lt, carry):
            cols = pl.ds(pl.multiple_of(lt * LANES, LANES), LANES)
            s1 = s_ref[0, lt]
            s2 = s_ref[1, lt]
            a_rep = _top16_sorted(s1)
            b_rep = _top16_sorted(s2)
            cands = _pair_candidates(a_rep, b_rep)
            thr = _kth_largest(cands, k)
            top = a_rep[0][0:1] + b_rep[0][0:1]
            z = jnp.sum(jnp.where(cands >= thr, jnp.exp(cands - top), 0.0), axis=0, keepdims=True)
            tau_ref[h, :, cols] = (thr - s1) - (jnp.abs(thr) + jnp.abs(s1)) * TAU_GUARD
            alpha_ref[h, :, cols] = jnp.exp(s1 - a_rep[0][0:1]) * (0.5 / z)
            s2_ref[h, lt] = s2
            beta_ref[h, lt] = jnp.exp(s2 - b_rep[0][0:1])
            return carry

        return lax.fori_loop(0, lane_tiles, tile, carry)

    lax.fori_loop(0, PEER_HEADS, head, 0)


def _peer_route(hnt, wqt, keys, tb):
    d, t = hnt.shape
    nq = wqt.shape[0]
    oshape = jax.ShapeDtypeStruct((PEER_HEADS, PEER_N_KEYS, t), F32)
    ospec = pl.BlockSpec((PEER_HEADS, PEER_N_KEYS, tb), lambda i: (0, 0, i))
    tshape = jax.ShapeDtypeStruct((PEER_HEADS, t // LANES, PEER_N_KEYS, LANES), F32)
    tspec = pl.BlockSpec((PEER_HEADS, tb // LANES, PEER_N_KEYS, LANES), lambda i: (0, i, 0, 0))
    return pl.pallas_call(
        _peer_route_kernel,
        grid=(t // tb,),
        in_specs=[pl.BlockSpec((d, tb), lambda i: (0, i)),
                  pl.BlockSpec((nq, d), lambda i: (0, 0)),
                  pl.BlockSpec(keys.shape, lambda i: (0, 0, 0, 0))],
        out_specs=[ospec, ospec, tspec, tspec],
        out_shape=[oshape, oshape, tshape, tshape],
        scratch_shapes=[pltpu.VMEM((nq, tb), BF16), pltpu.VMEM((2, tb // LANES, PEER_N_KEYS, LANES), F32)],
        compiler_params=_cparams(("parallel",)),
        name="peer_route",
    )(hnt, wqt, keys)


def _peer_dense_kernel(hnt_ref, u_ref, vt_ref, tau_ref, alpha_ref, s2_ref, beta_ref, o_ref,
                       hid_ref, act_ref, *, te, tb, n_tiles):
    j = pl.program_id(1)
    rows_per_tile = te // PEER_N_KEYS
    lane_tiles = tb // LANES

    def hidden():
        slot = j % 2
        hid = jnp.dot(u_ref[...], hnt_ref[...], preferred_element_type=F32)
        for lt in range(lane_tiles):
            hid_ref[slot, lt] = hid[:, lt * LANES:(lt + 1) * LANES]

    def activations():
        slot = (j + 1) % 2
        for lt in range(lane_tiles):
            for r in range(rows_per_tile):
                w = None
                cols = slice(lt * LANES, (lt + 1) * LANES)
                for h in range(PEER_HEADS):
                    sel = s2_ref[h, lt] >= tau_ref[h, r:r + 1, cols]
                    term = jnp.where(sel, beta_ref[h, lt], 0.0) * alpha_ref[h, r:r + 1, cols]
                    w = term if w is None else w + term
                rows = slice(r * PEER_N_KEYS, (r + 1) * PEER_N_KEYS)
                hid_t = hid_ref[slot, lt, rows, :]
                gelu2 = hid_t * (1.0 + lax.erf(hid_t * (2.0 ** -0.5)))
                act_ref[slot, lt, rows, :] = (gelu2 * w).astype(BF16)

    def project():
        slot = j % 2
        act = jnp.concatenate([act_ref[slot, lt] for lt in range(lane_tiles)], axis=1)
        o_ref[...] += jnp.dot(vt_ref[...], act, preferred_element_type=F32)

    @pl.when(jnp.logical_and(j >= 2, j < n_tiles))
    def _():
        hidden()
        activations()
        project()

    @pl.when(j == 0)
    def _():
        o_ref[...] = jnp.zeros(o_ref.shape, o_ref.dtype)
        hidden()

    @pl.when(j == 1)
    def _():
        hidden()
        activations()

    @pl.when(j == n_tiles)
    def _():
        activations()
        project()

    @pl.when(j == n_tiles + 1)
    def _():
        project()


def _peer_dense(hnt, u_bf16, v_bf16, tau, alpha, s2, beta, tb, te):
    d, t = hnt.shape
    n_tiles = u_bf16.shape[0] // te
    assert n_tiles >= 2
    vt_tiles = v_bf16.reshape(n_tiles, te, d).transpose(0, 2, 1)
    kern = functools.partial(_peer_dense_kernel, te=te, tb=tb, n_tiles=n_tiles)
    last = n_tiles - 1
    rspec = pl.BlockSpec((PEER_HEADS, tb // LANES, PEER_N_KEYS, LANES), lambda i, j: (0, i, 0, 0))
    i1spec = pl.BlockSpec((PEER_HEADS, te // PEER_N_KEYS, tb), lambda i, j: (0, jnp.clip(j - 1, 0, last), i))
    stage_shape = (2, tb // LANES, te, LANES)
    return pl.pallas_call(
        kern,
        grid=(t // tb, n_tiles + 2),
        in_specs=[pl.BlockSpec((d, tb), lambda i, j: (0, i)),
                  pl.BlockSpec((te, d), lambda i, j: (jnp.minimum(j, last), 0)),
                  pl.BlockSpec((None, d, te), lambda i, j: (jnp.clip(j - 2, 0, last), 0, 0)),
                  i1spec, i1spec, rspec, rspec],
        out_specs=pl.BlockSpec((d, tb), lambda i, j: (0, i)),
        out_shape=jax.ShapeDtypeStruct((d, t), F32),
        scratch_shapes=[pltpu.VMEM(stage_shape, F32), pltpu.VMEM(stage_shape, BF16)],
        compiler_params=_cparams(("parallel", "arbitrary")),
        name="peer_dense",
    )(hnt, u_bf16, vt_tiles, tau, alpha, s2, beta)


def _final_norm_kernel(h_ref, pt_ref, g_ref, o_ref):
    o_ref[...] = _rms(h_ref[...] + pt_ref[...].T, g_ref[...])


def _final_norm(h2, peer_t, g, tb):
    t, d = h2.shape
    return pl.pallas_call(
        _final_norm_kernel,
        grid=(t // tb,),
        in_specs=[pl.BlockSpec((tb, d), lambda i: (i, 0)),
                  pl.BlockSpec((d, tb), lambda i: (0, i)),
                  pl.BlockSpec((1, d), lambda i: (0, 0))],
        out_specs=pl.BlockSpec((tb, d), lambda i: (i, 0)),
        out_shape=jax.ShapeDtypeStruct((t, d), F32),
        compiler_params=_cparams(("parallel",)),
        name="final_norm",
    )(h2, peer_t, g)


def _largest_divisor(n, cap, multiple):
    best = multiple
    k = multiple
    while k <= min(n, cap):
        if n % k == 0:
            best = k
        k += multiple
    return best


def kernel(x, norm1_g, w_in, diff_lambda_q1, diff_lambda_k1, diff_lambda_q2, diff_lambda_k2, diff_subln_g,
           rel_bias_table, hgrn_lb_logits, hgrn_gnorm_g, w_out, norm2_g, peer_w_q, peer_sub_keys, peer_u,
           peer_v, final_norm_g):
    batch, seq, d = x.shape
    t = batch * seq
    assert d == D_MODEL and seq % 256 == 0 and w_in.shape[0] == 1
    tb_proj = _largest_divisor(t, 512, 256)
    tq = 256
    tb_route = _largest_divisor(t, 512, 256)
    tb_peer = _largest_divisor(t, 1024, 512)
    te_peer = 1024

    x2 = x.reshape(t, d)
    row = lambda v: v.reshape(1, -1).astype(F32)

    lam = (jnp.exp(jnp.sum(diff_lambda_q1[0].astype(F32) * diff_lambda_k1[0].astype(F32)))
           - jnp.exp(jnp.sum(diff_lambda_q2[0].astype(F32) * diff_lambda_k2[0].astype(F32)))
           + LAM_INIT).reshape(1)
    lb = jnp.cumsum(jax.nn.softmax(hgrn_lb_logits.astype(F32), axis=0), axis=0)[0]
    lb = lb.reshape(HGRN_HEADS, 1, HEAD_W)
    table = rel_bias_table.astype(F32)

    qkv, hin = _norm_proj(x2, row(norm1_g[0]), w_in[0].astype(BF16), tb_proj)
    bias = _bias_tiles(table, tq)
    a = _diff_attn(lam, table, qkv.reshape(batch, seq, ATTN_COLS), bias, row(diff_subln_g[0]), batch, seq, tq)
    o = _hgrn(hin.reshape(batch, seq, HGRN_COLS), lb, row(hgrn_gnorm_g[0]), batch, seq)
    w_o = w_out[0].astype(BF16)
    h2, hnt = _out_proj(x2, a.reshape(t, DIFF_WIDTH), o.reshape(t, HGRN_WIDTH),
                        w_o[:DIFF_WIDTH], w_o[DIFF_WIDTH:], row(norm2_g[0]), tb_proj)
    tau, alpha, s2, beta = _peer_route(hnt, peer_w_q[0].T.astype(BF16), peer_sub_keys[0].astype(BF16), tb_route)
    peer_t = _peer_dense(hnt, peer_u[0].astype(BF16), peer_v[0].astype(BF16), tau, alpha, s2, beta,
                         tb_peer, te_peer)
    out = _final_norm(h2, peer_t, row(final_norm_g), tb_proj)
    return out.reshape(batch, seq, d)
```

```python
import functools
import math

import numpy as np
import jax
import jax.numpy as jnp
from jax import lax
from jax.experimental import pallas as pl
from jax.experimental.pallas import tpu as pltpu

F32 = jnp.float32
BF16 = jnp.bfloat16

D_MODEL = 1024
DIFF_HEADS = 4
DIFF_QK_DIM = 64
LANES = 128
MXU_WIDTH = 256
SECTION_ROWS = MXU_WIDTH
HEAD_W = 128
DIFF_WIDTH = DIFF_HEADS * HEAD_W
HGRN_HEADS = 4
HGRN_WIDTH = HGRN_HEADS * HEAD_W
HGRN_CHUNK = 32
ATTN_COLS = 3 * DIFF_WIDTH
HGRN_COLS = 4 * HGRN_WIDTH
REL_BUCKETS = 32
REL_MAX_DIST = 128
PEER_HEADS = 8
PEER_N_KEYS = 128
PEER_N_EXPERTS = PEER_N_KEYS * PEER_N_KEYS
PEER_QUERY_DIM = 256
PEER_TOPK = 16
NORM_EPS = 1e-6
LAM_INIT = 0.8 - 0.6 * math.exp(-0.3 * 0)
NEG_BIG = -1e30

VMEM_LIMIT_BYTES = 56 * 1024 * 1024


def _cparams(sem):
    return pltpu.CompilerParams(dimension_semantics=sem, vmem_limit_bytes=VMEM_LIMIT_BYTES)


def _rms(x, g):
    return x * lax.rsqrt(jnp.mean(x * x, axis=-1, keepdims=True) + NORM_EPS) * g


def _norm_proj_kernel(x_ref, g_ref, w_ref, qt_ref, k_ref, vt_ref, oh_ref):
    y = _rms(x_ref[...], g_ref[...])
    p = jnp.dot(y.astype(BF16), w_ref[...], preferred_element_type=F32)
    qt_ref[...] = p[:, :DIFF_WIDTH].T.astype(BF16)
    k_ref[...] = p[:, DIFF_WIDTH:2 * DIFF_WIDTH].astype(BF16)
    vt_ref[...] = p[:, 2 * DIFF_WIDTH:ATTN_COLS].T.astype(BF16)
    oh_ref[...] = p[:, ATTN_COLS:]


def _norm_proj(x2, g, w_bf16, tb):
    t, d = x2.shape
    n = w_bf16.shape[1]
    return pl.pallas_call(
        _norm_proj_kernel,
        grid=(t // tb,),
        in_specs=[pl.BlockSpec((tb, d), lambda i: (i, 0)),
                  pl.BlockSpec((1, d), lambda i: (0, 0)),
                  pl.BlockSpec((d, n), lambda i: (0, 0))],
        out_specs=[pl.BlockSpec((DIFF_WIDTH, tb), lambda i: (0, i)),
                   pl.BlockSpec((tb, DIFF_WIDTH), lambda i: (i, 0)),
                   pl.BlockSpec((DIFF_WIDTH, tb), lambda i: (0, i)),
                   pl.BlockSpec((tb, HGRN_COLS), lambda i: (i, 0))],
        out_shape=[jax.ShapeDtypeStruct((DIFF_WIDTH, t), BF16),
                   jax.ShapeDtypeStruct((t, DIFF_WIDTH), BF16),
                   jax.ShapeDtypeStruct((DIFF_WIDTH, t), BF16),
                   jax.ShapeDtypeStruct((t, HGRN_COLS), F32)],
        compiler_params=_cparams(("parallel",)),
        name="norm_proj",
    )(x2, g, w_bf16)


def _t5_bucket_np(n):
    n = np.maximum(n, 0).astype(np.int32)
    max_exact = REL_BUCKETS // 2
    nf = np.maximum(n, max_exact).astype(np.float32)
    large = max_exact + (np.log(nf / np.float32(max_exact)) / np.float32(math.log(REL_MAX_DIST / max_exact))
                         * np.float32(REL_BUCKETS - max_exact)).astype(np.int32)
    large = np.minimum(large, REL_BUCKETS - 1)
    return np.where(n < max_exact, n, large).astype(np.int32)


def _bias_tiles_kernel(table_ref, bucket_ref, o_ref):
    h = pl.program_id(0)
    bucket = bucket_ref[...]
    acc = jnp.zeros(bucket.shape, F32)
    for b in range(REL_BUCKETS):
        acc = jnp.where(bucket == b, table_ref[b, h], acc)
    o_ref[...] = acc


def _bias_tiles(table, tq):
    c = np.arange(tq)[:, None]
    r = np.arange(tq)[None, :]
    tiles = np.stack([_t5_bucket_np(d * tq + r - c) for d in (0, 1)])
    tiles = np.concatenate([tiles, tiles], axis=2)
    return pl.pallas_call(
        _bias_tiles_kernel,
        grid=(DIFF_HEADS,),
        in_specs=[pl.BlockSpec(memory_space=pltpu.SMEM),
                  pl.BlockSpec((2, tq, 2 * tq), lambda h: (0, 0, 0))],
        out_specs=pl.BlockSpec((None, 2, tq, 2 * tq), lambda h: (h, 0, 0, 0)),
        out_shape=jax.ShapeDtypeStruct((DIFF_HEADS, 2, tq, 2 * tq), F32),
        compiler_params=_cparams(("arbitrary",)),
        name="bias_tiles",
    )(table, jnp.asarray(tiles))


def _diff_attn_kernel(lam_ref, table_ref, qt_ref, k_ref, vt_ref, bias_ref, g_ref, o_ref,
                      s_ref, mx_ref, l_ref, acc_ref, *, tq):
    h = pl.program_id(1)
    i = pl.program_id(2)
    qt = qt_ref[...] * jnp.asarray(DIFF_QK_DIM ** -0.5, BF16)
    row = lax.broadcasted_iota(jnp.int32, qt.shape, 0)
    zero = jnp.zeros_like(qt)
    q2t = jnp.concatenate([jnp.where(row < DIFF_QK_DIM, qt, zero),
                           jnp.where(row >= DIFF_QK_DIM, qt, zero)], axis=1)
    far_bias = table_ref[REL_BUCKETS - 1, h]

    def fold(op, x):
        return functools.reduce(op, [x[SUBLANES * k:SUBLANES * (k + 1)] for k in range(x.shape[0] // SUBLANES)])

    def scores(j, n_keys, bias, causal):
        start = pl.multiple_of(j * tq, tq)
        s = jnp.dot(k_ref[pl.ds(start, n_keys), :], q2t, preferred_element_type=F32) + bias
        if causal:
            c = lax.broadcasted_iota(jnp.int32, s.shape, 0)
            r = lax.broadcasted_iota(jnp.int32, s.shape, 1) % tq
            s = jnp.where(c <= r, s, NEG_BIG)
        for k in range(n_keys // tq):
            s_ref[j + k] = s[k * tq:(k + 1) * tq]
        mx_ref[...] = jnp.maximum(mx_ref[...], fold(jnp.maximum, s))

    mx_ref[...] = jnp.full(mx_ref.shape, NEG_BIG, F32)
    l_ref[...] = jnp.zeros(l_ref.shape, F32)
    acc_ref[...] = jnp.zeros(acc_ref.shape, F32)

    def far_pair(p, carry):
        scores(2 * p, 2 * tq, far_bias, False)
        return carry

    n_far = jnp.maximum(i - 1, 0)
    lax.fori_loop(0, n_far // 2, far_pair, 0)

    @pl.when(n_far % 2 == 1)
    def _():
        scores(n_far - 1, tq, far_bias, False)

    @pl.when(i >= 1)
    def _():
        scores(i - 1, tq, bias_ref[1], False)

    scores(i, tq, bias_ref[0], True)
    m = jnp.max(mx_ref[...], axis=0, keepdims=True)

    def weigh(j, n_keys):
        start = pl.multiple_of(j * tq, tq)
        s = jnp.concatenate([s_ref[j + k] for k in range(n_keys // tq)], axis=0)
        e = jnp.exp(s - m)
        l_ref[...] += fold(jnp.add, e)
        acc_ref[...] += jnp.dot(vt_ref[:, pl.ds(start, n_keys)], e.astype(BF16), preferred_element_type=F32)

    def weigh_pair(p, carry):
        weigh(2 * p, 2 * tq)
        return carry

    n_blocks = i + 1
    lax.fori_loop(0, n_blocks // 2, weigh_pair, 0)

    @pl.when(n_blocks % 2 == 1)
    def _():
        weigh(i, tq)

    l = jnp.sum(l_ref[...], axis=0, keepdims=True)
    acc = acc_ref[...]
    lam = lam_ref[0]
    out_t = acc[:, :tq] / l[:, :tq] - lam * (acc[:, tq:] / l[:, tq:])
    o_ref[...] = (_rms(out_t.T, g_ref[...]) * (1.0 - LAM_INIT)).astype(o_ref.dtype)


def _diff_attn(lam, table, q_t, k, v_t, bias, g, batch, seq, tq):
    kern = functools.partial(_diff_attn_kernel, tq=tq)
    nq = seq // tq
    return pl.pallas_call(
        kern,
        grid=(batch, DIFF_HEADS, nq),
        in_specs=[pl.BlockSpec(memory_space=pltpu.SMEM),
                  pl.BlockSpec(memory_space=pltpu.SMEM),
                  pl.BlockSpec((HEAD_W, tq), lambda b, h, i: (h, b * nq + i)),
                  pl.BlockSpec((seq, HEAD_W), lambda b, h, i: (b, h)),
                  pl.BlockSpec((HEAD_W, seq), lambda b, h, i: (h, b)),
                  pl.BlockSpec((None, 2, tq, 2 * tq), lambda b, h, i: (h, 0, 0, 0)),
                  pl.BlockSpec((1, HEAD_W), lambda b, h, i: (0, 0))],
        out_specs=pl.BlockSpec((None, tq, HEAD_W), lambda b, h, i: (b, i, h)),
        out_shape=jax.ShapeDtypeStruct((batch, seq, DIFF_WIDTH), BF16),
        scratch_shapes=[pltpu.VMEM((seq // tq, tq, 2 * tq), F32),
                        pltpu.VMEM((SUBLANES, 2 * tq), F32),
                        pltpu.VMEM((SUBLANES, 2 * tq), F32),
                        pltpu.VMEM((HEAD_W, 2 * tq), F32)],
        compiler_params=_cparams(("parallel", "parallel", "arbitrary")),
        name="diff_attn",
    )(lam, table, q_t, k, v_t, bias, g)


def _hgrn_kernel(hq_ref, hf_ref, hi_ref, hg_ref, lb_ref, g_ref, o_ref,
                 qd_ref, kd_ref, cd_ref, oi_ref, upd_ref, prev_ref, *, seq):
    c_len = HGRN_CHUNK
    n_chunks = seq // c_len
    lb = lb_ref[...]
    f = lb + (1.0 - lb) * jax.nn.sigmoid(hf_ref[...])
    log_f = jnp.log(f)
    kk = 1.0 - f
    hq = hq_ref[...]
    qq = hq * jax.nn.sigmoid(hq)
    vv = hi_ref[...]

    pos = lax.broadcasted_iota(jnp.int32, (seq, HEAD_W), 0) % c_len
    b = log_f
    sh = 1
    while sh < c_len:
        b = b + jnp.where(pos >= sh, pltpu.roll(b, sh, axis=0), 0.0)
        sh *= 2
    b3 = b.reshape(n_chunks, c_len, HEAD_W)
    b_last = b3[:, c_len - 1:c_len, :]
    q_dec = qq * jnp.exp(b)
    k_inv = kk * jnp.exp(-b)
    k_dec = kk * jnp.exp(b_last - b3).reshape(seq, HEAD_W)
    qd_ref[...] = q_dec.astype(BF16)
    kd_ref[...] = k_dec.astype(BF16)
    cd_ref[...] = jnp.exp(b_last.reshape(n_chunks, HEAD_W))

    blk = HEAD_W
    nb = seq // blk
    qd_b = q_dec.astype(BF16).reshape(nb, blk, HEAD_W)
    ki_b = k_inv.astype(BF16).reshape(nb, blk, HEAD_W)
    scores = jnp.einsum('nqd,nkd->nqk', qd_b, ki_b, preferred_element_type=F32)
    r = lax.broadcasted_iota(jnp.int32, (blk, blk), 0)
    c = lax.broadcasted_iota(jnp.int32, (blk, blk), 1)
    keep = (c <= r) & ((r // c_len) == (c // c_len))
    scores = jnp.where(keep[None], scores, 0.0)
    o_intra = jnp.einsum('nqk,nkv->nqv', scores.astype(BF16), vv.astype(BF16).reshape(nb, blk, HEAD_W),
                         preferred_element_type=F32).reshape(seq, HEAD_W)

    unroll = 8

    def rows_of(ci):
        return pl.ds(pl.multiple_of(ci * c_len, c_len), c_len)

    def updates(g, carry):
        for u in range(unroll):
            ci = g * unroll + u
            vc = hi_ref[rows_of(ci), :].astype(BF16)
            upd_ref[ci] = lax.dot_general(vc, kd_ref[rows_of(ci), :], (((0,), (0,)), ((), ())),
                                          preferred_element_type=F32)
        return carry

    lax.fori_loop(0, n_chunks // unroll, updates, 0)

    def recur(ci, st):
        prev_ref[ci] = st.astype(BF16)
        return st * cd_ref[pl.ds(ci, 1), :] + upd_ref[ci]

    lax.fori_loop(0, n_chunks, recur, jnp.zeros((HEAD_W, HEAD_W), F32))

    def reads(g, carry):
        for u in range(unroll):
            ci = g * unroll + u
            oi_ref[rows_of(ci), :] = lax.dot_general(qd_ref[rows_of(ci), :], prev_ref[ci],
                                                     (((1,), (1,)), ((), ())), preferred_element_type=F32)
        return carry

    lax.fori_loop(0, n_chunks // unroll, reads, 0)

    o = o_intra + oi_ref[...]
    hg = hg_ref[...]
    o_ref[...] = (_rms(o, g_ref[...]) * (hg * jax.nn.sigmoid(hg))).astype(o_ref.dtype)


def _hgrn(hin, lb, g, batch, seq):
    kern = functools.partial(_hgrn_kernel, seq=seq)
    col = lambda k: (lambda b, h: (b, 0, k * HGRN_HEADS + h))
    blk = (None, seq, HEAD_W)
    return pl.pallas_call(
        kern,
        grid=(batch, HGRN_HEADS),
        in_specs=[pl.BlockSpec(blk, col(0)), pl.BlockSpec(blk, col(1)),
                  pl.BlockSpec(blk, col(2)), pl.BlockSpec(blk, col(3)),
                  pl.BlockSpec((None, 1, HEAD_W), lambda b, h: (h, 0, 0)),
                  pl.BlockSpec((1, HEAD_W), lambda b, h: (0, 0))],
        out_specs=pl.BlockSpec(blk, lambda b, h: (b, 0, h)),
        out_shape=jax.ShapeDtypeStruct((batch, seq, HGRN_WIDTH), BF16),
        scratch_shapes=[pltpu.VMEM((seq, HEAD_W), BF16), pltpu.VMEM((seq, HEAD_W), BF16),
                        pltpu.VMEM((seq // HGRN_CHUNK, HEAD_W), F32), pltpu.VMEM((seq, HEAD_W), F32),
                        pltpu.VMEM((seq // HGRN_CHUNK, HEAD_W, HEAD_W), F32),
                        pltpu.VMEM((seq // HGRN_CHUNK, HEAD_W, HEAD_W), BF16)],
        compiler_params=_cparams(("parallel", "parallel")),
        name="hgrn",
    )(hin, hin, hin, hin, lb, g)


def _out_proj_kernel(x_ref, a_ref, o_ref, wa_ref, wo_ref, g_ref, h_ref, hnt_ref):
    h = (x_ref[...]
         + jnp.dot(a_ref[...], wa_ref[...], preferred_element_type=F32)
         + jnp.dot(o_ref[...], wo_ref[...], preferred_element_type=F32))
    h_ref[...] = h
    hnt_ref[...] = _rms(h, g_ref[...]).T.astype(BF16)


def _out_proj(x2, a2, o2, wa, wo, g, tb):
    t, d = x2.shape
    return pl.pallas_call(
        _out_proj_kernel,
        grid=(t // tb,),
        in_specs=[pl.BlockSpec((tb, d), lambda i: (i, 0)),
                  pl.BlockSpec((tb, DIFF_WIDTH), lambda i: (i, 0)),
                  pl.BlockSpec((tb, HGRN_WIDTH), lambda i: (i, 0)),
                  pl.BlockSpec((DIFF_WIDTH, d), lambda i: (0, 0)),
                  pl.BlockSpec((HGRN_WIDTH, d), lambda i: (0, 0)),
                  pl.BlockSpec((1, d), lambda i: (0, 0))],
        out_specs=[pl.BlockSpec((tb, d), lambda i: (i, 0)),
                   pl.BlockSpec((d, tb), lambda i: (0, i))],
        out_shape=[jax.ShapeDtypeStruct((t, d), F32),
                   jax.ShapeDtypeStruct((d, t), BF16)],
        compiler_params=_cparams(("parallel",)),
        name="out_proj",
    )(x2, a2, o2, wa, wo, g)


SUBLANES = 8
TAU_GUARD = 2.0 ** -21


def _oddeven_merge(lo, hi, r):
    step = r * 2
    if step < hi - lo:
        yield from _oddeven_merge(lo, hi, step)
        yield from _oddeven_merge(lo + r, hi, step)
        yield from [(i, i + r) for i in range(lo + r, hi - r, step)]
    else:
        yield (lo, lo + r)


def _oddeven_merge_sort(lo, hi):
    if hi - lo >= 1:
        mid = lo + (hi - lo) // 2
        yield from _oddeven_merge_sort(lo, mid)
        yield from _oddeven_merge_sort(mid + 1, hi)
        yield from _oddeven_merge(lo, hi, 1)


def _bitonic_merge(n):
    d = n // 2
    while d >= 1:
        yield from [(i, i + d) for i in range(n) if (i // d) % 2 == 0]
        d //= 2


SORT16 = tuple(_oddeven_merge_sort(0, PEER_TOPK - 1))
MERGE16 = tuple(_bitonic_merge(PEER_TOPK))


def _compare_exchange(v, pairs):
    for i, j in pairs:
        v[i], v[j] = jnp.maximum(v[i], v[j]), jnp.minimum(v[i], v[j])


def _top16_sorted(x):
    v = [x[SUBLANES * k:SUBLANES * (k + 1)] for k in range(PEER_TOPK)]
    _compare_exchange(v, SORT16)
    shift = SUBLANES // 2
    while shift >= 1:
        v = [jnp.maximum(v[k], pltpu.roll(v[PEER_TOPK - 1 - k], shift, axis=0)) for k in range(PEER_TOPK)]
        _compare_exchange(v, MERGE16)
        shift //= 2
    return v


def _rows_on_sublanes(rep, start):
    sub = lax.broadcasted_iota(jnp.int32, rep[0].shape, 0)
    out = rep[start]
    for s in range(1, SUBLANES):
        out = jnp.where(sub == s, rep[start + s], out)
    return out


def _pair_candidates(a_rep, b_rep):
    k = PEER_TOPK
    sub = lax.broadcasted_iota(jnp.int32, a_rep[0].shape, 0)
    b_lo = _rows_on_sublanes(b_rep, 0)
    b_hi = _rows_on_sublanes(b_rep, SUBLANES)
    a_hi = _rows_on_sublanes(a_rep, SUBLANES)
    pieces = [a_rep[0] + b_lo, a_rep[0] + b_hi, a_rep[1] + b_lo]
    for i in range(2, SUBLANES):
        pieces.append(jnp.where(sub < k // (i + 1), a_rep[i] + b_lo, -jnp.inf))
    pieces.append(a_hi + b_rep[0])
    return jnp.concatenate(pieces, axis=0)


def _kth_largest(x, k):
    for r in range(k):
        m = jnp.max(x, axis=0, keepdims=True)
        if r + 1 < k:
            x = jnp.where(x == m, -jnp.inf, x)
    return m


def _peer_route_kernel(hnt_ref, wqt_ref, keys_ref, tau_ref, alpha_ref, s2_ref, beta_ref, qt_ref, s_ref):
    qt_ref[...] = jnp.dot(wqt_ref[...], hnt_ref[...], preferred_element_type=F32).astype(BF16)
    half = PEER_QUERY_DIM // 2
    lane_tiles = hnt_ref.shape[1] // LANES
    k = PEER_TOPK

    def head(h, carry):
        base = pl.multiple_of(h * PEER_QUERY_DIM, PEER_QUERY_DIM)
        for c in range(2):
            s = jnp.dot(keys_ref[h, c], qt_ref[pl.ds(base + c * half, half), :], preferred_element_type=F32)
            for lt in range(lane_tiles):
                s_ref[c, lt] = s[:, lt * LANES:(lt + 1) * LANES]

        def tile(lt, carry):
            cols = pl.ds(pl.multiple_of(lt * LANES, LANES), LANES)
            s1 = s_ref[0, lt]
            s2 = s_ref[1, lt]
            a_rep = _top16_sorted(s1)
            b_rep = _top16_sorted(s2)
            cands = _pair_candidates(a_rep, b_rep)
            thr = _kth_largest(cands, k)
            top = a_rep[0][0:1] + b_rep[0][0:1]
            z = jnp.sum(jnp.where(cands >= thr, jnp.exp(cands - top), 0.0), axis=0, keepdims=True)
            tau_ref[h, :, cols] = (thr - s1) - (jnp.abs(thr) + jnp.abs(s1)) * TAU_GUARD
            alpha_ref[h, :, cols] = jnp.exp(s1 - a_rep[0][0:1]) * (0.5 / z)
            s2_ref[h, lt] = s2
            beta_ref[h, lt] = jnp.exp(s2 - b_rep[0][0:1])
            return carry

        return lax.fori_loop(0, lane_tiles, tile, carry)

    lax.fori_loop(0, PEER_HEADS, head, 0)


def _peer_route(hnt, wqt, keys, tb):
    d, t = hnt.shape
    nq = wqt.shape[0]
    oshape = jax.ShapeDtypeStruct((PEER_HEADS, PEER_N_KEYS, t), F32)
    ospec = pl.BlockSpec((PEER_HEADS, PEER_N_KEYS, tb), lambda i: (0, 0, i))
    tshape = jax.ShapeDtypeStruct((PEER_HEADS, t // LANES, PEER_N_KEYS, LANES), F32)
    tspec = pl.BlockSpec((PEER_HEADS, tb // LANES, PEER_N_KEYS, LANES), lambda i: (0, i, 0, 0))
    return pl.pallas_call(
        _peer_route_kernel,
        grid=(t // tb,),
        in_specs=[pl.BlockSpec((d, tb), lambda i: (0, i)),
                  pl.BlockSpec((nq, d), lambda i: (0, 0)),
                  pl.BlockSpec(keys.shape, lambda i: (0, 0, 0, 0))],
        out_specs=[ospec, ospec, tspec, tspec],
        out_shape=[oshape, oshape, tshape, tshape],
        scratch_shapes=[pltpu.VMEM((nq, tb), BF16), pltpu.VMEM((2, tb // LANES, PEER_N_KEYS, LANES), F32)],
        compiler_params=_cparams(("parallel",)),
        name="peer_route",
    )(hnt, wqt, keys)


def _peer_dense_kernel(hnt_ref, u_ref, vt_ref, tau_ref, alpha_ref, s2_ref, beta_ref, o_ref,
                       hid_ref, act_ref, *, te, tb, n_tiles):
    j = pl.program_id(1)
    cur = j % 2
    prev = 1 - cur
    rows_per_tile = te // PEER_N_KEYS
    lane_tiles = tb // LANES

    def hidden(hid_ref):
        hid = jnp.dot(u_ref[...], hnt_ref[...], preferred_element_type=F32)
        for lt in range(lane_tiles):
            hid_ref[lt] = hid[:, lt * LANES:(lt + 1) * LANES]

    def activations(hid_ref, act_ref):
        for lt in range(lane_tiles):
            cols = slice(lt * LANES, (lt + 1) * LANES)
            for r in range(rows_per_tile):
                w = None
                for h in range(PEER_HEADS):
                    sel = s2_ref[h, lt] >= tau_ref[h, r:r + 1, cols]
                    term = jnp.where(sel, beta_ref[h, lt], 0.0) * alpha_ref[h, r:r + 1, cols]
                    w = term if w is None else w + term
                rows = slice(r * PEER_N_KEYS, (r + 1) * PEER_N_KEYS)
                hid_t = hid_ref[lt, rows, :]
                gelu2 = hid_t * (1.0 + lax.erf(hid_t * (2.0 ** -0.5)))
                act_ref[lt, rows, :] = (gelu2 * w).astype(BF16)

    def project(act_ref):
        act = jnp.concatenate([act_ref[lt] for lt in range(lane_tiles)], axis=1)
        o_ref[...] += jnp.dot(vt_ref[...], act, preferred_element_type=F32)

    def run(do_a, do_b, do_c):
        if do_a:
            hidden(hid_ref.at[cur])
        if do_b:
            activations(hid_ref.at[prev], act_ref.at[prev])
        if do_c:
            project(act_ref.at[cur])

    @pl.when(jnp.logical_and(j >= 2, j < n_tiles))
    def _():
        run(True, True, True)

    @pl.when(j == 0)
    def _():
        o_ref[...] = jnp.zeros(o_ref.shape, o_ref.dtype)
        run(True, False, False)

    @pl.when(j == 1)
    def _():
        run(True, True, False)

    @pl.when(j == n_tiles)
    def _():
        run(False, True, True)

    @pl.when(j == n_tiles + 1)
    def _():
        run(False, False, True)


def _peer_dense(hnt, u_bf16, v_bf16, tau, alpha, s2, beta, tb, te):
    d, t = hnt.shape
    n_tiles = u_bf16.shape[0] // te
    assert n_tiles >= 2
    vt_tiles = v_bf16.reshape(n_tiles, te, d).transpose(0, 2, 1)
    kern = functools.partial(_peer_dense_kernel, te=te, tb=tb, n_tiles=n_tiles)
    last = n_tiles - 1
    rspec = pl.BlockSpec((PEER_HEADS, tb // LANES, PEER_N_KEYS, LANES), lambda i, j: (0, i, 0, 0))
    i1spec = pl.BlockSpec((PEER_HEADS, te // PEER_N_KEYS, tb), lambda i, j: (0, jnp.clip(j - 1, 0, last), i))
    stage_shape = (tb // LANES, te, LANES)
    return pl.pallas_call(
        kern,
        grid=(t // tb, n_tiles + 2),
        in_specs=[pl.BlockSpec((d, tb), lambda i, j: (0, i)),
                  pl.BlockSpec((te, d), lambda i, j: (jnp.minimum(j, last), 0)),
                  pl.BlockSpec((None, d, te), lambda i, j: (jnp.clip(j - 2, 0, last), 0, 0)),
                  i1spec, i1spec, rspec, rspec],
        out_specs=pl.BlockSpec((d, tb), lambda i, j: (0, i)),
        out_shape=jax.ShapeDtypeStruct((d, t), F32),
        scratch_shapes=[pltpu.VMEM((2,) + stage_shape, F32), pltpu.VMEM((2,) + stage_shape, BF16)],
        compiler_params=_cparams(("parallel", "arbitrary")),
        name="peer_dense",
    )(hnt, u_bf16, vt_tiles, tau, alpha, s2, beta)


def _final_norm_kernel(h_ref, pt_ref, g_ref, o_ref):
    o_ref[...] = _rms(h_ref[...] + pt_ref[...].T, g_ref[...])


def _final_norm(h2, peer_t, g, tb):
    t, d = h2.shape
    return pl.pallas_call(
        _final_norm_kernel,
        grid=(t // tb,),
        in_specs=[pl.BlockSpec((tb, d), lambda i: (i, 0)),
                  pl.BlockSpec((d, tb), lambda i: (0, i)),
                  pl.BlockSpec((1, d), lambda i: (0, 0))],
        out_specs=pl.BlockSpec((tb, d), lambda i: (i, 0)),
        out_shape=jax.ShapeDtypeStruct((t, d), F32),
        compiler_params=_cparams(("parallel",)),
        name="final_norm",
    )(h2, peer_t, g)


def _largest_divisor(n, cap, multiple):
    best = multiple
    k = multiple
    while k <= min(n, cap):
        if n % k == 0:
            best = k
        k += multiple
    return best


def kernel(x, norm1_g, w_in, diff_lambda_q1, diff_lambda_k1, diff_lambda_q2, diff_lambda_k2, diff_subln_g,
           rel_bias_table, hgrn_lb_logits, hgrn_gnorm_g, w_out, norm2_g, peer_w_q, peer_sub_keys, peer_u,
           peer_v, final_norm_g):
    batch, seq, d = x.shape
    t = batch * seq
    assert d == D_MODEL and seq % 256 == 0 and w_in.shape[0] == 1
    tb_proj = _largest_divisor(t, 512, 256)
    tq = 256
    tb_route = _largest_divisor(t, 512, 256)
    tb_peer = _largest_divisor(t, 1024, 512)
    te_peer = 1024

    x2 = x.reshape(t, d)
    row = lambda v: v.reshape(1, -1).astype(F32)

    lam = (jnp.exp(jnp.sum(diff_lambda_q1[0].astype(F32) * diff_lambda_k1[0].astype(F32)))
           - jnp.exp(jnp.sum(diff_lambda_q2[0].astype(F32) * diff_lambda_k2[0].astype(F32)))
           + LAM_INIT).reshape(1)
    lb = jnp.cumsum(jax.nn.softmax(hgrn_lb_logits.astype(F32), axis=0), axis=0)[0]
    lb = lb.reshape(HGRN_HEADS, 1, HEAD_W)
    table = rel_bias_table.astype(F32)

    q_t, k, v_t, hin = _norm_proj(x2, row(norm1_g[0]), w_in[0].astype(BF16), tb_proj)
    bias = _bias_tiles(table, tq)
    a = _diff_attn(lam, table, q_t, k, v_t, bias, row(diff_subln_g[0]), batch, seq, tq)
    o = _hgrn(hin.reshape(batch, seq, HGRN_COLS), lb, row(hgrn_gnorm_g[0]), batch, seq)
    w_o = w_out[0].astype(BF16)
    h2, hnt = _out_proj(x2, a.reshape(t, DIFF_WIDTH), o.reshape(t, HGRN_WIDTH),
                        w_o[:DIFF_WIDTH], w_o[DIFF_WIDTH:], row(norm2_g[0]), tb_proj)
    tau, alpha, s2, beta = _peer_route(hnt, peer_w_q[0].T.astype(BF16), peer_sub_keys[0].astype(BF16), tb_route)
    peer_t = _peer_dense(hnt, peer_u[0].astype(BF16), peer_v[0].astype(BF16), tau, alpha, s2, beta,
                         tb_peer, te_peer)
    out = _final_norm(h2, peer_t, row(final_norm_g), tb_proj)
    return out.reshape(batch, seq, d)
```

```python
import functools
import math

import numpy as np
import jax
import jax.numpy as jnp
from jax import lax
from jax.experimental import pallas as pl
from jax.experimental.pallas import tpu as pltpu

F32 = jnp.float32
BF16 = jnp.bfloat16

D_MODEL = 1024
DIFF_HEADS = 4
DIFF_QK_DIM = 64
LANES = 128
MXU_WIDTH = 256
SECTION_ROWS = MXU_WIDTH
HEAD_W = 128
DIFF_WIDTH = DIFF_HEADS * HEAD_W
HGRN_HEADS = 4
HGRN_WIDTH = HGRN_HEADS * HEAD_W
HGRN_CHUNK = 32
ATTN_COLS = 3 * DIFF_WIDTH
HGRN_COLS = 4 * HGRN_WIDTH
REL_BUCKETS = 32
REL_MAX_DIST = 128
PEER_HEADS = 8
PEER_N_KEYS = 128
PEER_N_EXPERTS = PEER_N_KEYS * PEER_N_KEYS
PEER_QUERY_DIM = 256
PEER_TOPK = 16
NORM_EPS = 1e-6
LAM_INIT = 0.8 - 0.6 * math.exp(-0.3 * 0)
NEG_BIG = -1e30

VMEM_LIMIT_BYTES = 56 * 1024 * 1024


def _cparams(sem):
    return pltpu.CompilerParams(dimension_semantics=sem, vmem_limit_bytes=VMEM_LIMIT_BYTES)


def _rms(x, g):
    return x * lax.rsqrt(jnp.mean(x * x, axis=-1, keepdims=True) + NORM_EPS) * g


def _norm_proj_kernel(x_ref, g_ref, w_ref, qt_ref, k_ref, vt_ref, oh_ref):
    y = _rms(x_ref[...], g_ref[...])
    p = jnp.dot(y.astype(BF16), w_ref[...], preferred_element_type=F32)
    qt_ref[...] = p[:, :DIFF_WIDTH].T.astype(BF16)
    k_ref[...] = p[:, DIFF_WIDTH:2 * DIFF_WIDTH].astype(BF16)
    vt_ref[...] = p[:, 2 * DIFF_WIDTH:ATTN_COLS].T.astype(BF16)
    oh_ref[...] = p[:, ATTN_COLS:]


def _norm_proj(x2, g, w_bf16, tb):
    t, d = x2.shape
    n = w_bf16.shape[1]
    return pl.pallas_call(
        _norm_proj_kernel,
        grid=(t // tb,),
        in_specs=[pl.BlockSpec((tb, d), lambda i: (i, 0)),
                  pl.BlockSpec((1, d), lambda i: (0, 0)),
                  pl.BlockSpec((d, n), lambda i: (0, 0))],
        out_specs=[pl.BlockSpec((DIFF_WIDTH, tb), lambda i: (0, i)),
                   pl.BlockSpec((tb, DIFF_WIDTH), lambda i: (i, 0)),
                   pl.BlockSpec((DIFF_WIDTH, tb), lambda i: (0, i)),
                   pl.BlockSpec((tb, HGRN_COLS), lambda i: (i, 0))],
        out_shape=[jax.ShapeDtypeStruct((DIFF_WIDTH, t), BF16),
                   jax.ShapeDtypeStruct((t, DIFF_WIDTH), BF16),
                   jax.ShapeDtypeStruct((DIFF_WIDTH, t), BF16),
                   jax.ShapeDtypeStruct((t, HGRN_COLS), F32)],
        compiler_params=_cparams(("parallel",)),
        name="norm_proj",
    )(x2, g, w_bf16)


def _t5_bucket_np(n):
    n = np.maximum(n, 0).astype(np.int32)
    max_exact = REL_BUCKETS // 2
    nf = np.maximum(n, max_exact).astype(np.float32)
    large = max_exact + (np.log(nf / np.float32(max_exact)) / np.float32(math.log(REL_MAX_DIST / max_exact))
                         * np.float32(REL_BUCKETS - max_exact)).astype(np.int32)
    large = np.minimum(large, REL_BUCKETS - 1)
    return np.where(n < max_exact, n, large).astype(np.int32)


def _bias_tiles_kernel(table_ref, bucket_ref, o_ref):
    h = pl.program_id(0)
    bucket = bucket_ref[...]
    acc = jnp.zeros(bucket.shape, F32)
    for b in range(REL_BUCKETS):
        acc = jnp.where(bucket == b, table_ref[b, h], acc)
    o_ref[...] = acc


def _bias_tiles(table, tq):
    c = np.arange(tq)[:, None]
    r = np.arange(tq)[None, :]
    tiles = np.stack([_t5_bucket_np(d * tq + r - c) for d in (0, 1)])
    tiles = np.concatenate([tiles, tiles], axis=2)
    return pl.pallas_call(
        _bias_tiles_kernel,
        grid=(DIFF_HEADS,),
        in_specs=[pl.BlockSpec(memory_space=pltpu.SMEM),
                  pl.BlockSpec((2, tq, 2 * tq), lambda h: (0, 0, 0))],
        out_specs=pl.BlockSpec((None, 2, tq, 2 * tq), lambda h: (h, 0, 0, 0)),
        out_shape=jax.ShapeDtypeStruct((DIFF_HEADS, 2, tq, 2 * tq), F32),
        compiler_params=_cparams(("arbitrary",)),
        name="bias_tiles",
    )(table, jnp.asarray(tiles))


def _diff_attn_kernel(lam_ref, table_ref, qt_ref, k_ref, vt_ref, bias_ref, g_ref, o_ref,
                      s_ref, mx_ref, l_ref, acc_ref, *, tq):
    h = pl.program_id(1)
    i = pl.program_id(2)
    qt = qt_ref[...] * jnp.asarray(DIFF_QK_DIM ** -0.5, BF16)
    row = lax.broadcasted_iota(jnp.int32, qt.shape, 0)
    zero = jnp.zeros_like(qt)
    q2t = jnp.concatenate([jnp.where(row < DIFF_QK_DIM, qt, zero),
                           jnp.where(row >= DIFF_QK_DIM, qt, zero)], axis=1)
    far_bias = table_ref[REL_BUCKETS - 1, h]

    def fold(op, x):
        return functools.reduce(op, [x[SUBLANES * k:SUBLANES * (k + 1)] for k in range(x.shape[0] // SUBLANES)])

    def scores(j, n_keys, bias, causal):
        start = pl.multiple_of(j * tq, tq)
        s = jnp.dot(k_ref[pl.ds(start, n_keys), :], q2t, preferred_element_type=F32) + bias
        if causal:
            c = lax.broadcasted_iota(jnp.int32, s.shape, 0)
            r = lax.broadcasted_iota(jnp.int32, s.shape, 1) % tq
            s = jnp.where(c <= r, s, NEG_BIG)
        for k in range(n_keys // tq):
            s_ref[j + k] = s[k * tq:(k + 1) * tq]
        mx_ref[...] = jnp.maximum(mx_ref[...], fold(jnp.maximum, s))

    mx_ref[...] = jnp.full(mx_ref.shape, NEG_BIG, F32)
    l_ref[...] = jnp.zeros(l_ref.shape, F32)
    acc_ref[...] = jnp.zeros(acc_ref.shape, F32)

    def far_pair(p, carry):
        scores(2 * p, 2 * tq, far_bias, False)
        return carry

    n_far = jnp.maximum(i - 1, 0)
    lax.fori_loop(0, n_far // 2, far_pair, 0)

    @pl.when(n_far % 2 == 1)
    def _():
        scores(n_far - 1, tq, far_bias, False)

    @pl.when(i >= 1)
    def _():
        scores(i - 1, tq, bias_ref[1], False)

    scores(i, tq, bias_ref[0], True)
    m = jnp.max(mx_ref[...], axis=0, keepdims=True)

    def weigh(j, n_keys):
        start = pl.multiple_of(j * tq, tq)
        s = jnp.concatenate([s_ref[j + k] for k in range(n_keys // tq)], axis=0)
        e = jnp.exp(s - m)
        l_ref[...] += fold(jnp.add, e)
        acc_ref[...] += jnp.dot(vt_ref[:, pl.ds(start, n_keys)], e.astype(BF16), preferred_element_type=F32)

    def weigh_pair(p, carry):
        weigh(2 * p, 2 * tq)
        return carry

    n_blocks = i + 1
    lax.fori_loop(0, n_blocks // 2, weigh_pair, 0)

    @pl.when(n_blocks % 2 == 1)
    def _():
        weigh(i, tq)

    l = jnp.sum(l_ref[...], axis=0, keepdims=True)
    acc = acc_ref[...]
    lam = lam_ref[0]
    out_t = acc[:, :tq] / l[:, :tq] - lam * (acc[:, tq:] / l[:, tq:])
    o_ref[...] = (_rms(out_t.T, g_ref[...]) * (1.0 - LAM_INIT)).astype(o_ref.dtype)


def _diff_attn(lam, table, q_t, k, v_t, bias, g, batch, seq, tq):
    kern = functools.partial(_diff_attn_kernel, tq=tq)
    nq = seq // tq
    return pl.pallas_call(
        kern,
        grid=(batch, DIFF_HEADS, nq),
        in_specs=[pl.BlockSpec(memory_space=pltpu.SMEM),
                  pl.BlockSpec(memory_space=pltpu.SMEM),
                  pl.BlockSpec((HEAD_W, tq), lambda b, h, i: (h, b * nq + i)),
                  pl.BlockSpec((seq, HEAD_W), lambda b, h, i: (b, h)),
                  pl.BlockSpec((HEAD_W, seq), lambda b, h, i: (h, b)),
                  pl.BlockSpec((None, 2, tq, 2 * tq), lambda b, h, i: (h, 0, 0, 0)),
                  pl.BlockSpec((1, HEAD_W), lambda b, h, i: (0, 0))],
        out_specs=pl.BlockSpec((None, tq, HEAD_W), lambda b, h, i: (b, i, h)),
        out_shape=jax.ShapeDtypeStruct((batch, seq, DIFF_WIDTH), BF16),
        scratch_shapes=[pltpu.VMEM((seq // tq, tq, 2 * tq), F32),
                        pltpu.VMEM((SUBLANES, 2 * tq), F32),
                        pltpu.VMEM((SUBLANES, 2 * tq), F32),
                        pltpu.VMEM((HEAD_W, 2 * tq), F32)],
        compiler_params=_cparams(("parallel", "parallel", "arbitrary")),
        name="diff_attn",
    )(lam, table, q_t, k, v_t, bias, g)


def _hgrn_kernel(hq_ref, hf_ref, hi_ref, hg_ref, lb_ref, g_ref, o_ref,
                 qd_ref, kd_ref, cd_ref, oi_ref, upd_ref, prev_ref, *, seq):
    c_len = HGRN_CHUNK
    n_chunks = seq // c_len
    lb = lb_ref[...]
    f = lb + (1.0 - lb) * jax.nn.sigmoid(hf_ref[...])
    log_f = jnp.log(f)
    kk = 1.0 - f
    hq = hq_ref[...]
    qq = hq * jax.nn.sigmoid(hq)
    vv = hi_ref[...]

    pos = lax.broadcasted_iota(jnp.int32, (seq, HEAD_W), 0) % c_len
    b = log_f
    sh = 1
    while sh < c_len:
        b = b + jnp.where(pos >= sh, pltpu.roll(b, sh, axis=0), 0.0)
        sh *= 2
    b3 = b.reshape(n_chunks, c_len, HEAD_W)
    b_last = b3[:, c_len - 1:c_len, :]
    q_dec = qq * jnp.exp(b)
    k_inv = kk * jnp.exp(-b)
    k_dec = kk * jnp.exp(b_last - b3).reshape(seq, HEAD_W)
    qd_ref[...] = q_dec.astype(BF16)
    kd_ref[...] = k_dec.astype(BF16)
    cd_ref[...] = jnp.exp(b_last.reshape(n_chunks, HEAD_W))

    blk = HEAD_W
    nb = seq // blk
    qd_b = q_dec.astype(BF16).reshape(nb, blk, HEAD_W)
    ki_b = k_inv.astype(BF16).reshape(nb, blk, HEAD_W)
    scores = jnp.einsum('nqd,nkd->nqk', qd_b, ki_b, preferred_element_type=F32)
    r = lax.broadcasted_iota(jnp.int32, (blk, blk), 0)
    c = lax.broadcasted_iota(jnp.int32, (blk, blk), 1)
    keep = (c <= r) & ((r // c_len) == (c // c_len))
    scores = jnp.where(keep[None], scores, 0.0)
    o_intra = jnp.einsum('nqk,nkv->nqv', scores.astype(BF16), vv.astype(BF16).reshape(nb, blk, HEAD_W),
                         preferred_element_type=F32).reshape(seq, HEAD_W)

    unroll = 8

    def rows_of(ci):
        return pl.ds(pl.multiple_of(ci * c_len, c_len), c_len)

    def updates(g, carry):
        for u in range(unroll):
            ci = g * unroll + u
            vc = hi_ref[rows_of(ci), :].astype(BF16)
            upd_ref[ci] = lax.dot_general(vc, kd_ref[rows_of(ci), :], (((0,), (0,)), ((), ())),
                                          preferred_element_type=F32)
        return carry

    lax.fori_loop(0, n_chunks // unroll, updates, 0)

    def recur(ci, st):
        prev_ref[ci] = st.astype(BF16)
        return st * cd_ref[pl.ds(ci, 1), :] + upd_ref[ci]

    lax.fori_loop(0, n_chunks, recur, jnp.zeros((HEAD_W, HEAD_W), F32))

    def reads(g, carry):
        for u in range(unroll):
            ci = g * unroll + u
            oi_ref[rows_of(ci), :] = lax.dot_general(qd_ref[rows_of(ci), :], prev_ref[ci],
                                                     (((1,), (1,)), ((), ())), preferred_element_type=F32)
        return carry

    lax.fori_loop(0, n_chunks // unroll, reads, 0)

    o = o_intra + oi_ref[...]
    hg = hg_ref[...]
    o_ref[...] = (_rms(o, g_ref[...]) * (hg * jax.nn.sigmoid(hg))).astype(o_ref.dtype)


def _hgrn(hin, lb, g, batch, seq):
    kern = functools.partial(_hgrn_kernel, seq=seq)
    col = lambda k: (lambda b, h: (b, 0, k * HGRN_HEADS + h))
    blk = (None, seq, HEAD_W)
    return pl.pallas_call(
        kern,
        grid=(batch, HGRN_HEADS),
        in_specs=[pl.BlockSpec(blk, col(0)), pl.BlockSpec(blk, col(1)),
                  pl.BlockSpec(blk, col(2)), pl.BlockSpec(blk, col(3)),
                  pl.BlockSpec((None, 1, HEAD_W), lambda b, h: (h, 0, 0)),
                  pl.BlockSpec((1, HEAD_W), lambda b, h: (0, 0))],
        out_specs=pl.BlockSpec(blk, lambda b, h: (b, 0, h)),
        out_shape=jax.ShapeDtypeStruct((batch, seq, HGRN_WIDTH), BF16),
        scratch_shapes=[pltpu.VMEM((seq, HEAD_W), BF16), pltpu.VMEM((seq, HEAD_W), BF16),
                        pltpu.VMEM((seq // HGRN_CHUNK, HEAD_W), F32), pltpu.VMEM((seq, HEAD_W), F32),
                        pltpu.VMEM((seq // HGRN_CHUNK, HEAD_W, HEAD_W), F32),
                        pltpu.VMEM((seq // HGRN_CHUNK, HEAD_W, HEAD_W), BF16)],
        compiler_params=_cparams(("parallel", "parallel")),
        name="hgrn",
    )(hin, hin, hin, hin, lb, g)


def _out_proj_kernel(x_ref, a_ref, o_ref, wa_ref, wo_ref, g_ref, h_ref, hnt_ref):
    h = (x_ref[...]
         + jnp.dot(a_ref[...], wa_ref[...], preferred_element_type=F32)
         + jnp.dot(o_ref[...], wo_ref[...], preferred_element_type=F32))
    h_ref[...] = h
    hnt_ref[...] = _rms(h, g_ref[...]).T.astype(BF16)


def _out_proj(x2, a2, o2, wa, wo, g, tb):
    t, d = x2.shape
    return pl.pallas_call(
        _out_proj_kernel,
        grid=(t // tb,),
        in_specs=[pl.BlockSpec((tb, d), lambda i: (i, 0)),
                  pl.BlockSpec((tb, DIFF_WIDTH), lambda i: (i, 0)),
                  pl.BlockSpec((tb, HGRN_WIDTH), lambda i: (i, 0)),
                  pl.BlockSpec((DIFF_WIDTH, d), lambda i: (0, 0)),
                  pl.BlockSpec((HGRN_WIDTH, d), lambda i: (0, 0)),
                  pl.BlockSpec((1, d), lambda i: (0, 0))],
        out_specs=[pl.BlockSpec((tb, d), lambda i: (i, 0)),
                   pl.BlockSpec((d, tb), lambda i: (0, i))],
        out_shape=[jax.ShapeDtypeStruct((t, d), F32),
                   jax.ShapeDtypeStruct((d, t), BF16)],
        compiler_params=_cparams(("parallel",)),
        name="out_proj",
    )(x2, a2, o2, wa, wo, g)


SUBLANES = 8
TAU_GUARD = 2.0 ** -21


def _oddeven_merge(lo, hi, r):
    step = r * 2
    if step < hi - lo:
        yield from _oddeven_merge(lo, hi, step)
        yield from _oddeven_merge(lo + r, hi, step)
        yield from [(i, i + r) for i in range(lo + r, hi - r, step)]
    else:
        yield (lo, lo + r)


def _oddeven_merge_sort(lo, hi):
    if hi - lo >= 1:
        mid = lo + (hi - lo) // 2
        yield from _oddeven_merge_sort(lo, mid)
        yield from _oddeven_merge_sort(mid + 1, hi)
        yield from _oddeven_merge(lo, hi, 1)


def _bitonic_merge(n):
    d = n // 2
    while d >= 1:
        yield from [(i, i + d) for i in range(n) if (i // d) % 2 == 0]
        d //= 2


SORT16 = tuple(_oddeven_merge_sort(0, PEER_TOPK - 1))
MERGE16 = tuple(_bitonic_merge(PEER_TOPK))


def _compare_exchange(v, pairs):
    for i, j in pairs:
        v[i], v[j] = jnp.maximum(v[i], v[j]), jnp.minimum(v[i], v[j])


def _top16_sorted(x):
    v = [x[SUBLANES * k:SUBLANES * (k + 1)] for k in range(PEER_TOPK)]
    _compare_exchange(v, SORT16)
    shift = SUBLANES // 2
    while shift >= 1:
        v = [jnp.maximum(v[k], pltpu.roll(v[PEER_TOPK - 1 - k], shift, axis=0)) for k in range(PEER_TOPK)]
        _compare_exchange(v, MERGE16)
        shift //= 2
    return v


def _rows_on_sublanes(rep, start):
    sub = lax.broadcasted_iota(jnp.int32, rep[0].shape, 0)
    out = rep[start]
    for s in range(1, SUBLANES):
        out = jnp.where(sub == s, rep[start + s], out)
    return out


def _pair_candidates(a_rep, b_rep):
    k = PEER_TOPK
    sub = lax.broadcasted_iota(jnp.int32, a_rep[0].shape, 0)
    b_lo = _rows_on_sublanes(b_rep, 0)
    b_hi = _rows_on_sublanes(b_rep, SUBLANES)
    a_hi = _rows_on_sublanes(a_rep, SUBLANES)
    pieces = [a_rep[0] + b_lo, a_rep[0] + b_hi, a_rep[1] + b_lo]
    for i in range(2, SUBLANES):
        pieces.append(jnp.where(sub < k // (i + 1), a_rep[i] + b_lo, -jnp.inf))
    pieces.append(a_hi + b_rep[0])
    return jnp.concatenate(pieces, axis=0)


def _kth_largest(x, k):
    for r in range(k):
        m = jnp.max(x, axis=0, keepdims=True)
        if r + 1 < k:
            x = jnp.where(x == m, -jnp.inf, x)
    return m


def _peer_route_kernel(hnt_ref, wqt_ref, keys_ref, tau_ref, alpha_ref, s2_ref, beta_ref, qt_ref, s_ref):
    qt_ref[...] = jnp.dot(wqt_ref[...], hnt_ref[...], preferred_element_type=F32).astype(BF16)
    half = PEER_QUERY_DIM // 2
    lane_tiles = hnt_ref.shape[1] // LANES
    k = PEER_TOPK

    def head(h, carry):
        base = pl.multiple_of(h * PEER_QUERY_DIM, PEER_QUERY_DIM)
        for c in range(2):
            s = jnp.dot(keys_ref[h, c], qt_ref[pl.ds(base + c * half, half), :], preferred_element_type=F32)
            for lt in range(lane_tiles):
                s_ref[c, lt] = s[:, lt * LANES:(lt + 1) * LANES]

        def tile(lt, carry):
            cols = pl.ds(pl.multiple_of(lt * LANES, LANES), LANES)
            s1 = s_ref[0, lt]
            s2 = s_ref[1, lt]
            a_rep = _top16_sorted(s1)
            b_rep = _top16_sorted(s2)
            cands = _pair_candidates(a_rep, b_rep)
            thr = _kth_largest(cands, k)
            top = a_rep[0][0:1] + b_rep[0][0:1]
            z = jnp.sum(jnp.where(cands >= thr, jnp.exp(cands - top), 0.0), axis=0, keepdims=True)
            tau_ref[h, :, cols] = (thr - s1) - (jnp.abs(thr) + jnp.abs(s1)) * TAU_GUARD
            alpha_ref[h, :, cols] = jnp.exp(s1 - a_rep[0][0:1]) * (0.5 / z)
            s2_ref[h, lt] = s2
            beta_ref[h, lt] = jnp.exp(s2 - b_rep[0][0:1])
            return carry

        return lax.fori_loop(0, lane_tiles, tile, carry)

    lax.fori_loop(0, PEER_HEADS, head, 0)


def _peer_route(hnt, wqt, keys, tb):
    d, t = hnt.shape
    nq = wqt.shape[0]
    oshape = jax.ShapeDtypeStruct((PEER_HEADS, PEER_N_KEYS, t), F32)
    ospec = pl.BlockSpec((PEER_HEADS, PEER_N_KEYS, tb), lambda i: (0, 0, i))
    tshape = jax.ShapeDtypeStruct((PEER_HEADS, t // LANES, PEER_N_KEYS, LANES), F32)
    tspec = pl.BlockSpec((PEER_HEADS, tb // LANES, PEER_N_KEYS, LANES), lambda i: (0, i, 0, 0))
    return pl.pallas_call(
        _peer_route_kernel,
        grid=(t // tb,),
        in_specs=[pl.BlockSpec((d, tb), lambda i: (0, i)),
                  pl.BlockSpec((nq, d), lambda i: (0, 0)),
                  pl.BlockSpec(keys.shape, lambda i: (0, 0, 0, 0))],
        out_specs=[ospec, ospec, tspec, tspec],
        out_shape=[oshape, oshape, tshape, tshape],
        scratch_shapes=[pltpu.VMEM((nq, tb), BF16), pltpu.VMEM((2, tb // LANES, PEER_N_KEYS, LANES), F32)],
        compiler_params=_cparams(("parallel",)),
        name="peer_route",
    )(hnt, wqt, keys)


def _peer_dense_kernel(hnt_ref, u_ref, vt_ref, tau_ref, alpha_ref, s2_ref, beta_ref, o_ref,
                       hid_ref, act_ref, *, te, tb, n_tiles):
    j = pl.program_id(1)
    cur = j % 2
    prev = 1 - cur
    rows_per_tile = te // PEER_N_KEYS
    lane_tiles = tb // LANES

    def hidden(hid_ref):
        hid = jnp.dot(u_ref[...], hnt_ref[...], preferred_element_type=F32)
        for lt in range(lane_tiles):
            hid_ref[lt] = hid[:, lt * LANES:(lt + 1) * LANES]

    def activations(hid_ref, act_ref):
        def lane_tile(lt, carry):
            cols = pl.ds(pl.multiple_of(lt * LANES, LANES), LANES)
            for r in range(rows_per_tile):
                w = None
                for h in range(PEER_HEADS):
                    sel = s2_ref[h, lt] >= tau_ref[h, r:r + 1, cols]
                    term = jnp.where(sel, beta_ref[h, lt], 0.0) * alpha_ref[h, r:r + 1, cols]
                    w = term if w is None else w + term
                rows = slice(r * PEER_N_KEYS, (r + 1) * PEER_N_KEYS)
                hid_t = hid_ref[lt, rows, :]
                gelu2 = hid_t * (1.0 + lax.erf(hid_t * (2.0 ** -0.5)))
                act_ref[lt, rows, :] = (gelu2 * w).astype(BF16)
            return carry

        lax.fori_loop(0, lane_tiles, lane_tile, 0)

    def project(act_ref):
        act = jnp.concatenate([act_ref[lt] for lt in range(lane_tiles)], axis=1)
        o_ref[...] += jnp.dot(vt_ref[...], act, preferred_element_type=F32)

    def run(do_a, do_b, do_c):
        if do_a:
            hidden(hid_ref.at[cur])
        if do_b:
            activations(hid_ref.at[prev], act_ref.at[prev])
        if do_c:
            project(act_ref.at[cur])

    @pl.when(jnp.logical_and(j >= 2, j < n_tiles))
    def _():
        run(True, True, True)

    @pl.when(j == 0)
    def _():
        o_ref[...] = jnp.zeros(o_ref.shape, o_ref.dtype)
        run(True, False, False)

    @pl.when(j == 1)
    def _():
        run(True, True, False)

    @pl.when(j == n_tiles)
    def _():
        run(False, True, True)

    @pl.when(j == n_tiles + 1)
    def _():
        run(False, False, True)


def _peer_dense(hnt, u_bf16, v_bf16, tau, alpha, s2, beta, tb, te):
    d, t = hnt.shape
    n_tiles = u_bf16.shape[0] // te
    assert n_tiles >= 2
    vt_tiles = v_bf16.reshape(n_tiles, te, d).transpose(0, 2, 1)
    kern = functools.partial(_peer_dense_kernel, te=te, tb=tb, n_tiles=n_tiles)
    last = n_tiles - 1
    rspec = pl.BlockSpec((PEER_HEADS, tb // LANES, PEER_N_KEYS, LANES), lambda i, j: (0, i, 0, 0))
    i1spec = pl.BlockSpec((PEER_HEADS, te // PEER_N_KEYS, tb), lambda i, j: (0, jnp.clip(j - 1, 0, last), i))
    stage_shape = (tb // LANES, te, LANES)
    return pl.pallas_call(
        kern,
        grid=(t // tb, n_tiles + 2),
        in_specs=[pl.BlockSpec((d, tb), lambda i, j: (0, i)),
                  pl.BlockSpec((te, d), lambda i, j: (jnp.minimum(j, last), 0)),
                  pl.BlockSpec((None, d, te), lambda i, j: (jnp.clip(j - 2, 0, last), 0, 0)),
                  i1spec, i1spec, rspec, rspec],
        out_specs=pl.BlockSpec((d, tb), lambda i, j: (0, i)),
        out_shape=jax.ShapeDtypeStruct((d, t), F32),
        scratch_shapes=[pltpu.VMEM((2,) + stage_shape, F32), pltpu.VMEM((2,) + stage_shape, BF16)],
        compiler_params=_cparams(("parallel", "arbitrary")),
        name="peer_dense",
    )(hnt, u_bf16, vt_tiles, tau, alpha, s2, beta)


def _final_norm_kernel(h_ref, pt_ref, g_ref, o_ref):
    o_ref[...] = _rms(h_ref[...] + pt_ref[...].T, g_ref[...])


def _final_norm(h2, peer_t, g, tb):
    t, d = h2.shape
    return pl.pallas_call(
        _final_norm_kernel,
        grid=(t // tb,),
        in_specs=[pl.BlockSpec((tb, d), lambda i: (i, 0)),
                  pl.BlockSpec((d, tb), lambda i: (0, i)),
                  pl.BlockSpec((1, d), lambda i: (0, 0))],
        out_specs=pl.BlockSpec((tb, d), lambda i: (i, 0)),
        out_shape=jax.ShapeDtypeStruct((t, d), F32),
        compiler_params=_cparams(("parallel",)),
        name="final_norm",
    )(h2, peer_t, g)


def _largest_divisor(n, cap, multiple):
    best = multiple
    k = multiple
    while k <= min(n, cap):
        if n % k == 0:
            best = k
        k += multiple
    return best


def kernel(x, norm1_g, w_in, diff_lambda_q1, diff_lambda_k1, diff_lambda_q2, diff_lambda_k2, diff_subln_g,
           rel_bias_table, hgrn_lb_logits, hgrn_gnorm_g, w_out, norm2_g, peer_w_q, peer_sub_keys, peer_u,
           peer_v, final_norm_g):
    batch, seq, d = x.shape
    t = batch * seq
    assert d == D_MODEL and seq % 256 == 0 and w_in.shape[0] == 1
    tb_proj = _largest_divisor(t, 512, 256)
    tq = 256
    tb_route = _largest_divisor(t, 512, 256)
    tb_peer = _largest_divisor(t, 1024, 512)
    te_peer = 1024

    x2 = x.reshape(t, d)
    row = lambda v: v.reshape(1, -1).astype(F32)

    lam = (jnp.exp(jnp.sum(diff_lambda_q1[0].astype(F32) * diff_lambda_k1[0].astype(F32)))
           - jnp.exp(jnp.sum(diff_lambda_q2[0].astype(F32) * diff_lambda_k2[0].astype(F32)))
           + LAM_INIT).reshape(1)
    lb = jnp.cumsum(jax.nn.softmax(hgrn_lb_logits.astype(F32), axis=0), axis=0)[0]
    lb = lb.reshape(HGRN_HEADS, 1, HEAD_W)
    table = rel_bias_table.astype(F32)

    q_t, k, v_t, hin = _norm_proj(x2, row(norm1_g[0]), w_in[0].astype(BF16), tb_proj)
    bias = _bias_tiles(table, tq)
    a = _diff_attn(lam, table, q_t, k, v_t, bias, row(diff_subln_g[0]), batch, seq, tq)
    o = _hgrn(hin.reshape(batch, seq, HGRN_COLS), lb, row(hgrn_gnorm_g[0]), batch, seq)
    w_o = w_out[0].astype(BF16)
    h2, hnt = _out_proj(x2, a.reshape(t, DIFF_WIDTH), o.reshape(t, HGRN_WIDTH),
                        w_o[:DIFF_WIDTH], w_o[DIFF_WIDTH:], row(norm2_g[0]), tb_proj)
    tau, alpha, s2, beta = _peer_route(hnt, peer_w_q[0].T.astype(BF16), peer_sub_keys[0].astype(BF16), tb_route)
    peer_t = _peer_dense(hnt, peer_u[0].astype(BF16), peer_v[0].astype(BF16), tau, alpha, s2, beta,
                         tb_peer, te_peer)
    out = _final_norm(h2, peer_t, row(final_norm_g), tb_proj)
    return out.reshape(batch, seq, d)
```

```python
import functools
import math

import numpy as np
import jax
import jax.numpy as jnp
from jax import lax
from jax.experimental import pallas as pl
from jax.experimental.pallas import tpu as pltpu

F32 = jnp.float32
BF16 = jnp.bfloat16

D_MODEL = 1024
DIFF_HEADS = 4
DIFF_QK_DIM = 64
LANES = 128
MXU_WIDTH = 256
SECTION_ROWS = MXU_WIDTH
HEAD_W = 128
DIFF_WIDTH = DIFF_HEADS * HEAD_W
HGRN_HEADS = 4
HGRN_WIDTH = HGRN_HEADS * HEAD_W
HGRN_CHUNK = 32
ATTN_COLS = 3 * DIFF_WIDTH
HGRN_COLS = 4 * HGRN_WIDTH
REL_BUCKETS = 32
REL_MAX_DIST = 128
PEER_HEADS = 8
PEER_N_KEYS = 128
PEER_N_EXPERTS = PEER_N_KEYS * PEER_N_KEYS
PEER_QUERY_DIM = 256
PEER_TOPK = 16
NORM_EPS = 1e-6
LAM_INIT = 0.8 - 0.6 * math.exp(-0.3 * 0)
NEG_BIG = -1e30

VMEM_LIMIT_BYTES = 56 * 1024 * 1024


def _cparams(sem):
    return pltpu.CompilerParams(dimension_semantics=sem, vmem_limit_bytes=VMEM_LIMIT_BYTES)


def _rms(x, g):
    return x * lax.rsqrt(jnp.mean(x * x, axis=-1, keepdims=True) + NORM_EPS) * g


def _norm_proj_kernel(x_ref, g_ref, w_ref, qt_ref, k_ref, vt_ref, oh_ref):
    y = _rms(x_ref[...], g_ref[...])
    p = jnp.dot(y.astype(BF16), w_ref[...], preferred_element_type=F32)
    qt_ref[...] = p[:, :DIFF_WIDTH].T.astype(BF16)
    k_ref[...] = p[:, DIFF_WIDTH:2 * DIFF_WIDTH].astype(BF16)
    vt_ref[...] = p[:, 2 * DIFF_WIDTH:ATTN_COLS].T.astype(BF16)
    oh_ref[...] = p[:, ATTN_COLS:]


def _norm_proj(x2, g, w_bf16, tb):
    t, d = x2.shape
    n = w_bf16.shape[1]
    return pl.pallas_call(
        _norm_proj_kernel,
        grid=(t // tb,),
        in_specs=[pl.BlockSpec((tb, d), lambda i: (i, 0)),
                  pl.BlockSpec((1, d), lambda i: (0, 0)),
                  pl.BlockSpec((d, n), lambda i: (0, 0))],
        out_specs=[pl.BlockSpec((DIFF_WIDTH, tb), lambda i: (0, i)),
                   pl.BlockSpec((tb, DIFF_WIDTH), lambda i: (i, 0)),
                   pl.BlockSpec((DIFF_WIDTH, tb), lambda i: (0, i)),
                   pl.BlockSpec((tb, HGRN_COLS), lambda i: (i, 0))],
        out_shape=[jax.ShapeDtypeStruct((DIFF_WIDTH, t), BF16),
                   jax.ShapeDtypeStruct((t, DIFF_WIDTH), BF16),
                   jax.ShapeDtypeStruct((DIFF_WIDTH, t), BF16),
                   jax.ShapeDtypeStruct((t, HGRN_COLS), F32)],
        compiler_params=_cparams(("parallel",)),
        name="norm_proj",
    )(x2, g, w_bf16)


def _t5_bucket_np(n):
    n = np.maximum(n, 0).astype(np.int32)
    max_exact = REL_BUCKETS // 2
    nf = np.maximum(n, max_exact).astype(np.float32)
    large = max_exact + (np.log(nf / np.float32(max_exact)) / np.float32(math.log(REL_MAX_DIST / max_exact))
                         * np.float32(REL_BUCKETS - max_exact)).astype(np.int32)
    large = np.minimum(large, REL_BUCKETS - 1)
    return np.where(n < max_exact, n, large).astype(np.int32)


def _bias_tiles_kernel(table_ref, bucket_ref, o_ref):
    h = pl.program_id(0)
    bucket = bucket_ref[...]
    acc = jnp.zeros(bucket.shape, F32)
    for b in range(REL_BUCKETS):
        acc = jnp.where(bucket == b, table_ref[b, h], acc)
    o_ref[...] = acc


def _bias_tiles(table, tq):
    c = np.arange(tq)[:, None]
    r = np.arange(tq)[None, :]
    tiles = np.stack([_t5_bucket_np(d * tq + r - c) for d in (0, 1)])
    tiles = np.concatenate([tiles, tiles], axis=2)
    return pl.pallas_call(
        _bias_tiles_kernel,
        grid=(DIFF_HEADS,),
        in_specs=[pl.BlockSpec(memory_space=pltpu.SMEM),
                  pl.BlockSpec((2, tq, 2 * tq), lambda h: (0, 0, 0))],
        out_specs=pl.BlockSpec((None, 2, tq, 2 * tq), lambda h: (h, 0, 0, 0)),
        out_shape=jax.ShapeDtypeStruct((DIFF_HEADS, 2, tq, 2 * tq), F32),
        compiler_params=_cparams(("arbitrary",)),
        name="bias_tiles",
    )(table, jnp.asarray(tiles))


def _diff_attn_kernel(lam_ref, table_ref, qt_ref, k_ref, vt_ref, bias_ref, g_ref, o_ref,
                      s_ref, mx_ref, l_ref, acc_ref, *, tq):
    h = pl.program_id(1)
    i = pl.program_id(2)
    qt = qt_ref[...] * jnp.asarray(DIFF_QK_DIM ** -0.5, BF16)
    row = lax.broadcasted_iota(jnp.int32, qt.shape, 0)
    zero = jnp.zeros_like(qt)
    q2t = jnp.concatenate([jnp.where(row < DIFF_QK_DIM, qt, zero),
                           jnp.where(row >= DIFF_QK_DIM, qt, zero)], axis=1)
    far_bias = table_ref[REL_BUCKETS - 1, h]

    def fold(op, x):
        return functools.reduce(op, [x[SUBLANES * k:SUBLANES * (k + 1)] for k in range(x.shape[0] // SUBLANES)])

    def scores(j, n_keys, bias, causal):
        start = pl.multiple_of(j * tq, tq)
        s = jnp.dot(k_ref[pl.ds(start, n_keys), :], q2t, preferred_element_type=F32) + bias
        if causal:
            c = lax.broadcasted_iota(jnp.int32, s.shape, 0)
            r = lax.broadcasted_iota(jnp.int32, s.shape, 1) % tq
            s = jnp.where(c <= r, s, NEG_BIG)
        for k in range(n_keys // tq):
            s_ref[j + k] = s[k * tq:(k + 1) * tq]
        mx_ref[...] = jnp.maximum(mx_ref[...], fold(jnp.maximum, s))

    mx_ref[...] = jnp.full(mx_ref.shape, NEG_BIG, F32)
    l_ref[...] = jnp.zeros(l_ref.shape, F32)
    acc_ref[...] = jnp.zeros(acc_ref.shape, F32)

    def far_pair(p, carry):
        scores(2 * p, 2 * tq, far_bias, False)
        return carry

    n_far = jnp.maximum(i - 1, 0)
    lax.fori_loop(0, n_far // 2, far_pair, 0)

    @pl.when(n_far % 2 == 1)
    def _():
        scores(n_far - 1, tq, far_bias, False)

    @pl.when(i >= 1)
    def _():
        scores(i - 1, tq, bias_ref[1], False)

    scores(i, tq, bias_ref[0], True)
    m = jnp.max(mx_ref[...], axis=0, keepdims=True)

    def weigh(j, n_keys):
        start = pl.multiple_of(j * tq, tq)
        s = jnp.concatenate([s_ref[j + k] for k in range(n_keys // tq)], axis=0)
        e = jnp.exp(s - m)
        l_ref[...] += fold(jnp.add, e)
        acc_ref[...] += jnp.dot(vt_ref[:, pl.ds(start, n_keys)], e.astype(BF16), preferred_element_type=F32)

    def weigh_pair(p, carry):
        weigh(2 * p, 2 * tq)
        return carry

    n_blocks = i + 1
    lax.fori_loop(0, n_blocks // 2, weigh_pair, 0)

    @pl.when(n_blocks % 2 == 1)
    def _():
        weigh(i, tq)

    l = jnp.sum(l_ref[...], axis=0, keepdims=True)
    acc = acc_ref[...]
    lam = lam_ref[0]
    out_t = acc[:, :tq] / l[:, :tq] - lam * (acc[:, tq:] / l[:, tq:])
    o_ref[...] = (_rms(out_t.T, g_ref[...]) * (1.0 - LAM_INIT)).astype(o_ref.dtype)


def _diff_attn(lam, table, q_t, k, v_t, bias, g, batch, seq, tq):
    kern = functools.partial(_diff_attn_kernel, tq=tq)
    nq = seq // tq
    return pl.pallas_call(
        kern,
        grid=(batch, DIFF_HEADS, nq),
        in_specs=[pl.BlockSpec(memory_space=pltpu.SMEM),
                  pl.BlockSpec(memory_space=pltpu.SMEM),
                  pl.BlockSpec((HEAD_W, tq), lambda b, h, i: (h, b * nq + i)),
                  pl.BlockSpec((seq, HEAD_W), lambda b, h, i: (b, h)),
                  pl.BlockSpec((HEAD_W, seq), lambda b, h, i: (h, b)),
                  pl.BlockSpec((None, 2, tq, 2 * tq), lambda b, h, i: (h, 0, 0, 0)),
                  pl.BlockSpec((1, HEAD_W), lambda b, h, i: (0, 0))],
        out_specs=pl.BlockSpec((None, tq, HEAD_W), lambda b, h, i: (b, i, h)),
        out_shape=jax.ShapeDtypeStruct((batch, seq, DIFF_WIDTH), BF16),
        scratch_shapes=[pltpu.VMEM((seq // tq, tq, 2 * tq), F32),
                        pltpu.VMEM((SUBLANES, 2 * tq), F32),
                        pltpu.VMEM((SUBLANES, 2 * tq), F32),
                        pltpu.VMEM((HEAD_W, 2 * tq), F32)],
        compiler_params=_cparams(("parallel", "parallel", "arbitrary")),
        name="diff_attn",
    )(lam, table, q_t, k, v_t, bias, g)


def _hgrn_kernel(hq_ref, hf_ref, hi_ref, hg_ref, lb_ref, g_ref, o_ref,
                 qd_ref, kd_ref, cd_ref, oi_ref, upd_ref, prev_ref, *, seq):
    c_len = HGRN_CHUNK
    n_chunks = seq // c_len
    lb = lb_ref[...]
    f = lb + (1.0 - lb) * jax.nn.sigmoid(hf_ref[...])
    log_f = jnp.log(f)
    kk = 1.0 - f
    hq = hq_ref[...]
    qq = hq * jax.nn.sigmoid(hq)
    vv = hi_ref[...]

    pos = lax.broadcasted_iota(jnp.int32, (seq, HEAD_W), 0) % c_len
    b = log_f
    sh = 1
    while sh < c_len:
        b = b + jnp.where(pos >= sh, pltpu.roll(b, sh, axis=0), 0.0)
        sh *= 2
    b3 = b.reshape(n_chunks, c_len, HEAD_W)
    b_last = b3[:, c_len - 1:c_len, :]
    q_dec = qq * jnp.exp(b)
    k_inv = kk * jnp.exp(-b)
    k_dec = kk * jnp.exp(b_last - b3).reshape(seq, HEAD_W)
    qd_ref[...] = q_dec.astype(BF16)
    kd_ref[...] = k_dec.astype(BF16)
    cd_ref[...] = jnp.exp(b_last.reshape(n_chunks, HEAD_W))

    blk = HEAD_W
    nb = seq // blk
    qd_b = q_dec.astype(BF16).reshape(nb, blk, HEAD_W)
    ki_b = k_inv.astype(BF16).reshape(nb, blk, HEAD_W)
    scores = jnp.einsum('nqd,nkd->nqk', qd_b, ki_b, preferred_element_type=F32)
    r = lax.broadcasted_iota(jnp.int32, (blk, blk), 0)
    c = lax.broadcasted_iota(jnp.int32, (blk, blk), 1)
    keep = (c <= r) & ((r // c_len) == (c // c_len))
    scores = jnp.where(keep[None], scores, 0.0)
    o_intra = jnp.einsum('nqk,nkv->nqv', scores.astype(BF16), vv.astype(BF16).reshape(nb, blk, HEAD_W),
                         preferred_element_type=F32).reshape(seq, HEAD_W)

    unroll = 8

    def rows_of(ci):
        return pl.ds(pl.multiple_of(ci * c_len, c_len), c_len)

    def updates(g, carry):
        for u in range(unroll):
            ci = g * unroll + u
            vc = hi_ref[rows_of(ci), :].astype(BF16)
            upd_ref[ci] = lax.dot_general(vc, kd_ref[rows_of(ci), :], (((0,), (0,)), ((), ())),
                                          preferred_element_type=F32)
        return carry

    lax.fori_loop(0, n_chunks // unroll, updates, 0)

    def recur(ci, st):
        prev_ref[ci] = st.astype(BF16)
        return st * cd_ref[pl.ds(ci, 1), :] + upd_ref[ci]

    lax.fori_loop(0, n_chunks, recur, jnp.zeros((HEAD_W, HEAD_W), F32))

    def reads(g, carry):
        for u in range(unroll):
            ci = g * unroll + u
            oi_ref[rows_of(ci), :] = lax.dot_general(qd_ref[rows_of(ci), :], prev_ref[ci],
                                                     (((1,), (1,)), ((), ())), preferred_element_type=F32)
        return carry

    lax.fori_loop(0, n_chunks // unroll, reads, 0)

    o = o_intra + oi_ref[...]
    hg = hg_ref[...]
    o_ref[...] = (_rms(o, g_ref[...]) * (hg * jax.nn.sigmoid(hg))).astype(o_ref.dtype)


def _hgrn(hin, lb, g, batch, seq):
    kern = functools.partial(_hgrn_kernel, seq=seq)
    col = lambda k: (lambda b, h: (b, 0, k * HGRN_HEADS + h))
    blk = (None, seq, HEAD_W)
    return pl.pallas_call(
        kern,
        grid=(batch, HGRN_HEADS),
        in_specs=[pl.BlockSpec(blk, col(0)), pl.BlockSpec(blk, col(1)),
                  pl.BlockSpec(blk, col(2)), pl.BlockSpec(blk, col(3)),
                  pl.BlockSpec((None, 1, HEAD_W), lambda b, h: (h, 0, 0)),
                  pl.BlockSpec((1, HEAD_W), lambda b, h: (0, 0))],
        out_specs=pl.BlockSpec(blk, lambda b, h: (b, 0, h)),
        out_shape=jax.ShapeDtypeStruct((batch, seq, HGRN_WIDTH), BF16),
        scratch_shapes=[pltpu.VMEM((seq, HEAD_W), BF16), pltpu.VMEM((seq, HEAD_W), BF16),
                        pltpu.VMEM((seq // HGRN_CHUNK, HEAD_W), F32), pltpu.VMEM((seq, HEAD_W), F32),
                        pltpu.VMEM((seq // HGRN_CHUNK, HEAD_W, HEAD_W), F32),
                        pltpu.VMEM((seq // HGRN_CHUNK, HEAD_W, HEAD_W), BF16)],
        compiler_params=_cparams(("parallel", "parallel")),
        name="hgrn",
    )(hin, hin, hin, hin, lb, g)


def _out_proj_kernel(x_ref, a_ref, o_ref, wa_ref, wo_ref, g_ref, h_ref, hnt_ref):
    h = (x_ref[...]
         + jnp.dot(a_ref[...], wa_ref[...], preferred_element_type=F32)
         + jnp.dot(o_ref[...], wo_ref[...], preferred_element_type=F32))
    h_ref[...] = h
    hnt_ref[...] = _rms(h, g_ref[...]).T.astype(BF16)


def _out_proj(x2, a2, o2, wa, wo, g, tb):
    t, d = x2.shape
    return pl.pallas_call(
        _out_proj_kernel,
        grid=(t // tb,),
        in_specs=[pl.BlockSpec((tb, d), lambda i: (i, 0)),
                  pl.BlockSpec((tb, DIFF_WIDTH), lambda i: (i, 0)),
                  pl.BlockSpec((tb, HGRN_WIDTH), lambda i: (i, 0)),
                  pl.BlockSpec((DIFF_WIDTH, d), lambda i: (0, 0)),
                  pl.BlockSpec((HGRN_WIDTH, d), lambda i: (0, 0)),
                  pl.BlockSpec((1, d), lambda i: (0, 0))],
        out_specs=[pl.BlockSpec((tb, d), lambda i: (i, 0)),
                   pl.BlockSpec((d, tb), lambda i: (0, i))],
        out_shape=[jax.ShapeDtypeStruct((t, d), F32),
                   jax.ShapeDtypeStruct((d, t), BF16)],
        compiler_params=_cparams(("parallel",)),
        name="out_proj",
    )(x2, a2, o2, wa, wo, g)


SUBLANES = 8
TAU_GUARD = 2.0 ** -21


def _oddeven_merge(lo, hi, r):
    step = r * 2
    if step < hi - lo:
        yield from _oddeven_merge(lo, hi, step)
        yield from _oddeven_merge(lo + r, hi, step)
        yield from [(i, i + r) for i in range(lo + r, hi - r, step)]
    else:
        yield (lo, lo + r)


def _oddeven_merge_sort(lo, hi):
    if hi - lo >= 1:
        mid = lo + (hi - lo) // 2
        yield from _oddeven_merge_sort(lo, mid)
        yield from _oddeven_merge_sort(mid + 1, hi)
        yield from _oddeven_merge(lo, hi, 1)


def _bitonic_merge(n):
    d = n // 2
    while d >= 1:
        yield from [(i, i + d) for i in range(n) if (i // d) % 2 == 0]
        d //= 2


SORT16 = tuple(_oddeven_merge_sort(0, PEER_TOPK - 1))
MERGE16 = tuple(_bitonic_merge(PEER_TOPK))


def _bf16_row_tile(row, n_rows):
    packed_rows = 16
    tile16 = jnp.broadcast_to(row, (packed_rows, row.shape[1])).astype(BF16)
    return jnp.broadcast_to(tile16[None], (n_rows // packed_rows, packed_rows, row.shape[1])
                            ).reshape(n_rows, row.shape[1])


def _compare_exchange(v, pairs):
    for i, j in pairs:
        v[i], v[j] = jnp.maximum(v[i], v[j]), jnp.minimum(v[i], v[j])


def _top16_sorted(x):
    v = [x[SUBLANES * k:SUBLANES * (k + 1)] for k in range(PEER_TOPK)]
    _compare_exchange(v, SORT16)
    shift = SUBLANES // 2
    while shift >= 1:
        v = [jnp.maximum(v[k], pltpu.roll(v[PEER_TOPK - 1 - k], shift, axis=0)) for k in range(PEER_TOPK)]
        _compare_exchange(v, MERGE16)
        shift //= 2
    return v


def _rows_on_sublanes(rep, start):
    sub = lax.broadcasted_iota(jnp.int32, rep[0].shape, 0)
    out = rep[start]
    for s in range(1, SUBLANES):
        out = jnp.where(sub == s, rep[start + s], out)
    return out


def _pair_candidates(a_rep, b_rep):
    k = PEER_TOPK
    sub = lax.broadcasted_iota(jnp.int32, a_rep[0].shape, 0)
    b_lo = _rows_on_sublanes(b_rep, 0)
    b_hi = _rows_on_sublanes(b_rep, SUBLANES)
    a_hi = _rows_on_sublanes(a_rep, SUBLANES)
    pieces = [a_rep[0] + b_lo, a_rep[0] + b_hi, a_rep[1] + b_lo]
    for i in range(2, SUBLANES):
        pieces.append(jnp.where(sub < k // (i + 1), a_rep[i] + b_lo, -jnp.inf))
    pieces.append(a_hi + b_rep[0])
    return jnp.concatenate(pieces, axis=0)


def _kth_largest(x, k):
    for r in range(k):
        m = jnp.max(x, axis=0, keepdims=True)
        if r + 1 < k:
            x = jnp.where(x == m, -jnp.inf, x)
    return m


def _peer_route_kernel(hnt_ref, wqt_ref, keys_ref, rank_ref, alpha_ref, nsel_ref, beta_ref, qt_ref, s_ref):
    qt_ref[...] = jnp.dot(wqt_ref[...], hnt_ref[...], preferred_element_type=F32).astype(BF16)
    half = PEER_QUERY_DIM // 2
    lane_tiles = hnt_ref.shape[1] // LANES
    k = PEER_TOPK

    def head(h, carry):
        base = pl.multiple_of(h * PEER_QUERY_DIM, PEER_QUERY_DIM)
        for c in range(2):
            s = jnp.dot(keys_ref[h, c], qt_ref[pl.ds(base + c * half, half), :], preferred_element_type=F32)
            for lt in range(lane_tiles):
                s_ref[c, lt] = s[:, lt * LANES:(lt + 1) * LANES]

        def tile(lt, carry):
            cols = pl.ds(pl.multiple_of(lt * LANES, LANES), LANES)
            s1 = s_ref[0, lt]
            s2 = s_ref[1, lt]
            a_rep = _top16_sorted(s1)
            b_rep = _top16_sorted(s2)
            cands = _pair_candidates(a_rep, b_rep)
            thr = _kth_largest(cands, k)
            top = a_rep[0][0:1] + b_rep[0][0:1]
            z = jnp.sum(jnp.where(cands >= thr, jnp.exp(cands - top), 0.0), axis=0, keepdims=True)
            rank1 = []
            n_sel = []
            theta = [(thr - a_rep[j]) - (jnp.abs(thr) + jnp.abs(a_rep[j])) * TAU_GUARD for j in range(k)]
            for v in range(PEER_N_KEYS // SUBLANES):
                x1 = s1[SUBLANES * v:SUBLANES * (v + 1)]
                x2 = s2[SUBLANES * v:SUBLANES * (v + 1)]
                r1 = jnp.zeros_like(x1)
                ns = jnp.zeros_like(x2)
                for j in range(k):
                    r1 = jnp.where(a_rep[j] > x1, float(j + 1), r1)
                    ns = jnp.where(x2 >= theta[j], float(j + 1), ns)
                rank1.append(r1)
                n_sel.append(ns)
            rank_ref[h, :, cols] = jnp.concatenate(rank1, axis=0)
            alpha_ref[h, :, cols] = jnp.exp(s1 - a_rep[0][0:1]) * (0.5 / z)
            nsel_ref[h, lt] = jnp.concatenate(n_sel, axis=0).astype(BF16)
            beta_ref[h, lt] = jnp.exp(s2 - b_rep[0][0:1]).astype(BF16)
            return carry

        return lax.fori_loop(0, lane_tiles, tile, carry)

    lax.fori_loop(0, PEER_HEADS, head, 0)


def _peer_route(hnt, wqt, keys, tb):
    d, t = hnt.shape
    nq = wqt.shape[0]
    oshape = jax.ShapeDtypeStruct((PEER_HEADS, PEER_N_KEYS, t), F32)
    ospec = pl.BlockSpec((PEER_HEADS, PEER_N_KEYS, tb), lambda i: (0, 0, i))
    tshape = jax.ShapeDtypeStruct((PEER_HEADS, t // LANES, PEER_N_KEYS, LANES), BF16)
    tspec = pl.BlockSpec((PEER_HEADS, tb // LANES, PEER_N_KEYS, LANES), lambda i: (0, i, 0, 0))
    return pl.pallas_call(
        _peer_route_kernel,
        grid=(t // tb,),
        in_specs=[pl.BlockSpec((d, tb), lambda i: (0, i)),
                  pl.BlockSpec((nq, d), lambda i: (0, 0)),
                  pl.BlockSpec(keys.shape, lambda i: (0, 0, 0, 0))],
        out_specs=[ospec, ospec, tspec, tspec],
        out_shape=[oshape, oshape, tshape, tshape],
        scratch_shapes=[pltpu.VMEM((nq, tb), BF16), pltpu.VMEM((2, tb // LANES, PEER_N_KEYS, LANES), F32)],
        compiler_params=_cparams(("parallel",)),
        name="peer_route",
    )(hnt, wqt, keys)


def _peer_dense_kernel(hnt_ref, u_ref, vt_ref, rank_ref, alpha_ref, nsel_ref, beta_ref, o_ref,
                       hid_ref, act_ref, *, te, tb, n_tiles):
    j = pl.program_id(1)
    cur = j % 2
    prev = 1 - cur
    rows_per_tile = te // PEER_N_KEYS
    lane_tiles = tb // LANES

    def hidden(hid_ref):
        hid = jnp.dot(u_ref[...], hnt_ref[...], preferred_element_type=F32)
        for lt in range(lane_tiles):
            hid_ref[lt] = hid[:, lt * LANES:(lt + 1) * LANES]

    def activations(hid_ref, act_ref):
        def lane_tile(lt, carry):
            cols = pl.ds(pl.multiple_of(lt * LANES, LANES), LANES)
            zero = jnp.zeros((PEER_N_KEYS, LANES), BF16)
            for r in range(rows_per_tile):
                w = None
                for h in range(PEER_HEADS):
                    sel = _bf16_row_tile(rank_ref[h, r:r + 1, cols], PEER_N_KEYS) < nsel_ref[h, lt]
                    term = jnp.where(sel, beta_ref[h, lt], zero) * _bf16_row_tile(alpha_ref[h, r:r + 1, cols],
                                                                                  PEER_N_KEYS)
                    w = term if w is None else w + term
                rows = slice(r * PEER_N_KEYS, (r + 1) * PEER_N_KEYS)
                hid_t = hid_ref[lt, rows, :]
                gelu2 = hid_t * (1.0 + lax.erf(hid_t * (2.0 ** -0.5)))
                act_ref[lt, rows, :] = gelu2.astype(BF16) * w
            return carry

        lax.fori_loop(0, lane_tiles, lane_tile, 0)

    def project(act_ref):
        act = jnp.concatenate([act_ref[lt] for lt in range(lane_tiles)], axis=1)
        o_ref[...] += jnp.dot(vt_ref[...], act, preferred_element_type=F32)

    def run(do_a, do_b, do_c):
        if do_a:
            hidden(hid_ref.at[cur])
        if do_b:
            activations(hid_ref.at[prev], act_ref.at[prev])
        if do_c:
            project(act_ref.at[cur])

    @pl.when(jnp.logical_and(j >= 2, j < n_tiles))
    def _():
        run(True, True, True)

    @pl.when(j == 0)
    def _():
        o_ref[...] = jnp.zeros(o_ref.shape, o_ref.dtype)
        run(True, False, False)

    @pl.when(j == 1)
    def _():
        run(True, True, False)

    @pl.when(j == n_tiles)
    def _():
        run(False, True, True)

    @pl.when(j == n_tiles + 1)
    def _():
        run(False, False, True)


def _peer_dense(hnt, u_bf16, v_bf16, rank1, alpha, n_sel, beta, tb, te):
    d, t = hnt.shape
    n_tiles = u_bf16.shape[0] // te
    assert n_tiles >= 2
    vt_tiles = v_bf16.reshape(n_tiles, te, d).transpose(0, 2, 1)
    kern = functools.partial(_peer_dense_kernel, te=te, tb=tb, n_tiles=n_tiles)
    last = n_tiles - 1
    rspec = pl.BlockSpec((PEER_HEADS, tb // LANES, PEER_N_KEYS, LANES), lambda i, j: (0, i, 0, 0))
    i1spec = pl.BlockSpec((PEER_HEADS, te // PEER_N_KEYS, tb), lambda i, j: (0, jnp.clip(j - 1, 0, last), i))
    stage_shape = (tb // LANES, te, LANES)
    return pl.pallas_call(
        kern,
        grid=(t // tb, n_tiles + 2),
        in_specs=[pl.BlockSpec((d, tb), lambda i, j: (0, i)),
                  pl.BlockSpec((te, d), lambda i, j: (jnp.minimum(j, last), 0)),
                  pl.BlockSpec((None, d, te), lambda i, j: (jnp.clip(j - 2, 0, last), 0, 0)),
                  i1spec, i1spec, rspec, rspec],
        out_specs=pl.BlockSpec((d, tb), lambda i, j: (0, i)),
        out_shape=jax.ShapeDtypeStruct((d, t), F32),
        scratch_shapes=[pltpu.VMEM((2,) + stage_shape, F32), pltpu.VMEM((2,) + stage_shape, BF16)],
        compiler_params=_cparams(("parallel", "arbitrary")),
        name="peer_dense",
    )(hnt, u_bf16, vt_tiles, rank1, alpha, n_sel, beta)


def _final_norm_kernel(h_ref, pt_ref, g_ref, o_ref):
    o_ref[...] = _rms(h_ref[...] + pt_ref[...].T, g_ref[...])


def _final_norm(h2, peer_t, g, tb):
    t, d = h2.shape
    return pl.pallas_call(
        _final_norm_kernel,
        grid=(t // tb,),
        in_specs=[pl.BlockSpec((tb, d), lambda i: (i, 0)),
                  pl.BlockSpec((d, tb), lambda i: (0, i)),
                  pl.BlockSpec((1, d), lambda i: (0, 0))],
        out_specs=pl.BlockSpec((tb, d), lambda i: (i, 0)),
        out_shape=jax.ShapeDtypeStruct((t, d), F32),
        compiler_params=_cparams(("parallel",)),
        name="final_norm",
    )(h2, peer_t, g)


def _largest_divisor(n, cap, multiple):
    best = multiple
    k = multiple
    while k <= min(n, cap):
        if n % k == 0:
            best = k
        k += multiple
    return best


def kernel(x, norm1_g, w_in, diff_lambda_q1, diff_lambda_k1, diff_lambda_q2, diff_lambda_k2, diff_subln_g,
           rel_bias_table, hgrn_lb_logits, hgrn_gnorm_g, w_out, norm2_g, peer_w_q, peer_sub_keys, peer_u,
           peer_v, final_norm_g):
    batch, seq, d = x.shape
    t = batch * seq
    assert d == D_MODEL and seq % 256 == 0 and w_in.shape[0] == 1
    tb_proj = _largest_divisor(t, 512, 256)
    tq = 256
    tb_route = _largest_divisor(t, 512, 256)
    tb_peer = _largest_divisor(t, 1024, 512)
    te_peer = 1024

    x2 = x.reshape(t, d)
    row = lambda v: v.reshape(1, -1).astype(F32)

    lam = (jnp.exp(jnp.sum(diff_lambda_q1[0].astype(F32) * diff_lambda_k1[0].astype(F32)))
           - jnp.exp(jnp.sum(diff_lambda_q2[0].astype(F32) * diff_lambda_k2[0].astype(F32)))
           + LAM_INIT).reshape(1)
    lb = jnp.cumsum(jax.nn.softmax(hgrn_lb_logits.astype(F32), axis=0), axis=0)[0]
    lb = lb.reshape(HGRN_HEADS, 1, HEAD_W)
    table = rel_bias_table.astype(F32)

    q_t, k, v_t, hin = _norm_proj(x2, row(norm1_g[0]), w_in[0].astype(BF16), tb_proj)
    bias = _bias_tiles(table, tq)
    a = _diff_attn(lam, table, q_t, k, v_t, bias, row(diff_subln_g[0]), batch, seq, tq)
    o = _hgrn(hin.reshape(batch, seq, HGRN_COLS), lb, row(hgrn_gnorm_g[0]), batch, seq)
    w_o = w_out[0].astype(BF16)
    h2, hnt = _out_proj(x2, a.reshape(t, DIFF_WIDTH), o.reshape(t, HGRN_WIDTH),
                        w_o[:DIFF_WIDTH], w_o[DIFF_WIDTH:], row(norm2_g[0]), tb_proj)
    rank1, alpha, n_sel, beta = _peer_route(hnt, peer_w_q[0].T.astype(BF16), peer_sub_keys[0].astype(BF16), tb_route)
    peer_t = _peer_dense(hnt, peer_u[0].astype(BF16), peer_v[0].astype(BF16), rank1, alpha, n_sel, beta,
                         tb_peer, te_peer)
    out = _final_norm(h2, peer_t, row(final_norm_g), tb_proj)
    return out.reshape(batch, seq, d)
```

```python
import functools
import math

import numpy as np
import jax
import jax.numpy as jnp
from jax import lax
from jax.experimental import pallas as pl
from jax.experimental.pallas import tpu as pltpu

F32 = jnp.float32
BF16 = jnp.bfloat16

D_MODEL = 1024
DIFF_HEADS = 4
DIFF_QK_DIM = 64
LANES = 128
MXU_WIDTH = 256
SECTION_ROWS = MXU_WIDTH
HEAD_W = 128
DIFF_WIDTH = DIFF_HEADS * HEAD_W
HGRN_HEADS = 4
HGRN_WIDTH = HGRN_HEADS * HEAD_W
HGRN_CHUNK = 32
ATTN_COLS = 3 * DIFF_WIDTH
HGRN_COLS = 4 * HGRN_WIDTH
REL_BUCKETS = 32
REL_MAX_DIST = 128
PEER_HEADS = 8
PEER_N_KEYS = 128
PEER_N_EXPERTS = PEER_N_KEYS * PEER_N_KEYS
PEER_QUERY_DIM = 256
PEER_TOPK = 16
NORM_EPS = 1e-6
LAM_INIT = 0.8 - 0.6 * math.exp(-0.3 * 0)
NEG_BIG = -1e30

VMEM_LIMIT_BYTES = 56 * 1024 * 1024


def _cparams(sem):
    return pltpu.CompilerParams(dimension_semantics=sem, vmem_limit_bytes=VMEM_LIMIT_BYTES)


def _rms(x, g):
    return x * lax.rsqrt(jnp.mean(x * x, axis=-1, keepdims=True) + NORM_EPS) * g


def _norm_proj_kernel(x_ref, g_ref, w_ref, qt_ref, k_ref, vt_ref, oh_ref):
    y = _rms(x_ref[...], g_ref[...])
    p = jnp.dot(y.astype(BF16), w_ref[...], preferred_element_type=F32)
    qt_ref[...] = p[:, :DIFF_WIDTH].T.astype(BF16)
    k_ref[...] = p[:, DIFF_WIDTH:2 * DIFF_WIDTH].astype(BF16)
    vt_ref[...] = p[:, 2 * DIFF_WIDTH:ATTN_COLS].T.astype(BF16)
    oh_ref[...] = p[:, ATTN_COLS:]


def _norm_proj(x2, g, w_bf16, tb):
    t, d = x2.shape
    n = w_bf16.shape[1]
    return pl.pallas_call(
        _norm_proj_kernel,
        grid=(t // tb,),
        in_specs=[pl.BlockSpec((tb, d), lambda i: (i, 0)),
                  pl.BlockSpec((1, d), lambda i: (0, 0)),
                  pl.BlockSpec((d, n), lambda i: (0, 0))],
        out_specs=[pl.BlockSpec((DIFF_WIDTH, tb), lambda i: (0, i)),
                   pl.BlockSpec((tb, DIFF_WIDTH), lambda i: (i, 0)),
                   pl.BlockSpec((DIFF_WIDTH, tb), lambda i: (0, i)),
                   pl.BlockSpec((tb, HGRN_COLS), lambda i: (i, 0))],
        out_shape=[jax.ShapeDtypeStruct((DIFF_WIDTH, t), BF16),
                   jax.ShapeDtypeStruct((t, DIFF_WIDTH), BF16),
                   jax.ShapeDtypeStruct((DIFF_WIDTH, t), BF16),
                   jax.ShapeDtypeStruct((t, HGRN_COLS), F32)],
        compiler_params=_cparams(("parallel",)),
        name="norm_proj",
    )(x2, g, w_bf16)


def _t5_bucket_np(n):
    n = np.maximum(n, 0).astype(np.int32)
    max_exact = REL_BUCKETS // 2
    nf = np.maximum(n, max_exact).astype(np.float32)
    large = max_exact + (np.log(nf / np.float32(max_exact)) / np.float32(math.log(REL_MAX_DIST / max_exact))
                         * np.float32(REL_BUCKETS - max_exact)).astype(np.int32)
    large = np.minimum(large, REL_BUCKETS - 1)
    return np.where(n < max_exact, n, large).astype(np.int32)


def _bias_tiles_kernel(table_ref, bucket_ref, o_ref):
    h = pl.program_id(0)
    bucket = bucket_ref[...]
    acc = jnp.zeros(bucket.shape, F32)
    for b in range(REL_BUCKETS):
        acc = jnp.where(bucket == b, table_ref[b, h], acc)
    o_ref[...] = acc


def _bias_tiles(table, tq):
    c = np.arange(tq)[:, None]
    r = np.arange(tq)[None, :]
    tiles = np.stack([_t5_bucket_np(d * tq + r - c) for d in (0, 1)])
    tiles = np.concatenate([tiles, tiles], axis=2)
    return pl.pallas_call(
        _bias_tiles_kernel,
        grid=(DIFF_HEADS,),
        in_specs=[pl.BlockSpec(memory_space=pltpu.SMEM),
                  pl.BlockSpec((2, tq, 2 * tq), lambda h: (0, 0, 0))],
        out_specs=pl.BlockSpec((None, 2, tq, 2 * tq), lambda h: (h, 0, 0, 0)),
        out_shape=jax.ShapeDtypeStruct((DIFF_HEADS, 2, tq, 2 * tq), F32),
        compiler_params=_cparams(("arbitrary",)),
        name="bias_tiles",
    )(table, jnp.asarray(tiles))


def _diff_attn_kernel(lam_ref, table_ref, qt_ref, k_ref, vt_ref, bias_ref, g_ref, o_ref,
                      s_ref, mx_ref, l_ref, acc_ref, *, tq):
    h = pl.program_id(1)
    i = pl.program_id(2)
    qt = qt_ref[...] * jnp.asarray(DIFF_QK_DIM ** -0.5, BF16)
    row = lax.broadcasted_iota(jnp.int32, qt.shape, 0)
    zero = jnp.zeros_like(qt)
    q2t = jnp.concatenate([jnp.where(row < DIFF_QK_DIM, qt, zero),
                           jnp.where(row >= DIFF_QK_DIM, qt, zero)], axis=1)
    far_bias = table_ref[REL_BUCKETS - 1, h]

    def fold(op, x):
        return functools.reduce(op, [x[SUBLANES * k:SUBLANES * (k + 1)] for k in range(x.shape[0] // SUBLANES)])

    def scores(j, n_keys, bias, causal):
        start = pl.multiple_of(j * tq, tq)
        s = jnp.dot(k_ref[pl.ds(start, n_keys), :], q2t, preferred_element_type=F32) + bias
        if causal:
            c = lax.broadcasted_iota(jnp.int32, s.shape, 0)
            r = lax.broadcasted_iota(jnp.int32, s.shape, 1) % tq
            s = jnp.where(c <= r, s, NEG_BIG)
        for k in range(n_keys // tq):
            s_ref[j + k] = s[k * tq:(k + 1) * tq]
        mx_ref[...] = jnp.maximum(mx_ref[...], fold(jnp.maximum, s))

    mx_ref[...] = jnp.full(mx_ref.shape, NEG_BIG, F32)
    l_ref[...] = jnp.zeros(l_ref.shape, F32)
    acc_ref[...] = jnp.zeros(acc_ref.shape, F32)

    def far_pair(p, carry):
        scores(2 * p, 2 * tq, far_bias, False)
        return carry

    n_far = jnp.maximum(i - 1, 0)
    lax.fori_loop(0, n_far // 2, far_pair, 0)

    @pl.when(n_far % 2 == 1)
    def _():
        scores(n_far - 1, tq, far_bias, False)

    @pl.when(i >= 1)
    def _():
        scores(i - 1, tq, bias_ref[1], False)

    scores(i, tq, bias_ref[0], True)
    m = jnp.max(mx_ref[...], axis=0, keepdims=True)

    def weigh(j, n_keys):
        start = pl.multiple_of(j * tq, tq)
        s = jnp.concatenate([s_ref[j + k] for k in range(n_keys // tq)], axis=0)
        e = jnp.exp(s - m)
        l_ref[...] += fold(jnp.add, e)
        acc_ref[...] += jnp.dot(vt_ref[:, pl.ds(start, n_keys)], e.astype(BF16), preferred_element_type=F32)

    def weigh_pair(p, carry):
        weigh(2 * p, 2 * tq)
        return carry

    n_blocks = i + 1
    lax.fori_loop(0, n_blocks // 2, weigh_pair, 0)

    @pl.when(n_blocks % 2 == 1)
    def _():
        weigh(i, tq)

    l = jnp.sum(l_ref[...], axis=0, keepdims=True)
    acc = acc_ref[...]
    lam = lam_ref[0]
    out_t = acc[:, :tq] / l[:, :tq] - lam * (acc[:, tq:] / l[:, tq:])
    o_ref[...] = (_rms(out_t.T, g_ref[...]) * (1.0 - LAM_INIT)).astype(o_ref.dtype)


def _diff_attn(lam, table, q_t, k, v_t, bias, g, batch, seq, tq):
    kern = functools.partial(_diff_attn_kernel, tq=tq)
    nq = seq // tq
    return pl.pallas_call(
        kern,
        grid=(batch, DIFF_HEADS, nq),
        in_specs=[pl.BlockSpec(memory_space=pltpu.SMEM),
                  pl.BlockSpec(memory_space=pltpu.SMEM),
                  pl.BlockSpec((HEAD_W, tq), lambda b, h, i: (h, b * nq + i)),
                  pl.BlockSpec((seq, HEAD_W), lambda b, h, i: (b, h)),
                  pl.BlockSpec((HEAD_W, seq), lambda b, h, i: (h, b)),
                  pl.BlockSpec((None, 2, tq, 2 * tq), lambda b, h, i: (h, 0, 0, 0)),
                  pl.BlockSpec((1, HEAD_W), lambda b, h, i: (0, 0))],
        out_specs=pl.BlockSpec((None, tq, HEAD_W), lambda b, h, i: (b, i, h)),
        out_shape=jax.ShapeDtypeStruct((batch, seq, DIFF_WIDTH), BF16),
        scratch_shapes=[pltpu.VMEM((seq // tq, tq, 2 * tq), F32),
                        pltpu.VMEM((SUBLANES, 2 * tq), F32),
                        pltpu.VMEM((SUBLANES, 2 * tq), F32),
                        pltpu.VMEM((HEAD_W, 2 * tq), F32)],
        compiler_params=_cparams(("parallel", "parallel", "arbitrary")),
        name="diff_attn",
    )(lam, table, q_t, k, v_t, bias, g)


def _hgrn_kernel(hq_ref, hf_ref, hi_ref, hg_ref, lb_ref, g_ref, o_ref,
                 qd_ref, kd_ref, cd_ref, oi_ref, upd_ref, prev_ref, *, seq):
    c_len = HGRN_CHUNK
    n_chunks = seq // c_len
    lb = lb_ref[...]
    f = lb + (1.0 - lb) * jax.nn.sigmoid(hf_ref[...])
    log_f = jnp.log(f)
    kk = 1.0 - f
    hq = hq_ref[...]
    qq = hq * jax.nn.sigmoid(hq)
    vv = hi_ref[...]

    pos = lax.broadcasted_iota(jnp.int32, (seq, HEAD_W), 0) % c_len
    b = log_f
    sh = 1
    while sh < c_len:
        b = b + jnp.where(pos >= sh, pltpu.roll(b, sh, axis=0), 0.0)
        sh *= 2
    b3 = b.reshape(n_chunks, c_len, HEAD_W)
    b_last = b3[:, c_len - 1:c_len, :]
    q_dec = qq * jnp.exp(b)
    k_inv = kk * jnp.exp(-b)
    k_dec = kk * jnp.exp(b_last - b3).reshape(seq, HEAD_W)
    qd_ref[...] = q_dec.astype(BF16)
    kd_ref[...] = k_dec.astype(BF16)
    cd_ref[...] = jnp.exp(b_last.reshape(n_chunks, HEAD_W))

    blk = HEAD_W
    nb = seq // blk
    qd_b = q_dec.astype(BF16).reshape(nb, blk, HEAD_W)
    ki_b = k_inv.astype(BF16).reshape(nb, blk, HEAD_W)
    scores = jnp.einsum('nqd,nkd->nqk', qd_b, ki_b, preferred_element_type=F32)
    r = lax.broadcasted_iota(jnp.int32, (blk, blk), 0)
    c = lax.broadcasted_iota(jnp.int32, (blk, blk), 1)
    keep = (c <= r) & ((r // c_len) == (c // c_len))
    scores = jnp.where(keep[None], scores, 0.0)
    o_intra = jnp.einsum('nqk,nkv->nqv', scores.astype(BF16), vv.astype(BF16).reshape(nb, blk, HEAD_W),
                         preferred_element_type=F32).reshape(seq, HEAD_W)

    unroll = 8

    def rows_of(ci):
        return pl.ds(pl.multiple_of(ci * c_len, c_len), c_len)

    def updates(g, carry):
        for u in range(unroll):
            ci = g * unroll + u
            vc = hi_ref[rows_of(ci), :].astype(BF16)
            upd_ref[ci] = lax.dot_general(vc, kd_ref[rows_of(ci), :], (((0,), (0,)), ((), ())),
                                          preferred_element_type=F32)
        return carry

    lax.fori_loop(0, n_chunks // unroll, updates, 0)

    def recur(ci, st):
        prev_ref[ci] = st.astype(BF16)
        return st * cd_ref[pl.ds(ci, 1), :] + upd_ref[ci]

    lax.fori_loop(0, n_chunks, recur, jnp.zeros((HEAD_W, HEAD_W), F32))

    def reads(g, carry):
        for u in range(unroll):
            ci = g * unroll + u
            oi_ref[rows_of(ci), :] = lax.dot_general(qd_ref[rows_of(ci), :], prev_ref[ci],
                                                     (((1,), (1,)), ((), ())), preferred_element_type=F32)
        return carry

    lax.fori_loop(0, n_chunks // unroll, reads, 0)

    o = o_intra + oi_ref[...]
    hg = hg_ref[...]
    o_ref[...] = (_rms(o, g_ref[...]) * (hg * jax.nn.sigmoid(hg))).astype(o_ref.dtype)


def _hgrn(hin, lb, g, batch, seq):
    kern = functools.partial(_hgrn_kernel, seq=seq)
    col = lambda k: (lambda b, h: (b, 0, k * HGRN_HEADS + h))
    blk = (None, seq, HEAD_W)
    return pl.pallas_call(
        kern,
        grid=(batch, HGRN_HEADS),
        in_specs=[pl.BlockSpec(blk, col(0)), pl.BlockSpec(blk, col(1)),
                  pl.BlockSpec(blk, col(2)), pl.BlockSpec(blk, col(3)),
                  pl.BlockSpec((None, 1, HEAD_W), lambda b, h: (h, 0, 0)),
                  pl.BlockSpec((1, HEAD_W), lambda b, h: (0, 0))],
        out_specs=pl.BlockSpec(blk, lambda b, h: (b, 0, h)),
        out_shape=jax.ShapeDtypeStruct((batch, seq, HGRN_WIDTH), BF16),
        scratch_shapes=[pltpu.VMEM((seq, HEAD_W), BF16), pltpu.VMEM((seq, HEAD_W), BF16),
                        pltpu.VMEM((seq // HGRN_CHUNK, HEAD_W), F32), pltpu.VMEM((seq, HEAD_W), F32),
                        pltpu.VMEM((seq // HGRN_CHUNK, HEAD_W, HEAD_W), F32),
                        pltpu.VMEM((seq // HGRN_CHUNK, HEAD_W, HEAD_W), BF16)],
        compiler_params=_cparams(("parallel", "parallel")),
        name="hgrn",
    )(hin, hin, hin, hin, lb, g)


def _out_proj_kernel(x_ref, a_ref, o_ref, wa_ref, wo_ref, g_ref, h_ref, hnt_ref):
    h = (x_ref[...]
         + jnp.dot(a_ref[...], wa_ref[...], preferred_element_type=F32)
         + jnp.dot(o_ref[...], wo_ref[...], preferred_element_type=F32))
    h_ref[...] = h
    hnt_ref[...] = _rms(h, g_ref[...]).T.astype(BF16)


def _out_proj(x2, a2, o2, wa, wo, g, tb):
    t, d = x2.shape
    return pl.pallas_call(
        _out_proj_kernel,
        grid=(t // tb,),
        in_specs=[pl.BlockSpec((tb, d), lambda i: (i, 0)),
                  pl.BlockSpec((tb, DIFF_WIDTH), lambda i: (i, 0)),
                  pl.BlockSpec((tb, HGRN_WIDTH), lambda i: (i, 0)),
                  pl.BlockSpec((DIFF_WIDTH, d), lambda i: (0, 0)),
                  pl.BlockSpec((HGRN_WIDTH, d), lambda i: (0, 0)),
                  pl.BlockSpec((1, d), lambda i: (0, 0))],
        out_specs=[pl.BlockSpec((tb, d), lambda i: (i, 0)),
                   pl.BlockSpec((d, tb), lambda i: (0, i))],
        out_shape=[jax.ShapeDtypeStruct((t, d), F32),
                   jax.ShapeDtypeStruct((d, t), BF16)],
        compiler_params=_cparams(("parallel",)),
        name="out_proj",
    )(x2, a2, o2, wa, wo, g)


SUBLANES = 8
TAU_GUARD = 2.0 ** -21


def _oddeven_merge(lo, hi, r):
    step = r * 2
    if step < hi - lo:
        yield from _oddeven_merge(lo, hi, step)
        yield from _oddeven_merge(lo + r, hi, step)
        yield from [(i, i + r) for i in range(lo + r, hi - r, step)]
    else:
        yield (lo, lo + r)


def _oddeven_merge_sort(lo, hi):
    if hi - lo >= 1:
        mid = lo + (hi - lo) // 2
        yield from _oddeven_merge_sort(lo, mid)
        yield from _oddeven_merge_sort(mid + 1, hi)
        yield from _oddeven_merge(lo, hi, 1)


def _bitonic_merge(n):
    d = n // 2
    while d >= 1:
        yield from [(i, i + d) for i in range(n) if (i // d) % 2 == 0]
        d //= 2


SORT16 = tuple(_oddeven_merge_sort(0, PEER_TOPK - 1))
MERGE16 = tuple(_bitonic_merge(PEER_TOPK))


def _bf16_row_tile(row, n_rows):
    packed_rows = 16
    tile16 = jnp.broadcast_to(row, (packed_rows, row.shape[1])).astype(BF16)
    return jnp.broadcast_to(tile16[None], (n_rows // packed_rows, packed_rows, row.shape[1])
                            ).reshape(n_rows, row.shape[1])


def _compare_exchange(v, pairs):
    for i, j in pairs:
        v[i], v[j] = jnp.maximum(v[i], v[j]), jnp.minimum(v[i], v[j])


def _top16_sorted(x):
    v = [x[SUBLANES * k:SUBLANES * (k + 1)] for k in range(PEER_TOPK)]
    _compare_exchange(v, SORT16)
    shift = SUBLANES // 2
    while shift >= 1:
        v = [jnp.maximum(v[k], pltpu.roll(v[PEER_TOPK - 1 - k], shift, axis=0)) for k in range(PEER_TOPK)]
        _compare_exchange(v, MERGE16)
        shift //= 2
    return v


def _rows_on_sublanes(rep, start):
    sub = lax.broadcasted_iota(jnp.int32, rep[0].shape, 0)
    out = rep[start]
    for s in range(1, SUBLANES):
        out = jnp.where(sub == s, rep[start + s], out)
    return out


def _pair_candidates(a_rep, b_rep):
    k = PEER_TOPK
    sub = lax.broadcasted_iota(jnp.int32, a_rep[0].shape, 0)
    b_lo = _rows_on_sublanes(b_rep, 0)
    b_hi = _rows_on_sublanes(b_rep, SUBLANES)
    a_hi = _rows_on_sublanes(a_rep, SUBLANES)
    pieces = [a_rep[0] + b_lo, a_rep[0] + b_hi, a_rep[1] + b_lo]
    for i in range(2, SUBLANES):
        pieces.append(jnp.where(sub < k // (i + 1), a_rep[i] + b_lo, -jnp.inf))
    pieces.append(a_hi + b_rep[0])
    return jnp.concatenate(pieces, axis=0)


def _kth_largest(x, k):
    for r in range(k):
        m = jnp.max(x, axis=0, keepdims=True)
        if r + 1 < k:
            x = jnp.where(x == m, -jnp.inf, x)
    return m


def _peer_route_kernel(hnt_ref, wqt_ref, keys_ref, rank_ref, alpha_ref, nsel_ref, beta_ref, qt_ref, s_ref):
    qt_ref[...] = jnp.dot(wqt_ref[...], hnt_ref[...], preferred_element_type=F32).astype(BF16)
    half = PEER_QUERY_DIM // 2
    lane_tiles = hnt_ref.shape[1] // LANES
    k = PEER_TOPK

    def head(h, carry):
        base = pl.multiple_of(h * PEER_QUERY_DIM, PEER_QUERY_DIM)
        for c in range(2):
            s = jnp.dot(keys_ref[h, c], qt_ref[pl.ds(base + c * half, half), :], preferred_element_type=F32)
            for lt in range(lane_tiles):
                s_ref[c, lt] = s[:, lt * LANES:(lt + 1) * LANES]

        def tile(lt, carry):
            cols = pl.ds(pl.multiple_of(lt * LANES, LANES), LANES)
            s1 = s_ref[0, lt]
            s2 = s_ref[1, lt]
            a_rep = _top16_sorted(s1)
            b_rep = _top16_sorted(s2)
            cands = _pair_candidates(a_rep, b_rep)
            thr = _kth_largest(cands, k)
            top = a_rep[0][0:1] + b_rep[0][0:1]
            z = jnp.sum(jnp.where(cands >= thr, jnp.exp(cands - top), 0.0), axis=0, keepdims=True)
            rank1 = []
            n_sel = []
            theta = [(thr - a_rep[j]) - (jnp.abs(thr) + jnp.abs(a_rep[j])) * TAU_GUARD for j in range(k)]
            for v in range(PEER_N_KEYS // SUBLANES):
                x1 = s1[SUBLANES * v:SUBLANES * (v + 1)]
                x2 = s2[SUBLANES * v:SUBLANES * (v + 1)]
                r1 = jnp.zeros_like(x1)
                ns = jnp.zeros_like(x2)
                for j in range(k):
                    r1 = jnp.where(a_rep[j] > x1, float(j + 1), r1)
                    ns = jnp.where(x2 >= theta[j], float(j + 1), ns)
                rank1.append(r1)
                n_sel.append(ns)
            rank_ref[h, :, cols] = jnp.concatenate(rank1, axis=0)
            alpha_ref[h, :, cols] = jnp.exp(s1 - a_rep[0][0:1]) * (0.5 / z)
            nsel_ref[h, lt] = jnp.concatenate(n_sel, axis=0).astype(BF16)
            beta_ref[h, lt] = jnp.exp(s2 - b_rep[0][0:1]).astype(BF16)
            return carry

        return lax.fori_loop(0, lane_tiles, tile, carry)

    lax.fori_loop(0, PEER_HEADS, head, 0)


def _peer_route(hnt, wqt, keys, tb):
    d, t = hnt.shape
    nq = wqt.shape[0]
    oshape = jax.ShapeDtypeStruct((PEER_HEADS, PEER_N_KEYS, t), F32)
    ospec = pl.BlockSpec((PEER_HEADS, PEER_N_KEYS, tb), lambda i: (0, 0, i))
    tshape = jax.ShapeDtypeStruct((PEER_HEADS, t // LANES, PEER_N_KEYS, LANES), BF16)
    tspec = pl.BlockSpec((PEER_HEADS, tb // LANES, PEER_N_KEYS, LANES), lambda i: (0, i, 0, 0))
    return pl.pallas_call(
        _peer_route_kernel,
        grid=(t // tb,),
        in_specs=[pl.BlockSpec((d, tb), lambda i: (0, i)),
                  pl.BlockSpec((nq, d), lambda i: (0, 0)),
                  pl.BlockSpec(keys.shape, lambda i: (0, 0, 0, 0))],
        out_specs=[ospec, ospec, tspec, tspec],
        out_shape=[oshape, oshape, tshape, tshape],
        scratch_shapes=[pltpu.VMEM((nq, tb), BF16), pltpu.VMEM((2, tb // LANES, PEER_N_KEYS, LANES), F32)],
        compiler_params=_cparams(("parallel",)),
        name="peer_route",
    )(hnt, wqt, keys)


def _peer_dense_kernel(hnt_ref, u_ref, vt_ref, rank_ref, alpha_ref, nsel_ref, beta_ref, o_ref,
                       hid_ref, act_ref, *, te, tb):
    j = pl.program_id(1)
    rows_per_tile = te // PEER_N_KEYS
    group_w = 2 * MXU_WIDTH
    tiles_per_group = group_w // LANES

    @pl.when(j == 0)
    def _():
        o_ref[...] = jnp.zeros(o_ref.shape, o_ref.dtype)

    def group(g, carry):
        gcols = pl.ds(pl.multiple_of(g * group_w, group_w), group_w)
        hid = jnp.dot(u_ref[...], hnt_ref[:, gcols], preferred_element_type=F32)
        for q in range(tiles_per_group):
            hid_ref[q] = hid[:, q * LANES:(q + 1) * LANES]

        def lane_tile(q, carry):
            lt = g * tiles_per_group + q
            cols = pl.ds(pl.multiple_of(lt * LANES, LANES), LANES)
            zero = jnp.zeros((PEER_N_KEYS, LANES), BF16)
            for r in range(rows_per_tile):
                w = None
                for h in range(PEER_HEADS):
                    sel = _bf16_row_tile(rank_ref[h, r:r + 1, cols], PEER_N_KEYS) < nsel_ref[h, lt]
                    term = jnp.where(sel, beta_ref[h, lt], zero) * _bf16_row_tile(alpha_ref[h, r:r + 1, cols],
                                                                                  PEER_N_KEYS)
                    w = term if w is None else w + term
                rows = slice(r * PEER_N_KEYS, (r + 1) * PEER_N_KEYS)
                hid_t = hid_ref[q, rows, :]
                gelu2 = hid_t * (1.0 + lax.erf(hid_t * (2.0 ** -0.5)))
                act_ref[q, rows, :] = gelu2.astype(BF16) * w
            return carry

        lax.fori_loop(0, tiles_per_group, lane_tile, 0)
        act = jnp.concatenate([act_ref[q] for q in range(tiles_per_group)], axis=1)
        o_ref[:, gcols] += jnp.dot(vt_ref[...], act, preferred_element_type=F32)
        return carry

    lax.fori_loop(0, tb // group_w, group, 0)


def _peer_dense(hnt, u_bf16, v_bf16, rank1, alpha, n_sel, beta, tb, te):
    d, t = hnt.shape
    n_tiles = u_bf16.shape[0] // te
    assert tb % (2 * MXU_WIDTH) == 0
    vt_tiles = v_bf16.reshape(n_tiles, te, d).transpose(0, 2, 1)
    kern = functools.partial(_peer_dense_kernel, te=te, tb=tb)
    rspec = pl.BlockSpec((PEER_HEADS, tb // LANES, PEER_N_KEYS, LANES), lambda i, j: (0, i, 0, 0))
    i1spec = pl.BlockSpec((PEER_HEADS, te // PEER_N_KEYS, tb), lambda i, j: (0, j, i))
    stage_shape = (2 * MXU_WIDTH // LANES, te, LANES)
    return pl.pallas_call(
        kern,
        grid=(t // tb, n_tiles),
        in_specs=[pl.BlockSpec((d, tb), lambda i, j: (0, i)),
                  pl.BlockSpec((te, d), lambda i, j: (j, 0)),
                  pl.BlockSpec((None, d, te), lambda i, j: (j, 0, 0)),
                  i1spec, i1spec, rspec, rspec],
        out_specs=pl.BlockSpec((d, tb), lambda i, j: (0, i)),
        out_shape=jax.ShapeDtypeStruct((d, t), F32),
        scratch_shapes=[pltpu.VMEM(stage_shape, F32), pltpu.VMEM(stage_shape, BF16)],
        compiler_params=_cparams(("parallel", "arbitrary")),
        name="peer_dense",
    )(hnt, u_bf16, vt_tiles, rank1, alpha, n_sel, beta)


def _final_norm_kernel(h_ref, pt_ref, g_ref, o_ref):
    o_ref[...] = _rms(h_ref[...] + pt_ref[...].T, g_ref[...])


def _final_norm(h2, peer_t, g, tb):
    t, d = h2.shape
    return pl.pallas_call(
        _final_norm_kernel,
        grid=(t // tb,),
        in_specs=[pl.BlockSpec((tb, d), lambda i: (i, 0)),
                  pl.BlockSpec((d, tb), lambda i: (0, i)),
                  pl.BlockSpec((1, d), lambda i: (0, 0))],
        out_specs=pl.BlockSpec((tb, d), lambda i: (i, 0)),
        out_shape=jax.ShapeDtypeStruct((t, d), F32),
        compiler_params=_cparams(("parallel",)),
        name="final_norm",
    )(h2, peer_t, g)


def _largest_divisor(n, cap, multiple):
    best = multiple
    k = multiple
    while k <= min(n, cap):
        if n % k == 0:
            best = k
        k += multiple
    return best


def kernel(x, norm1_g, w_in, diff_lambda_q1, diff_lambda_k1, diff_lambda_q2, diff_lambda_k2, diff_subln_g,
           rel_bias_table, hgrn_lb_logits, hgrn_gnorm_g, w_out, norm2_g, peer_w_q, peer_sub_keys, peer_u,
           peer_v, final_norm_g):
    batch, seq, d = x.shape
    t = batch * seq
    assert d == D_MODEL and seq % 256 == 0 and w_in.shape[0] == 1
    tb_proj = _largest_divisor(t, 512, 256)
    tq = 256
    tb_route = _largest_divisor(t, 512, 256)
    tb_peer = _largest_divisor(t, 1024, 512)
    te_peer = 1024

    x2 = x.reshape(t, d)
    row = lambda v: v.reshape(1, -1).astype(F32)

    lam = (jnp.exp(jnp.sum(diff_lambda_q1[0].astype(F32) * diff_lambda_k1[0].astype(F32)))
           - jnp.exp(jnp.sum(diff_lambda_q2[0].astype(F32) * diff_lambda_k2[0].astype(F32)))
           + LAM_INIT).reshape(1)
    lb = jnp.cumsum(jax.nn.softmax(hgrn_lb_logits.astype(F32), axis=0), axis=0)[0]
    lb = lb.reshape(HGRN_HEADS, 1, HEAD_W)
    table = rel_bias_table.astype(F32)

    q_t, k, v_t, hin = _norm_proj(x2, row(norm1_g[0]), w_in[0].astype(BF16), tb_proj)
    bias = _bias_tiles(table, tq)
    a = _diff_attn(lam, table, q_t, k, v_t, bias, row(diff_subln_g[0]), batch, seq, tq)
    o = _hgrn(hin.reshape(batch, seq, HGRN_COLS), lb, row(hgrn_gnorm_g[0]), batch, seq)
    w_o = w_out[0].astype(BF16)
    h2, hnt = _out_proj(x2, a.reshape(t, DIFF_WIDTH), o.reshape(t, HGRN_WIDTH),
                        w_o[:DIFF_WIDTH], w_o[DIFF_WIDTH:], row(norm2_g[0]), tb_proj)
    rank1, alpha, n_sel, beta = _peer_route(hnt, peer_w_q[0].T.astype(BF16), peer_sub_keys[0].astype(BF16), tb_route)
    peer_t = _peer_dense(hnt, peer_u[0].astype(BF16), peer_v[0].astype(BF16), rank1, alpha, n_sel, beta,
                         tb_peer, te_peer)
    out = _final_norm(h2, peer_t, row(final_norm_g), tb_proj)
    return out.reshape(batch, seq, d)
```

```python
import functools
import math

import numpy as np
import jax
import jax.numpy as jnp
from jax import lax
from jax.experimental import pallas as pl
from jax.experimental.pallas import tpu as pltpu

F32 = jnp.float32
BF16 = jnp.bfloat16

D_MODEL = 1024
DIFF_HEADS = 4
DIFF_QK_DIM = 64
LANES = 128
MXU_WIDTH = 256
SECTION_ROWS = MXU_WIDTH
HEAD_W = 128
DIFF_WIDTH = DIFF_HEADS * HEAD_W
HGRN_HEADS = 4
HGRN_WIDTH = HGRN_HEADS * HEAD_W
HGRN_CHUNK = 32
ATTN_COLS = 3 * DIFF_WIDTH
HGRN_COLS = 4 * HGRN_WIDTH
REL_BUCKETS = 32
REL_MAX_DIST = 128
PEER_HEADS = 8
PEER_N_KEYS = 128
PEER_N_EXPERTS = PEER_N_KEYS * PEER_N_KEYS
PEER_QUERY_DIM = 256
PEER_TOPK = 16
NORM_EPS = 1e-6
LAM_INIT = 0.8 - 0.6 * math.exp(-0.3 * 0)
NEG_BIG = -1e30

VMEM_LIMIT_BYTES = 56 * 1024 * 1024


def _cparams(sem):
    return pltpu.CompilerParams(dimension_semantics=sem, vmem_limit_bytes=VMEM_LIMIT_BYTES)


def _rms(x, g):
    return x * lax.rsqrt(jnp.mean(x * x, axis=-1, keepdims=True) + NORM_EPS) * g


def _norm_proj_kernel(x_ref, g_ref, w_ref, qt_ref, k_ref, vt_ref, oh_ref):
    y = _rms(x_ref[...], g_ref[...])
    p = jnp.dot(y.astype(BF16), w_ref[...], preferred_element_type=F32)
    qt_ref[...] = p[:, :DIFF_WIDTH].T.astype(BF16)
    k_ref[...] = p[:, DIFF_WIDTH:2 * DIFF_WIDTH].astype(BF16)
    vt_ref[...] = p[:, 2 * DIFF_WIDTH:ATTN_COLS].T.astype(BF16)
    oh_ref[...] = p[:, ATTN_COLS:]


def _norm_proj(x2, g, w_bf16, tb):
    t, d = x2.shape
    n = w_bf16.shape[1]
    return pl.pallas_call(
        _norm_proj_kernel,
        grid=(t // tb,),
        in_specs=[pl.BlockSpec((tb, d), lambda i: (i, 0)),
                  pl.BlockSpec((1, d), lambda i: (0, 0)),
                  pl.BlockSpec((d, n), lambda i: (0, 0))],
        out_specs=[pl.BlockSpec((DIFF_WIDTH, tb), lambda i: (0, i)),
                   pl.BlockSpec((tb, DIFF_WIDTH), lambda i: (i, 0)),
                   pl.BlockSpec((DIFF_WIDTH, tb), lambda i: (0, i)),
                   pl.BlockSpec((tb, HGRN_COLS), lambda i: (i, 0))],
        out_shape=[jax.ShapeDtypeStruct((DIFF_WIDTH, t), BF16),
                   jax.ShapeDtypeStruct((t, DIFF_WIDTH), BF16),
                   jax.ShapeDtypeStruct((DIFF_WIDTH, t), BF16),
                   jax.ShapeDtypeStruct((t, HGRN_COLS), F32)],
        compiler_params=_cparams(("parallel",)),
        name="norm_proj",
    )(x2, g, w_bf16)


def _t5_bucket_np(n):
    n = np.maximum(n, 0).astype(np.int32)
    max_exact = REL_BUCKETS // 2
    nf = np.maximum(n, max_exact).astype(np.float32)
    large = max_exact + (np.log(nf / np.float32(max_exact)) / np.float32(math.log(REL_MAX_DIST / max_exact))
                         * np.float32(REL_BUCKETS - max_exact)).astype(np.int32)
    large = np.minimum(large, REL_BUCKETS - 1)
    return np.where(n < max_exact, n, large).astype(np.int32)


def _bias_tiles_kernel(table_ref, bucket_ref, o_ref):
    h = pl.program_id(0)
    bucket = bucket_ref[...]
    acc = jnp.zeros(bucket.shape, F32)
    for b in range(REL_BUCKETS):
        acc = jnp.where(bucket == b, table_ref[b, h], acc)
    o_ref[...] = acc


def _bias_tiles(table, tq):
    c = np.arange(tq)[:, None]
    r = np.arange(tq)[None, :]
    tiles = np.stack([_t5_bucket_np(d * tq + r - c) for d in (0, 1)])
    tiles = np.concatenate([tiles, tiles], axis=2)
    return pl.pallas_call(
        _bias_tiles_kernel,
        grid=(DIFF_HEADS,),
        in_specs=[pl.BlockSpec(memory_space=pltpu.SMEM),
                  pl.BlockSpec((2, tq, 2 * tq), lambda h: (0, 0, 0))],
        out_specs=pl.BlockSpec((None, 2, tq, 2 * tq), lambda h: (h, 0, 0, 0)),
        out_shape=jax.ShapeDtypeStruct((DIFF_HEADS, 2, tq, 2 * tq), F32),
        compiler_params=_cparams(("arbitrary",)),
        name="bias_tiles",
    )(table, jnp.asarray(tiles))


def _diff_attn_kernel(lam_ref, table_ref, qt_ref, k_ref, vt_ref, bias_ref, g_ref, o_ref,
                      s_ref, mx_ref, l_ref, acc_ref, *, tq):
    h0 = pl.program_id(1) * ATTN_HEADS_PER_STEP
    i = pl.program_id(2)
    heads = range(ATTN_HEADS_PER_STEP)
    hrows = lambda hh: slice(hh * HEAD_W, (hh + 1) * HEAD_W)
    q2t = []
    for hh in heads:
        qt = qt_ref[hrows(hh), :] * jnp.asarray(DIFF_QK_DIM ** -0.5, BF16)
        row = lax.broadcasted_iota(jnp.int32, qt.shape, 0)
        zero = jnp.zeros_like(qt)
        q2t.append(jnp.concatenate([jnp.where(row < DIFF_QK_DIM, qt, zero),
                                    jnp.where(row >= DIFF_QK_DIM, qt, zero)], axis=1))

    def fold(op, x):
        return functools.reduce(op, [x[SUBLANES * k:SUBLANES * (k + 1)] for k in range(x.shape[0] // SUBLANES)])

    def scores(j, n_keys, delta, causal):
        start = pl.multiple_of(j * tq, tq)
        for hh in heads:
            bias = table_ref[REL_BUCKETS - 1, h0 + hh] if delta is None else bias_ref[hh, delta]
            s = jnp.dot(k_ref[pl.ds(start, n_keys), hrows(hh)], q2t[hh], preferred_element_type=F32) + bias
            if causal:
                c = lax.broadcasted_iota(jnp.int32, s.shape, 0)
                r = lax.broadcasted_iota(jnp.int32, s.shape, 1) % tq
                s = jnp.where(c <= r, s, NEG_BIG)
            for k in range(n_keys // tq):
                s_ref[hh, j + k] = s[k * tq:(k + 1) * tq]
            mx_ref[hh] = jnp.maximum(mx_ref[hh], fold(jnp.maximum, s))

    mx_ref[...] = jnp.full(mx_ref.shape, NEG_BIG, F32)
    l_ref[...] = jnp.zeros(l_ref.shape, F32)
    acc_ref[...] = jnp.zeros(acc_ref.shape, F32)

    def far_pair(p, carry):
        scores(2 * p, 2 * tq, None, False)
        return carry

    n_far = jnp.maximum(i - 1, 0)
    lax.fori_loop(0, n_far // 2, far_pair, 0)

    @pl.when(n_far % 2 == 1)
    def _():
        scores(n_far - 1, tq, None, False)

    @pl.when(i >= 1)
    def _():
        scores(i - 1, tq, 1, False)

    scores(i, tq, 0, True)
    m = [jnp.max(mx_ref[hh], axis=0, keepdims=True) for hh in heads]

    def weigh(j, n_keys):
        start = pl.multiple_of(j * tq, tq)
        for hh in heads:
            s = jnp.concatenate([s_ref[hh, j + k] for k in range(n_keys // tq)], axis=0)
            e = jnp.exp(s - m[hh])
            l_ref[hh] += fold(jnp.add, e)
            acc_ref[hh] += jnp.dot(vt_ref[hrows(hh), pl.ds(start, n_keys)], e.astype(BF16),
                                   preferred_element_type=F32)

    def weigh_pair(p, carry):
        weigh(2 * p, 2 * tq)
        return carry

    n_blocks = i + 1
    lax.fori_loop(0, n_blocks // 2, weigh_pair, 0)

    @pl.when(n_blocks % 2 == 1)
    def _():
        weigh(i, tq)

    lam = lam_ref[0]
    for hh in heads:
        l = jnp.sum(l_ref[hh], axis=0, keepdims=True)
        acc = acc_ref[hh]
        out_t = acc[:, :tq] / l[:, :tq] - lam * (acc[:, tq:] / l[:, tq:])
        o_ref[:, hrows(hh)] = (_rms(out_t.T, g_ref[...]) * (1.0 - LAM_INIT)).astype(o_ref.dtype)


def _diff_attn(lam, table, q_t, k, v_t, bias, g, batch, seq, tq):
    kern = functools.partial(_diff_attn_kernel, tq=tq)
    nq = seq // tq
    hp = ATTN_HEADS_PER_STEP
    hw = hp * HEAD_W
    return pl.pallas_call(
        kern,
        grid=(batch, DIFF_HEADS // hp, nq),
        in_specs=[pl.BlockSpec(memory_space=pltpu.SMEM),
                  pl.BlockSpec(memory_space=pltpu.SMEM),
                  pl.BlockSpec((hw, tq), lambda b, h, i: (h, b * nq + i)),
                  pl.BlockSpec((seq, hw), lambda b, h, i: (b, h)),
                  pl.BlockSpec((hw, seq), lambda b, h, i: (h, b)),
                  pl.BlockSpec((hp, 2, tq, 2 * tq), lambda b, h, i: (h, 0, 0, 0)),
                  pl.BlockSpec((1, HEAD_W), lambda b, h, i: (0, 0))],
        out_specs=pl.BlockSpec((None, tq, hw), lambda b, h, i: (b, i, h)),
        out_shape=jax.ShapeDtypeStruct((batch, seq, DIFF_WIDTH), BF16),
        scratch_shapes=[pltpu.VMEM((hp, seq // tq, tq, 2 * tq), F32),
                        pltpu.VMEM((hp, SUBLANES, 2 * tq), F32),
                        pltpu.VMEM((hp, SUBLANES, 2 * tq), F32),
                        pltpu.VMEM((hp, HEAD_W, 2 * tq), F32)],
        compiler_params=_cparams(("parallel", "parallel", "arbitrary")),
        name="diff_attn",
    )(lam, table, q_t, k, v_t, bias, g)


def _hgrn_kernel(hq_ref, hf_ref, hi_ref, hg_ref, lb_ref, g_ref, o_ref,
                 qd_ref, kd_ref, cd_ref, oi_ref, upd_ref, prev_ref, *, seq):
    c_len = HGRN_CHUNK
    n_chunks = seq // c_len
    lb = lb_ref[...]
    f = lb + (1.0 - lb) * jax.nn.sigmoid(hf_ref[...])
    log_f = jnp.log(f)
    kk = 1.0 - f
    hq = hq_ref[...]
    qq = hq * jax.nn.sigmoid(hq)
    vv = hi_ref[...]

    pos = lax.broadcasted_iota(jnp.int32, (seq, HEAD_W), 0) % c_len
    b = log_f
    sh = 1
    while sh < c_len:
        b = b + jnp.where(pos >= sh, pltpu.roll(b, sh, axis=0), 0.0)
        sh *= 2
    b3 = b.reshape(n_chunks, c_len, HEAD_W)
    b_last = b3[:, c_len - 1:c_len, :]
    q_dec = qq * jnp.exp(b)
    k_inv = kk * jnp.exp(-b)
    k_dec = kk * jnp.exp(b_last - b3).reshape(seq, HEAD_W)
    qd_ref[...] = q_dec.astype(BF16)
    kd_ref[...] = k_dec.astype(BF16)
    cd_ref[...] = jnp.exp(b_last.reshape(n_chunks, HEAD_W))

    blk = HEAD_W
    nb = seq // blk
    qd_b = q_dec.astype(BF16).reshape(nb, blk, HEAD_W)
    ki_b = k_inv.astype(BF16).reshape(nb, blk, HEAD_W)
    scores = jnp.einsum('nqd,nkd->nqk', qd_b, ki_b, preferred_element_type=F32)
    r = lax.broadcasted_iota(jnp.int32, (blk, blk), 0)
    c = lax.broadcasted_iota(jnp.int32, (blk, blk), 1)
    keep = (c <= r) & ((r // c_len) == (c // c_len))
    scores = jnp.where(keep[None], scores, 0.0)
    o_intra = jnp.einsum('nqk,nkv->nqv', scores.astype(BF16), vv.astype(BF16).reshape(nb, blk, HEAD_W),
                         preferred_element_type=F32).reshape(seq, HEAD_W)

    unroll = 8

    def rows_of(ci):
        return pl.ds(pl.multiple_of(ci * c_len, c_len), c_len)

    def updates(g, carry):
        for u in range(unroll):
            ci = g * unroll + u
            vc = hi_ref[rows_of(ci), :].astype(BF16)
            upd_ref[ci] = lax.dot_general(vc, kd_ref[rows_of(ci), :], (((0,), (0,)), ((), ())),
                                          preferred_element_type=F32)
        return carry

    lax.fori_loop(0, n_chunks // unroll, updates, 0)

    def recur(ci, st):
        prev_ref[ci] = st.astype(BF16)
        return st * cd_ref[pl.ds(ci, 1), :] + upd_ref[ci]

    lax.fori_loop(0, n_chunks, recur, jnp.zeros((HEAD_W, HEAD_W), F32))

    def reads(g, carry):
        for u in range(unroll):
            ci = g * unroll + u
            oi_ref[rows_of(ci), :] = lax.dot_general(qd_ref[rows_of(ci), :], prev_ref[ci],
                                                     (((1,), (1,)), ((), ())), preferred_element_type=F32)
        return carry

    lax.fori_loop(0, n_chunks // unroll, reads, 0)

    o = o_intra + oi_ref[...]
    hg = hg_ref[...]
    o_ref[...] = (_rms(o, g_ref[...]) * (hg * jax.nn.sigmoid(hg))).astype(o_ref.dtype)


def _hgrn(hin, lb, g, batch, seq):
    kern = functools.partial(_hgrn_kernel, seq=seq)
    col = lambda k: (lambda b, h: (b, 0, k * HGRN_HEADS + h))
    blk = (None, seq, HEAD_W)
    return pl.pallas_call(
        kern,
        grid=(batch, HGRN_HEADS),
        in_specs=[pl.BlockSpec(blk, col(0)), pl.BlockSpec(blk, col(1)),
                  pl.BlockSpec(blk, col(2)), pl.BlockSpec(blk, col(3)),
                  pl.BlockSpec((None, 1, HEAD_W), lambda b, h: (h, 0, 0)),
                  pl.BlockSpec((1, HEAD_W), lambda b, h: (0, 0))],
        out_specs=pl.BlockSpec(blk, lambda b, h: (b, 0, h)),
        out_shape=jax.ShapeDtypeStruct((batch, seq, HGRN_WIDTH), BF16),
        scratch_shapes=[pltpu.VMEM((seq, HEAD_W), BF16), pltpu.VMEM((seq, HEAD_W), BF16),
                        pltpu.VMEM((seq // HGRN_CHUNK, HEAD_W), F32), pltpu.VMEM((seq, HEAD_W), F32),
                        pltpu.VMEM((seq // HGRN_CHUNK, HEAD_W, HEAD_W), F32),
                        pltpu.VMEM((seq // HGRN_CHUNK, HEAD_W, HEAD_W), BF16)],
        compiler_params=_cparams(("parallel", "parallel")),
        name="hgrn",
    )(hin, hin, hin, hin, lb, g)


def _out_proj_kernel(x_ref, a_ref, o_ref, wa_ref, wo_ref, g_ref, h_ref, hnt_ref):
    h = (x_ref[...]
         + jnp.dot(a_ref[...], wa_ref[...], preferred_element_type=F32)
         + jnp.dot(o_ref[...], wo_ref[...], preferred_element_type=F32))
    h_ref[...] = h
    hnt_ref[...] = _rms(h, g_ref[...]).T.astype(BF16)


def _out_proj(x2, a2, o2, wa, wo, g, tb):
    t, d = x2.shape
    return pl.pallas_call(
        _out_proj_kernel,
        grid=(t // tb,),
        in_specs=[pl.BlockSpec((tb, d), lambda i: (i, 0)),
                  pl.BlockSpec((tb, DIFF_WIDTH), lambda i: (i, 0)),
                  pl.BlockSpec((tb, HGRN_WIDTH), lambda i: (i, 0)),
                  pl.BlockSpec((DIFF_WIDTH, d), lambda i: (0, 0)),
                  pl.BlockSpec((HGRN_WIDTH, d), lambda i: (0, 0)),
                  pl.BlockSpec((1, d), lambda i: (0, 0))],
        out_specs=[pl.BlockSpec((tb, d), lambda i: (i, 0)),
                   pl.BlockSpec((d, tb), lambda i: (0, i))],
        out_shape=[jax.ShapeDtypeStruct((t, d), F32),
                   jax.ShapeDtypeStruct((d, t), BF16)],
        compiler_params=_cparams(("parallel",)),
        name="out_proj",
    )(x2, a2, o2, wa, wo, g)


SUBLANES = 8
ATTN_HEADS_PER_STEP = 4
TAU_GUARD = 2.0 ** -21


def _oddeven_merge(lo, hi, r):
    step = r * 2
    if step < hi - lo:
        yield from _oddeven_merge(lo, hi, step)
        yield from _oddeven_merge(lo + r, hi, step)
        yield from [(i, i + r) for i in range(lo + r, hi - r, step)]
    else:
        yield (lo, lo + r)


def _oddeven_merge_sort(lo, hi):
    if hi - lo >= 1:
        mid = lo + (hi - lo) // 2
        yield from _oddeven_merge_sort(lo, mid)
        yield from _oddeven_merge_sort(mid + 1, hi)
        yield from _oddeven_merge(lo, hi, 1)


def _bitonic_merge(n):
    d = n // 2
    while d >= 1:
        yield from [(i, i + d) for i in range(n) if (i // d) % 2 == 0]
        d //= 2


SORT16 = tuple(_oddeven_merge_sort(0, PEER_TOPK - 1))
MERGE16 = tuple(_bitonic_merge(PEER_TOPK))


def _bf16_row_tile(row, n_rows):
    packed_rows = 16
    tile16 = jnp.broadcast_to(row, (packed_rows, row.shape[1])).astype(BF16)
    return jnp.broadcast_to(tile16[None], (n_rows // packed_rows, packed_rows, row.shape[1])
                            ).reshape(n_rows, row.shape[1])


def _compare_exchange(v, pairs):
    for i, j in pairs:
        v[i], v[j] = jnp.maximum(v[i], v[j]), jnp.minimum(v[i], v[j])


def _top16_sorted(x):
    v = [x[SUBLANES * k:SUBLANES * (k + 1)] for k in range(PEER_TOPK)]
    _compare_exchange(v, SORT16)
    shift = SUBLANES // 2
    while shift >= 1:
        v = [jnp.maximum(v[k], pltpu.roll(v[PEER_TOPK - 1 - k], shift, axis=0)) for k in range(PEER_TOPK)]
        _compare_exchange(v, MERGE16)
        shift //= 2
    return v


def _rows_on_sublanes(rep, start):
    sub = lax.broadcasted_iota(jnp.int32, rep[0].shape, 0)
    out = rep[start]
    for s in range(1, SUBLANES):
        out = jnp.where(sub == s, rep[start + s], out)
    return out


def _pair_candidates(a_rep, b_rep):
    k = PEER_TOPK
    sub = lax.broadcasted_iota(jnp.int32, a_rep[0].shape, 0)
    b_lo = _rows_on_sublanes(b_rep, 0)
    b_hi = _rows_on_sublanes(b_rep, SUBLANES)
    a_hi = _rows_on_sublanes(a_rep, SUBLANES)
    pieces = [a_rep[0] + b_lo, a_rep[0] + b_hi, a_rep[1] + b_lo]
    for i in range(2, SUBLANES):
        pieces.append(jnp.where(sub < k // (i + 1), a_rep[i] + b_lo, -jnp.inf))
    pieces.append(a_hi + b_rep[0])
    return jnp.concatenate(pieces, axis=0)


def _kth_largest(x, k):
    for r in range(k):
        m = jnp.max(x, axis=0, keepdims=True)
        if r + 1 < k:
            x = jnp.where(x == m, -jnp.inf, x)
    return m


def _peer_route_kernel(hnt_ref, wqt_ref, keys_ref, rank_ref, alpha_ref, nsel_ref, beta_ref, qt_ref, s_ref):
    qt_ref[...] = jnp.dot(wqt_ref[...], hnt_ref[...], preferred_element_type=F32).astype(BF16)
    half = PEER_QUERY_DIM // 2
    lane_tiles = hnt_ref.shape[1] // LANES
    k = PEER_TOPK

    def head(h, carry):
        base = pl.multiple_of(h * PEER_QUERY_DIM, PEER_QUERY_DIM)
        for c in range(2):
            s = jnp.dot(keys_ref[h, c], qt_ref[pl.ds(base + c * half, half), :], preferred_element_type=F32)
            for lt in range(lane_tiles):
                s_ref[c, lt] = s[:, lt * LANES:(lt + 1) * LANES]

        def tile(lt, carry):
            cols = pl.ds(pl.multiple_of(lt * LANES, LANES), LANES)
            s1 = s_ref[0, lt]
            s2 = s_ref[1, lt]
            a_rep = _top16_sorted(s1)
            b_rep = _top16_sorted(s2)
            cands = _pair_candidates(a_rep, b_rep)
            thr = _kth_largest(cands, k)
            top = a_rep[0][0:1] + b_rep[0][0:1]
            z = jnp.sum(jnp.where(cands >= thr, jnp.exp(cands - top), 0.0), axis=0, keepdims=True)
            rank1 = []
            n_sel = []
            theta = [(thr - a_rep[j]) - (jnp.abs(thr) + jnp.abs(a_rep[j])) * TAU_GUARD for j in range(k)]
            for v in range(PEER_N_KEYS // SUBLANES):
                x1 = s1[SUBLANES * v:SUBLANES * (v + 1)]
                x2 = s2[SUBLANES * v:SUBLANES * (v + 1)]
                r1 = jnp.zeros_like(x1)
                ns = jnp.zeros_like(x2)
                for j in range(k):
                    r1 = jnp.where(a_rep[j] > x1, float(j + 1), r1)
                    ns = jnp.where(x2 >= theta[j], float(j + 1), ns)
                rank1.append(r1)
                n_sel.append(ns)
            rank_ref[h, :, cols] = jnp.concatenate(rank1, axis=0)
            alpha_ref[h, :, cols] = jnp.exp(s1 - a_rep[0][0:1]) * (0.5 / z)
            nsel_ref[h, lt] = jnp.concatenate(n_sel, axis=0).astype(BF16)
            beta_ref[h, lt] = jnp.exp(s2 - b_rep[0][0:1]).astype(BF16)
            return carry

        return lax.fori_loop(0, lane_tiles, tile, carry)

    lax.fori_loop(0, PEER_HEADS, head, 0)


def _peer_route(hnt, wqt, keys, tb):
    d, t = hnt.shape
    nq = wqt.shape[0]
    oshape = jax.ShapeDtypeStruct((PEER_HEADS, PEER_N_KEYS, t), F32)
    ospec = pl.BlockSpec((PEER_HEADS, PEER_N_KEYS, tb), lambda i: (0, 0, i))
    tshape = jax.ShapeDtypeStruct((PEER_HEADS, t // LANES, PEER_N_KEYS, LANES), BF16)
    tspec = pl.BlockSpec((PEER_HEADS, tb // LANES, PEER_N_KEYS, LANES), lambda i: (0, i, 0, 0))
    return pl.pallas_call(
        _peer_route_kernel,
        grid=(t // tb,),
        in_specs=[pl.BlockSpec((d, tb), lambda i: (0, i)),
                  pl.BlockSpec((nq, d), lambda i: (0, 0)),
                  pl.BlockSpec(keys.shape, lambda i: (0, 0, 0, 0))],
        out_specs=[ospec, ospec, tspec, tspec],
        out_shape=[oshape, oshape, tshape, tshape],
        scratch_shapes=[pltpu.VMEM((nq, tb), BF16), pltpu.VMEM((2, tb // LANES, PEER_N_KEYS, LANES), F32)],
        compiler_params=_cparams(("parallel",)),
        name="peer_route",
    )(hnt, wqt, keys)


def _peer_dense_kernel(hnt_ref, u_ref, vt_ref, rank_ref, alpha_ref, nsel_ref, beta_ref, o_ref,
                       hid_ref, act_ref, *, te, tb, n_tiles):
    j = pl.program_id(1)
    cur = j % 2
    prev = 1 - cur
    rows_per_tile = te // PEER_N_KEYS
    lane_tiles = tb // LANES

    def hidden(hid_ref):
        hid = jnp.dot(u_ref[...], hnt_ref[...], preferred_element_type=F32)
        for lt in range(lane_tiles):
            hid_ref[lt] = hid[:, lt * LANES:(lt + 1) * LANES]

    def activations(hid_ref, act_ref):
        def lane_tile(lt, carry):
            cols = pl.ds(pl.multiple_of(lt * LANES, LANES), LANES)
            zero = jnp.zeros((PEER_N_KEYS, LANES), BF16)
            for r in range(rows_per_tile):
                w = None
                for h in range(PEER_HEADS):
                    sel = _bf16_row_tile(rank_ref[h, r:r + 1, cols], PEER_N_KEYS) < nsel_ref[h, lt]
                    term = jnp.where(sel, beta_ref[h, lt], zero) * _bf16_row_tile(alpha_ref[h, r:r + 1, cols],
                                                                                  PEER_N_KEYS)
                    w = term if w is None else w + term
                rows = slice(r * PEER_N_KEYS, (r + 1) * PEER_N_KEYS)
                hid_t = hid_ref[lt, rows, :]
                gelu2 = hid_t * (1.0 + lax.erf(hid_t * (2.0 ** -0.5)))
                act_ref[lt, rows, :] = gelu2.astype(BF16) * w
            return carry

        lax.fori_loop(0, lane_tiles, lane_tile, 0)

    def project(act_ref):
        act = jnp.concatenate([act_ref[lt] for lt in range(lane_tiles)], axis=1)
        o_ref[...] += jnp.dot(vt_ref[...], act, preferred_element_type=F32)

    def run(do_a, do_b, do_c):
        if do_a:
            hidden(hid_ref.at[cur])
        if do_b:
            activations(hid_ref.at[prev], act_ref.at[prev])
        if do_c:
            project(act_ref.at[cur])

    @pl.when(jnp.logical_and(j >= 2, j < n_tiles))
    def _():
        run(True, True, True)

    @pl.when(j == 0)
    def _():
        o_ref[...] = jnp.zeros(o_ref.shape, o_ref.dtype)
        run(True, False, False)

    @pl.when(j == 1)
    def _():
        run(True, True, False)

    @pl.when(j == n_tiles)
    def _():
        run(False, True, True)

    @pl.when(j == n_tiles + 1)
    def _():
        run(False, False, True)


def _peer_dense(hnt, u_bf16, v_bf16, rank1, alpha, n_sel, beta, tb, te):
    d, t = hnt.shape
    n_tiles = u_bf16.shape[0] // te
    assert n_tiles >= 2
    vt_tiles = v_bf16.reshape(n_tiles, te, d).transpose(0, 2, 1)
    kern = functools.partial(_peer_dense_kernel, te=te, tb=tb, n_tiles=n_tiles)
    last = n_tiles - 1
    rspec = pl.BlockSpec((PEER_HEADS, tb // LANES, PEER_N_KEYS, LANES), lambda i, j: (0, i, 0, 0))
    i1spec = pl.BlockSpec((PEER_HEADS, te // PEER_N_KEYS, tb), lambda i, j: (0, jnp.clip(j - 1, 0, last), i))
    stage_shape = (tb // LANES, te, LANES)
    return pl.pallas_call(
        kern,
        grid=(t // tb, n_tiles + 2),
        in_specs=[pl.BlockSpec((d, tb), lambda i, j: (0, i)),
                  pl.BlockSpec((te, d), lambda i, j: (jnp.minimum(j, last), 0)),
                  pl.BlockSpec((None, d, te), lambda i, j: (jnp.clip(j - 2, 0, last), 0, 0)),
                  i1spec, i1spec, rspec, rspec],
        out_specs=pl.BlockSpec((d, tb), lambda i, j: (0, i)),
        out_shape=jax.ShapeDtypeStruct((d, t), F32),
        scratch_shapes=[pltpu.VMEM((2,) + stage_shape, F32), pltpu.VMEM((2,) + stage_shape, BF16)],
        compiler_params=_cparams(("parallel", "arbitrary")),
        name="peer_dense",
    )(hnt, u_bf16, vt_tiles, rank1, alpha, n_sel, beta)


def _final_norm_kernel(h_ref, pt_ref, g_ref, o_ref):
    o_ref[...] = _rms(h_ref[...] + pt_ref[...].T, g_ref[...])


def _final_norm(h2, peer_t, g, tb):
    t, d = h2.shape
    return pl.pallas_call(
        _final_norm_kernel,
        grid=(t // tb,),
        in_specs=[pl.BlockSpec((tb, d), lambda i: (i, 0)),
                  pl.BlockSpec((d, tb), lambda i: (0, i)),
                  pl.BlockSpec((1, d), lambda i: (0, 0))],
        out_specs=pl.BlockSpec((tb, d), lambda i: (i, 0)),
        out_shape=jax.ShapeDtypeStruct((t, d), F32),
        compiler_params=_cparams(("parallel",)),
        name="final_norm",
    )(h2, peer_t, g)


def _largest_divisor(n, cap, multiple):
    best = multiple
    k = multiple
    while k <= min(n, cap):
        if n % k == 0:
            best = k
        k += multiple
    return best


def kernel(x, norm1_g, w_in, diff_lambda_q1, diff_lambda_k1, diff_lambda_q2, diff_lambda_k2, diff_subln_g,
           rel_bias_table, hgrn_lb_logits, hgrn_gnorm_g, w_out, norm2_g, peer_w_q, peer_sub_keys, peer_u,
           peer_v, final_norm_g):
    batch, seq, d = x.shape
    t = batch * seq
    assert d == D_MODEL and seq % 256 == 0 and w_in.shape[0] == 1
    tb_proj = _largest_divisor(t, 512, 256)
    tq = 256
    tb_route = _largest_divisor(t, 512, 256)
    tb_peer = _largest_divisor(t, 1024, 512)
    te_peer = 1024

    x2 = x.reshape(t, d)
    row = lambda v: v.reshape(1, -1).astype(F32)

    lam = (jnp.exp(jnp.sum(diff_lambda_q1[0].astype(F32) * diff_lambda_k1[0].astype(F32)))
           - jnp.exp(jnp.sum(diff_lambda_q2[0].astype(F32) * diff_lambda_k2[0].astype(F32)))
           + LAM_INIT).reshape(1)
    lb = jnp.cumsum(jax.nn.softmax(hgrn_lb_logits.astype(F32), axis=0), axis=0)[0]
    lb = lb.reshape(HGRN_HEADS, 1, HEAD_W)
    table = rel_bias_table.astype(F32)

    q_t, k, v_t, hin = _norm_proj(x2, row(norm1_g[0]), w_in[0].astype(BF16), tb_proj)
    bias = _bias_tiles(table, tq)
    a = _diff_attn(lam, table, q_t, k, v_t, bias, row(diff_subln_g[0]), batch, seq, tq)
    o = _hgrn(hin.reshape(batch, seq, HGRN_COLS), lb, row(hgrn_gnorm_g[0]), batch, seq)
    w_o = w_out[0].astype(BF16)
    h2, hnt = _out_proj(x2, a.reshape(t, DIFF_WIDTH), o.reshape(t, HGRN_WIDTH),
                        w_o[:DIFF_WIDTH], w_o[DIFF_WIDTH:], row(norm2_g[0]), tb_proj)
    rank1, alpha, n_sel, beta = _peer_route(hnt, peer_w_q[0].T.astype(BF16), peer_sub_keys[0].astype(BF16), tb_route)
    peer_t = _peer_dense(hnt, peer_u[0].astype(BF16), peer_v[0].astype(BF16), rank1, alpha, n_sel, beta,
                         tb_peer, te_peer)
    out = _final_norm(h2, peer_t, row(final_norm_g), tb_proj)
    return out.reshape(batch, seq, d)
```

```python
import functools
import math

import numpy as np
import jax
import jax.numpy as jnp
from jax import lax
from jax.experimental import pallas as pl
from jax.experimental.pallas import tpu as pltpu

F32 = jnp.float32
BF16 = jnp.bfloat16

D_MODEL = 1024
DIFF_HEADS = 4
DIFF_QK_DIM = 64
LANES = 128
MXU_WIDTH = 256
SECTION_ROWS = MXU_WIDTH
HEAD_W = 128
DIFF_WIDTH = DIFF_HEADS * HEAD_W
HGRN_HEADS = 4
HGRN_WIDTH = HGRN_HEADS * HEAD_W
HGRN_CHUNK = 32
ATTN_COLS = 3 * DIFF_WIDTH
HGRN_COLS = 4 * HGRN_WIDTH
REL_BUCKETS = 32
REL_MAX_DIST = 128
PEER_HEADS = 8
PEER_N_KEYS = 128
PEER_N_EXPERTS = PEER_N_KEYS * PEER_N_KEYS
PEER_QUERY_DIM = 256
PEER_TOPK = 16
NORM_EPS = 1e-6
LAM_INIT = 0.8 - 0.6 * math.exp(-0.3 * 0)
NEG_BIG = -1e30

VMEM_LIMIT_BYTES = 56 * 1024 * 1024


def _cparams(sem):
    return pltpu.CompilerParams(dimension_semantics=sem, vmem_limit_bytes=VMEM_LIMIT_BYTES)


def _rms(x, g):
    return x * lax.rsqrt(jnp.mean(x * x, axis=-1, keepdims=True) + NORM_EPS) * g


def _norm_proj_kernel(x_ref, g_ref, w_ref, qt_ref, k_ref, vt_ref, oh_ref):
    y = _rms(x_ref[...], g_ref[...])
    p = jnp.dot(y.astype(BF16), w_ref[...], preferred_element_type=F32)
    qt_ref[...] = p[:, :DIFF_WIDTH].T.astype(BF16)
    k_ref[...] = p[:, DIFF_WIDTH:2 * DIFF_WIDTH].astype(BF16)
    vt_ref[...] = p[:, 2 * DIFF_WIDTH:ATTN_COLS].T.astype(BF16)
    oh_ref[...] = p[:, ATTN_COLS:]


def _norm_proj(x2, g, w_bf16, tb):
    t, d = x2.shape
    n = w_bf16.shape[1]
    return pl.pallas_call(
        _norm_proj_kernel,
        grid=(t // tb,),
        in_specs=[pl.BlockSpec((tb, d), lambda i: (i, 0)),
                  pl.BlockSpec((1, d), lambda i: (0, 0)),
                  pl.BlockSpec((d, n), lambda i: (0, 0))],
        out_specs=[pl.BlockSpec((DIFF_WIDTH, tb), lambda i: (0, i)),
                   pl.BlockSpec((tb, DIFF_WIDTH), lambda i: (i, 0)),
                   pl.BlockSpec((DIFF_WIDTH, tb), lambda i: (0, i)),
                   pl.BlockSpec((tb, HGRN_COLS), lambda i: (i, 0))],
        out_shape=[jax.ShapeDtypeStruct((DIFF_WIDTH, t), BF16),
                   jax.ShapeDtypeStruct((t, DIFF_WIDTH), BF16),
                   jax.ShapeDtypeStruct((DIFF_WIDTH, t), BF16),
                   jax.ShapeDtypeStruct((t, HGRN_COLS), F32)],
        compiler_params=_cparams(("parallel",)),
        name="norm_proj",
    )(x2, g, w_bf16)


def _t5_bucket_np(n):
    n = np.maximum(n, 0).astype(np.int32)
    max_exact = REL_BUCKETS // 2
    nf = np.maximum(n, max_exact).astype(np.float32)
    large = max_exact + (np.log(nf / np.float32(max_exact)) / np.float32(math.log(REL_MAX_DIST / max_exact))
                         * np.float32(REL_BUCKETS - max_exact)).astype(np.int32)
    large = np.minimum(large, REL_BUCKETS - 1)
    return np.where(n < max_exact, n, large).astype(np.int32)


def _bias_tiles_kernel(table_ref, bucket_ref, o_ref):
    h = pl.program_id(0)
    bucket = bucket_ref[...]
    acc = jnp.zeros(bucket.shape, F32)
    for b in range(REL_BUCKETS):
        acc = jnp.where(bucket == b, table_ref[b, h], acc)
    o_ref[...] = acc


def _bias_tiles(table, tq):
    c = np.arange(tq)[:, None]
    r = np.arange(tq)[None, :]
    tiles = np.stack([_t5_bucket_np(d * tq + r - c) for d in (0, 1)])
    tiles = np.concatenate([tiles, tiles], axis=2)
    return pl.pallas_call(
        _bias_tiles_kernel,
        grid=(DIFF_HEADS,),
        in_specs=[pl.BlockSpec(memory_space=pltpu.SMEM),
                  pl.BlockSpec((2, tq, 2 * tq), lambda h: (0, 0, 0))],
        out_specs=pl.BlockSpec((None, 2, tq, 2 * tq), lambda h: (h, 0, 0, 0)),
        out_shape=jax.ShapeDtypeStruct((DIFF_HEADS, 2, tq, 2 * tq), F32),
        compiler_params=_cparams(("arbitrary",)),
        name="bias_tiles",
    )(table, jnp.asarray(tiles))


def _diff_attn_kernel(lam_ref, table_ref, qt_ref, k_ref, vt_ref, bias_ref, g_ref, o_ref,
                      s_ref, mx_ref, l_ref, acc_ref, *, tq):
    h0 = pl.program_id(1) * ATTN_HEADS_PER_STEP
    i = pl.program_id(2)
    heads = range(ATTN_HEADS_PER_STEP)
    hrows = lambda hh: slice(hh * HEAD_W, (hh + 1) * HEAD_W)
    q2t = []
    for hh in heads:
        qt = qt_ref[hrows(hh), :] * jnp.asarray(DIFF_QK_DIM ** -0.5, BF16)
        row = lax.broadcasted_iota(jnp.int32, qt.shape, 0)
        zero = jnp.zeros_like(qt)
        q2t.append(jnp.concatenate([jnp.where(row < DIFF_QK_DIM, qt, zero),
                                    jnp.where(row >= DIFF_QK_DIM, qt, zero)], axis=1))

    def fold(op, x):
        return functools.reduce(op, [x[SUBLANES * k:SUBLANES * (k + 1)] for k in range(x.shape[0] // SUBLANES)])

    def scores(j, n_keys, delta, causal):
        start = pl.multiple_of(j * tq, tq)
        for hh in heads:
            bias = table_ref[REL_BUCKETS - 1, h0 + hh] if delta is None else bias_ref[hh, delta]
            s = jnp.dot(k_ref[pl.ds(start, n_keys), hrows(hh)], q2t[hh], preferred_element_type=F32) + bias
            if causal:
                c = lax.broadcasted_iota(jnp.int32, s.shape, 0)
                r = lax.broadcasted_iota(jnp.int32, s.shape, 1) % tq
                s = jnp.where(c <= r, s, NEG_BIG)
            for k in range(n_keys // tq):
                s_ref[hh, j + k] = s[k * tq:(k + 1) * tq]
            mx_ref[hh] = jnp.maximum(mx_ref[hh], fold(jnp.maximum, s))

    mx_ref[...] = jnp.full(mx_ref.shape, NEG_BIG, F32)
    l_ref[...] = jnp.zeros(l_ref.shape, F32)
    acc_ref[...] = jnp.zeros(acc_ref.shape, F32)

    def far_pair(p, carry):
        scores(2 * p, 2 * tq, None, False)
        return carry

    n_far = jnp.maximum(i - 1, 0)
    lax.fori_loop(0, n_far // 2, far_pair, 0)

    @pl.when(n_far % 2 == 1)
    def _():
        scores(n_far - 1, tq, None, False)

    @pl.when(i >= 1)
    def _():
        scores(i - 1, tq, 1, False)

    scores(i, tq, 0, True)
    m = [jnp.max(mx_ref[hh], axis=0, keepdims=True) for hh in heads]

    def weigh(j, n_keys):
        start = pl.multiple_of(j * tq, tq)
        for hh in heads:
            s = jnp.concatenate([s_ref[hh, j + k] for k in range(n_keys // tq)], axis=0)
            e = jnp.exp(s - m[hh])
            l_ref[hh] += fold(jnp.add, e)
            acc_ref[hh] += jnp.dot(vt_ref[hrows(hh), pl.ds(start, n_keys)], e.astype(BF16),
                                   preferred_element_type=F32)

    def weigh_pair(p, carry):
        weigh(2 * p, 2 * tq)
        return carry

    n_blocks = i + 1
    lax.fori_loop(0, n_blocks // 2, weigh_pair, 0)

    @pl.when(n_blocks % 2 == 1)
    def _():
        weigh(i, tq)

    lam = lam_ref[0]
    for hh in heads:
        l = jnp.sum(l_ref[hh], axis=0, keepdims=True)
        acc = acc_ref[hh]
        out_t = acc[:, :tq] / l[:, :tq] - lam * (acc[:, tq:] / l[:, tq:])
        o_ref[:, hrows(hh)] = (_rms(out_t.T, g_ref[...]) * (1.0 - LAM_INIT)).astype(o_ref.dtype)


def _diff_attn(lam, table, q_t, k, v_t, bias, g, batch, seq, tq):
    kern = functools.partial(_diff_attn_kernel, tq=tq)
    nq = seq // tq
    hp = ATTN_HEADS_PER_STEP
    hw = hp * HEAD_W
    return pl.pallas_call(
        kern,
        grid=(batch, DIFF_HEADS // hp, nq),
        in_specs=[pl.BlockSpec(memory_space=pltpu.SMEM),
                  pl.BlockSpec(memory_space=pltpu.SMEM),
                  pl.BlockSpec((hw, tq), lambda b, h, i: (h, b * nq + i)),
                  pl.BlockSpec((seq, hw), lambda b, h, i: (b, h)),
                  pl.BlockSpec((hw, seq), lambda b, h, i: (h, b)),
                  pl.BlockSpec((hp, 2, tq, 2 * tq), lambda b, h, i: (h, 0, 0, 0)),
                  pl.BlockSpec((1, HEAD_W), lambda b, h, i: (0, 0))],
        out_specs=pl.BlockSpec((None, tq, hw), lambda b, h, i: (b, i, h)),
        out_shape=jax.ShapeDtypeStruct((batch, seq, DIFF_WIDTH), BF16),
        scratch_shapes=[pltpu.VMEM((hp, seq // tq, tq, 2 * tq), F32),
                        pltpu.VMEM((hp, SUBLANES, 2 * tq), F32),
                        pltpu.VMEM((hp, SUBLANES, 2 * tq), F32),
                        pltpu.VMEM((hp, HEAD_W, 2 * tq), F32)],
        compiler_params=_cparams(("parallel", "parallel", "arbitrary")),
        name="diff_attn",
    )(lam, table, q_t, k, v_t, bias, g)


def _hgrn_kernel(hq_ref, hf_ref, hi_ref, hg_ref, lb_ref, g_ref, o_ref,
                 qd_ref, kd_ref, cd_ref, oi_ref, upd_ref, prev_ref, *, seq):
    c_len = HGRN_CHUNK
    n_chunks = seq // c_len
    lb = lb_ref[...]
    f = lb + (1.0 - lb) * jax.nn.sigmoid(hf_ref[...])
    log_f = jnp.log(f)
    kk = 1.0 - f
    hq = hq_ref[...]
    qq = hq * jax.nn.sigmoid(hq)
    vv = hi_ref[...]

    pos = lax.broadcasted_iota(jnp.int32, (seq, HEAD_W), 0) % c_len
    b = log_f
    sh = 1
    while sh < c_len:
        b = b + jnp.where(pos >= sh, pltpu.roll(b, sh, axis=0), 0.0)
        sh *= 2
    b3 = b.reshape(n_chunks, c_len, HEAD_W)
    b_last = b3[:, c_len - 1:c_len, :]
    q_dec = qq * jnp.exp(b)
    k_inv = kk * jnp.exp(-b)
    k_dec = kk * jnp.exp(b_last - b3).reshape(seq, HEAD_W)
    qd_ref[...] = q_dec.astype(BF16)
    kd_ref[...] = k_dec.astype(BF16)
    cd_ref[...] = jnp.exp(b_last.reshape(n_chunks, HEAD_W))

    blk = HEAD_W
    nb = seq // blk
    qd_b = q_dec.astype(BF16).reshape(nb, blk, HEAD_W)
    ki_b = k_inv.astype(BF16).reshape(nb, blk, HEAD_W)
    scores = jnp.einsum('nqd,nkd->nqk', qd_b, ki_b, preferred_element_type=F32)
    r = lax.broadcasted_iota(jnp.int32, (blk, blk), 0)
    c = lax.broadcasted_iota(jnp.int32, (blk, blk), 1)
    keep = (c <= r) & ((r // c_len) == (c // c_len))
    scores = jnp.where(keep[None], scores, 0.0)
    o_intra = jnp.einsum('nqk,nkv->nqv', scores.astype(BF16), vv.astype(BF16).reshape(nb, blk, HEAD_W),
                         preferred_element_type=F32).reshape(seq, HEAD_W)

    unroll = 16

    def rows_of(ci):
        return pl.ds(pl.multiple_of(ci * c_len, c_len), c_len)

    def updates(g, carry):
        for u in range(unroll):
            ci = g * unroll + u
            vc = hi_ref[rows_of(ci), :].astype(BF16)
            upd_ref[ci] = lax.dot_general(vc, kd_ref[rows_of(ci), :], (((0,), (0,)), ((), ())),
                                          preferred_element_type=F32)
        return carry

    lax.fori_loop(0, n_chunks // unroll, updates, 0)

    def recur(ci, st):
        prev_ref[ci] = st.astype(BF16)
        return st * cd_ref[pl.ds(ci, 1), :] + upd_ref[ci]

    lax.fori_loop(0, n_chunks, recur, jnp.zeros((HEAD_W, HEAD_W), F32))

    def reads(g, carry):
        for u in range(unroll):
            ci = g * unroll + u
            oi_ref[rows_of(ci), :] = lax.dot_general(qd_ref[rows_of(ci), :], prev_ref[ci],
                                                     (((1,), (1,)), ((), ())), preferred_element_type=F32)
        return carry

    lax.fori_loop(0, n_chunks // unroll, reads, 0)

    o = o_intra + oi_ref[...]
    hg = hg_ref[...]
    o_ref[...] = (_rms(o, g_ref[...]) * (hg * jax.nn.sigmoid(hg))).astype(o_ref.dtype)


def _hgrn(hin, lb, g, batch, seq):
    kern = functools.partial(_hgrn_kernel, seq=seq)
    col = lambda k: (lambda b, h: (b, 0, k * HGRN_HEADS + h))
    blk = (None, seq, HEAD_W)
    return pl.pallas_call(
        kern,
        grid=(batch, HGRN_HEADS),
        in_specs=[pl.BlockSpec(blk, col(0)), pl.BlockSpec(blk, col(1)),
                  pl.BlockSpec(blk, col(2)), pl.BlockSpec(blk, col(3)),
                  pl.BlockSpec((None, 1, HEAD_W), lambda b, h: (h, 0, 0)),
                  pl.BlockSpec((1, HEAD_W), lambda b, h: (0, 0))],
        out_specs=pl.BlockSpec(blk, lambda b, h: (b, 0, h)),
        out_shape=jax.ShapeDtypeStruct((batch, seq, HGRN_WIDTH), BF16),
        scratch_shapes=[pltpu.VMEM((seq, HEAD_W), BF16), pltpu.VMEM((seq, HEAD_W), BF16),
                        pltpu.VMEM((seq // HGRN_CHUNK, HEAD_W), F32), pltpu.VMEM((seq, HEAD_W), F32),
                        pltpu.VMEM((seq // HGRN_CHUNK, HEAD_W, HEAD_W), F32),
                        pltpu.VMEM((seq // HGRN_CHUNK, HEAD_W, HEAD_W), BF16)],
        compiler_params=_cparams(("parallel", "parallel")),
        name="hgrn",
    )(hin, hin, hin, hin, lb, g)


def _out_proj_kernel(x_ref, a_ref, o_ref, wa_ref, wo_ref, g_ref, h_ref, hnt_ref):
    h = (x_ref[...]
         + jnp.dot(a_ref[...], wa_ref[...], preferred_element_type=F32)
         + jnp.dot(o_ref[...], wo_ref[...], preferred_element_type=F32))
    h_ref[...] = h
    hnt_ref[...] = _rms(h, g_ref[...]).T.astype(BF16)


def _out_proj(x2, a2, o2, wa, wo, g, tb):
    t, d = x2.shape
    return pl.pallas_call(
        _out_proj_kernel,
        grid=(t // tb,),
        in_specs=[pl.BlockSpec((tb, d), lambda i: (i, 0)),
                  pl.BlockSpec((tb, DIFF_WIDTH), lambda i: (i, 0)),
                  pl.BlockSpec((tb, HGRN_WIDTH), lambda i: (i, 0)),
                  pl.BlockSpec((DIFF_WIDTH, d), lambda i: (0, 0)),
                  pl.BlockSpec((HGRN_WIDTH, d), lambda i: (0, 0)),
                  pl.BlockSpec((1, d), lambda i: (0, 0))],
        out_specs=[pl.BlockSpec((tb, d), lambda i: (i, 0)),
                   pl.BlockSpec((d, tb), lambda i: (0, i))],
        out_shape=[jax.ShapeDtypeStruct((t, d), F32),
                   jax.ShapeDtypeStruct((d, t), BF16)],
        compiler_params=_cparams(("parallel",)),
        name="out_proj",
    )(x2, a2, o2, wa, wo, g)


SUBLANES = 8
ATTN_HEADS_PER_STEP = 4
TAU_GUARD = 2.0 ** -21


def _oddeven_merge(lo, hi, r):
    step = r * 2
    if step < hi - lo:
        yield from _oddeven_merge(lo, hi, step)
        yield from _oddeven_merge(lo + r, hi, step)
        yield from [(i, i + r) for i in range(lo + r, hi - r, step)]
    else:
        yield (lo, lo + r)


def _oddeven_merge_sort(lo, hi):
    if hi - lo >= 1:
        mid = lo + (hi - lo) // 2
        yield from _oddeven_merge_sort(lo, mid)
        yield from _oddeven_merge_sort(mid + 1, hi)
        yield from _oddeven_merge(lo, hi, 1)


def _bitonic_merge(n):
    d = n // 2
    while d >= 1:
        yield from [(i, i + d) for i in range(n) if (i // d) % 2 == 0]
        d //= 2


SORT16 = tuple(_oddeven_merge_sort(0, PEER_TOPK - 1))
MERGE16 = tuple(_bitonic_merge(PEER_TOPK))


def _bf16_row_tile(row, n_rows):
    packed_rows = 16
    tile16 = jnp.broadcast_to(row, (packed_rows, row.shape[1])).astype(BF16)
    return jnp.broadcast_to(tile16[None], (n_rows // packed_rows, packed_rows, row.shape[1])
                            ).reshape(n_rows, row.shape[1])


def _compare_exchange(v, pairs):
    for i, j in pairs:
        v[i], v[j] = jnp.maximum(v[i], v[j]), jnp.minimum(v[i], v[j])


def _top16_sorted(x):
    v = [x[SUBLANES * k:SUBLANES * (k + 1)] for k in range(PEER_TOPK)]
    _compare_exchange(v, SORT16)
    shift = SUBLANES // 2
    while shift >= 1:
        v = [jnp.maximum(v[k], pltpu.roll(v[PEER_TOPK - 1 - k], shift, axis=0)) for k in range(PEER_TOPK)]
        _compare_exchange(v, MERGE16)
        shift //= 2
    return v


def _rows_on_sublanes(rep, start):
    sub = lax.broadcasted_iota(jnp.int32, rep[0].shape, 0)
    out = rep[start]
    for s in range(1, SUBLANES):
        out = jnp.where(sub == s, rep[start + s], out)
    return out


def _pair_candidates(a_rep, b_rep):
    k = PEER_TOPK
    sub = lax.broadcasted_iota(jnp.int32, a_rep[0].shape, 0)
    b_lo = _rows_on_sublanes(b_rep, 0)
    b_hi = _rows_on_sublanes(b_rep, SUBLANES)
    a_hi = _rows_on_sublanes(a_rep, SUBLANES)
    pieces = [a_rep[0] + b_lo, a_rep[0] + b_hi, a_rep[1] + b_lo]
    for i in range(2, SUBLANES):
        pieces.append(jnp.where(sub < k // (i + 1), a_rep[i] + b_lo, -jnp.inf))
    pieces.append(a_hi + b_rep[0])
    return jnp.concatenate(pieces, axis=0)


def _kth_largest(x, k):
    for r in range(k):
        m = jnp.max(x, axis=0, keepdims=True)
        if r + 1 < k:
            x = jnp.where(x == m, -jnp.inf, x)
    return m


def _peer_route_kernel(hnt_ref, wqt_ref, keys_ref, rank_ref, alpha_ref, nsel_ref, beta_ref, qt_ref, s_ref):
    qt_ref[...] = jnp.dot(wqt_ref[...], hnt_ref[...], preferred_element_type=F32).astype(BF16)
    half = PEER_QUERY_DIM // 2
    lane_tiles = hnt_ref.shape[1] // LANES
    k = PEER_TOPK

    def head(h, carry):
        base = pl.multiple_of(h * PEER_QUERY_DIM, PEER_QUERY_DIM)
        for c in range(2):
            s = jnp.dot(keys_ref[h, c], qt_ref[pl.ds(base + c * half, half), :], preferred_element_type=F32)
            for lt in range(lane_tiles):
                s_ref[c, lt] = s[:, lt * LANES:(lt + 1) * LANES]

        def tile(lt, carry):
            cols = pl.ds(pl.multiple_of(lt * LANES, LANES), LANES)
            s1 = s_ref[0, lt]
            s2 = s_ref[1, lt]
            a_rep = _top16_sorted(s1)
            b_rep = _top16_sorted(s2)
            cands = _pair_candidates(a_rep, b_rep)
            thr = _kth_largest(cands, k)
            top = a_rep[0][0:1] + b_rep[0][0:1]
            z = jnp.sum(jnp.where(cands >= thr, jnp.exp(cands - top), 0.0), axis=0, keepdims=True)
            rank1 = []
            n_sel = []
            theta = [(thr - a_rep[j]) - (jnp.abs(thr) + jnp.abs(a_rep[j])) * TAU_GUARD for j in range(k)]
            for v in range(PEER_N_KEYS // SUBLANES):
                x1 = s1[SUBLANES * v:SUBLANES * (v + 1)]
                x2 = s2[SUBLANES * v:SUBLANES * (v + 1)]
                r1 = jnp.zeros_like(x1)
                ns = jnp.zeros_like(x2)
                for j in range(k):
                    r1 = jnp.where(a_rep[j] > x1, float(j + 1), r1)
                    ns = jnp.where(x2 >= theta[j], float(j + 1), ns)
                rank1.append(r1)
                n_sel.append(ns)
            rank_ref[h, :, cols] = jnp.concatenate(rank1, axis=0)
            alpha_ref[h, :, cols] = jnp.exp(s1 - a_rep[0][0:1]) * (0.5 / z)
            nsel_ref[h, lt] = jnp.concatenate(n_sel, axis=0).astype(BF16)
            beta_ref[h, lt] = jnp.exp(s2 - b_rep[0][0:1]).astype(BF16)
            return carry

        def tile_pair(p, carry):
            return tile(2 * p + 1, tile(2 * p, carry))

        return lax.fori_loop(0, lane_tiles // 2, tile_pair, carry)

    lax.fori_loop(0, PEER_HEADS, head, 0)


def _peer_route(hnt, wqt, keys, tb):
    d, t = hnt.shape
    nq = wqt.shape[0]
    oshape = jax.ShapeDtypeStruct((PEER_HEADS, PEER_N_KEYS, t), F32)
    ospec = pl.BlockSpec((PEER_HEADS, PEER_N_KEYS, tb), lambda i: (0, 0, i))
    tshape = jax.ShapeDtypeStruct((PEER_HEADS, t // LANES, PEER_N_KEYS, LANES), BF16)
    tspec = pl.BlockSpec((PEER_HEADS, tb // LANES, PEER_N_KEYS, LANES), lambda i: (0, i, 0, 0))
    return pl.pallas_call(
        _peer_route_kernel,
        grid=(t // tb,),
        in_specs=[pl.BlockSpec((d, tb), lambda i: (0, i)),
                  pl.BlockSpec((nq, d), lambda i: (0, 0)),
                  pl.BlockSpec(keys.shape, lambda i: (0, 0, 0, 0))],
        out_specs=[ospec, ospec, tspec, tspec],
        out_shape=[oshape, oshape, tshape, tshape],
        scratch_shapes=[pltpu.VMEM((nq, tb), BF16), pltpu.VMEM((2, tb // LANES, PEER_N_KEYS, LANES), F32)],
        compiler_params=_cparams(("parallel",)),
        name="peer_route",
    )(hnt, wqt, keys)


def _peer_dense_kernel(hnt_ref, u_ref, vt_ref, rank_ref, alpha_ref, nsel_ref, beta_ref, o_ref,
                       hid_ref, act_ref, *, te, tb, n_tiles):
    j = pl.program_id(1)
    cur = j % 2
    prev = 1 - cur
    rows_per_tile = te // PEER_N_KEYS
    lane_tiles = tb // LANES

    def hidden(hid_ref):
        hid = jnp.dot(u_ref[...], hnt_ref[...], preferred_element_type=F32)
        for lt in range(lane_tiles):
            hid_ref[lt] = hid[:, lt * LANES:(lt + 1) * LANES]

    def activations(hid_ref, act_ref):
        def lane_tile(lt, carry):
            cols = pl.ds(pl.multiple_of(lt * LANES, LANES), LANES)
            zero = jnp.zeros((PEER_N_KEYS, LANES), BF16)
            for r in range(rows_per_tile):
                w = None
                for h in range(PEER_HEADS):
                    sel = _bf16_row_tile(rank_ref[h, r:r + 1, cols], PEER_N_KEYS) < nsel_ref[h, lt]
                    term = jnp.where(sel, beta_ref[h, lt], zero) * _bf16_row_tile(alpha_ref[h, r:r + 1, cols],
                                                                                  PEER_N_KEYS)
                    w = term if w is None else w + term
                rows = slice(r * PEER_N_KEYS, (r + 1) * PEER_N_KEYS)
                hid_t = hid_ref[lt, rows, :]
                gelu2 = hid_t * (1.0 + lax.erf(hid_t * (2.0 ** -0.5)))
                act_ref[lt, rows, :] = gelu2.astype(BF16) * w
            return carry

        lax.fori_loop(0, lane_tiles, lane_tile, 0)

    def project(act_ref):
        act = jnp.concatenate([act_ref[lt] for lt in range(lane_tiles)], axis=1)
        o_ref[...] += jnp.dot(vt_ref[...], act, preferred_element_type=F32)

    def run(do_a, do_b, do_c):
        if do_a:
            hidden(hid_ref.at[cur])
        if do_b:
            activations(hid_ref.at[prev], act_ref.at[prev])
        if do_c:
            project(act_ref.at[cur])

    @pl.when(jnp.logical_and(j >= 2, j < n_tiles))
    def _():
        run(True, True, True)

    @pl.when(j == 0)
    def _():
        o_ref[...] = jnp.zeros(o_ref.shape, o_ref.dtype)
        run(True, False, False)

    @pl.when(j == 1)
    def _():
        run(True, True, False)

    @pl.when(j == n_tiles)
    def _():
        run(False, True, True)

    @pl.when(j == n_tiles + 1)
    def _():
        run(False, False, True)


def _peer_dense(hnt, u_bf16, v_bf16, rank1, alpha, n_sel, beta, tb, te):
    d, t = hnt.shape
    n_tiles = u_bf16.shape[0] // te
    assert n_tiles >= 2
    vt_tiles = v_bf16.reshape(n_tiles, te, d).transpose(0, 2, 1)
    kern = functools.partial(_peer_dense_kernel, te=te, tb=tb, n_tiles=n_tiles)
    last = n_tiles - 1
    rspec = pl.BlockSpec((PEER_HEADS, tb // LANES, PEER_N_KEYS, LANES), lambda i, j: (0, i, 0, 0))
    i1spec = pl.BlockSpec((PEER_HEADS, te // PEER_N_KEYS, tb), lambda i, j: (0, jnp.clip(j - 1, 0, last), i))
    stage_shape = (tb // LANES, te, LANES)
    return pl.pallas_call(
        kern,
        grid=(t // tb, n_tiles + 2),
        in_specs=[pl.BlockSpec((d, tb), lambda i, j: (0, i)),
                  pl.BlockSpec((te, d), lambda i, j: (jnp.minimum(j, last), 0)),
                  pl.BlockSpec((None, d, te), lambda i, j: (jnp.clip(j - 2, 0, last), 0, 0)),
                  i1spec, i1spec, rspec, rspec],
        out_specs=pl.BlockSpec((d, tb), lambda i, j: (0, i)),
        out_shape=jax.ShapeDtypeStruct((d, t), F32),
        scratch_shapes=[pltpu.VMEM((2,) + stage_shape, F32), pltpu.VMEM((2,) + stage_shape, BF16)],
        compiler_params=_cparams(("parallel", "arbitrary")),
        name="peer_dense",
    )(hnt, u_bf16, vt_tiles, rank1, alpha, n_sel, beta)


def _final_norm_kernel(h_ref, pt_ref, g_ref, o_ref):
    o_ref[...] = _rms(h_ref[...] + pt_ref[...].T, g_ref[...])


def _final_norm(h2, peer_t, g, tb):
    t, d = h2.shape
    return pl.pallas_call(
        _final_norm_kernel,
        grid=(t // tb,),
        in_specs=[pl.BlockSpec((tb, d), lambda i: (i, 0)),
                  pl.BlockSpec((d, tb), lambda i: (0, i)),
                  pl.BlockSpec((1, d), lambda i: (0, 0))],
        out_specs=pl.BlockSpec((tb, d), lambda i: (i, 0)),
        out_shape=jax.ShapeDtypeStruct((t, d), F32),
        compiler_params=_cparams(("parallel",)),
        name="final_norm",
    )(h2, peer_t, g)


def _largest_divisor(n, cap, multiple):
    best = multiple
    k = multiple
    while k <= min(n, cap):
        if n % k == 0:
            best = k
        k += multiple
    return best


def kernel(x, norm1_g, w_in, diff_lambda_q1, diff_lambda_k1, diff_lambda_q2, diff_lambda_k2, diff_subln_g,
           rel_bias_table, hgrn_lb_logits, hgrn_gnorm_g, w_out, norm2_g, peer_w_q, peer_sub_keys, peer_u,
           peer_v, final_norm_g):
    batch, seq, d = x.shape
    t = batch * seq
    assert d == D_MODEL and seq % 256 == 0 and w_in.shape[0] == 1
    tb_proj = _largest_divisor(t, 512, 256)
    tq = 256
    tb_route = _largest_divisor(t, 512, 256)
    tb_peer = _largest_divisor(t, 1024, 512)
    te_peer = 1024

    x2 = x.reshape(t, d)
    row = lambda v: v.reshape(1, -1).astype(F32)

    lam = (jnp.exp(jnp.sum(diff_lambda_q1[0].astype(F32) * diff_lambda_k1[0].astype(F32)))
           - jnp.exp(jnp.sum(diff_lambda_q2[0].astype(F32) * diff_lambda_k2[0].astype(F32)))
           + LAM_INIT).reshape(1)
    lb = jnp.cumsum(jax.nn.softmax(hgrn_lb_logits.astype(F32), axis=0), axis=0)[0]
    lb = lb.reshape(HGRN_HEADS, 1, HEAD_W)
    table = rel_bias_table.astype(F32)

    q_t, k, v_t, hin = _norm_proj(x2, row(norm1_g[0]), w_in[0].astype(BF16), tb_proj)
    bias = _bias_tiles(table, tq)
    a = _diff_attn(lam, table, q_t, k, v_t, bias, row(diff_subln_g[0]), batch, seq, tq)
    o = _hgrn(hin.reshape(batch, seq, HGRN_COLS), lb, row(hgrn_gnorm_g[0]), batch, seq)
    w_o = w_out[0].astype(BF16)
    h2, hnt = _out_proj(x2, a.reshape(t, DIFF_WIDTH), o.reshape(t, HGRN_WIDTH),
                        w_o[:DIFF_WIDTH], w_o[DIFF_WIDTH:], row(norm2_g[0]), tb_proj)
    rank1, alpha, n_sel, beta = _peer_route(hnt, peer_w_q[0].T.astype(BF16), peer_sub_keys[0].astype(BF16), tb_route)
    peer_t = _peer_dense(hnt, peer_u[0].astype(BF16), peer_v[0].astype(BF16), rank1, alpha, n_sel, beta,
                         tb_peer, te_peer)
    out = _final_norm(h2, peer_t, row(final_norm_g), tb_proj)
    return out.reshape(batch, seq, d)
```

```python
import functools
import math

import numpy as np
import jax
import jax.numpy as jnp
from jax import lax
from jax.experimental import pallas as pl
from jax.experimental.pallas import tpu as pltpu

F32 = jnp.float32
BF16 = jnp.bfloat16

D_MODEL = 1024
DIFF_HEADS = 4
DIFF_QK_DIM = 64
LANES = 128
HEAD_W = 128
DIFF_WIDTH = DIFF_HEADS * HEAD_W
HGRN_HEADS = 4
HGRN_WIDTH = HGRN_HEADS * HEAD_W
HGRN_CHUNK = 32
ATTN_COLS = 3 * DIFF_WIDTH
HGRN_COLS = 4 * HGRN_WIDTH
REL_BUCKETS = 32
REL_MAX_DIST = 128
PEER_HEADS = 8
PEER_N_KEYS = 128
PEER_QUERY_DIM = 256
PEER_TOPK = 16
NORM_EPS = 1e-6
LAM_INIT = 0.8 - 0.6 * math.exp(-0.3 * 0)
NEG_BIG = -1e30

VMEM_LIMIT_BYTES = 56 * 1024 * 1024


def _cparams(sem):
    return pltpu.CompilerParams(dimension_semantics=sem, vmem_limit_bytes=VMEM_LIMIT_BYTES)


def _rms(x, g):
    return x * lax.rsqrt(jnp.mean(x * x, axis=-1, keepdims=True) + NORM_EPS) * g


def _norm_proj_kernel(x_ref, g_ref, w_ref, qt_ref, k_ref, vt_ref, oh_ref):
    y = _rms(x_ref[...], g_ref[...])
    p = jnp.dot(y.astype(BF16), w_ref[...], preferred_element_type=F32)
    qt_ref[...] = p[:, :DIFF_WIDTH].T.astype(BF16)
    k_ref[...] = p[:, DIFF_WIDTH:2 * DIFF_WIDTH].astype(BF16)
    vt_ref[...] = p[:, 2 * DIFF_WIDTH:ATTN_COLS].T.astype(BF16)
    oh_ref[...] = p[:, ATTN_COLS:]


def _norm_proj(x2, g, w_bf16, tb):
    t, d = x2.shape
    n = w_bf16.shape[1]
    return pl.pallas_call(
        _norm_proj_kernel,
        grid=(t // tb,),
        in_specs=[pl.BlockSpec((tb, d), lambda i: (i, 0)),
                  pl.BlockSpec((1, d), lambda i: (0, 0)),
                  pl.BlockSpec((d, n), lambda i: (0, 0))],
        out_specs=[pl.BlockSpec((DIFF_WIDTH, tb), lambda i: (0, i)),
                   pl.BlockSpec((tb, DIFF_WIDTH), lambda i: (i, 0)),
                   pl.BlockSpec((DIFF_WIDTH, tb), lambda i: (0, i)),
                   pl.BlockSpec((tb, HGRN_COLS), lambda i: (i, 0))],
        out_shape=[jax.ShapeDtypeStruct((DIFF_WIDTH, t), BF16),
                   jax.ShapeDtypeStruct((t, DIFF_WIDTH), BF16),
                   jax.ShapeDtypeStruct((DIFF_WIDTH, t), BF16),
                   jax.ShapeDtypeStruct((t, HGRN_COLS), F32)],
        compiler_params=_cparams(("parallel",)),
        name="norm_proj",
    )(x2, g, w_bf16)


def _t5_bucket_np(n):
    n = np.maximum(n, 0).astype(np.int32)
    max_exact = REL_BUCKETS // 2
    nf = np.maximum(n, max_exact).astype(np.float32)
    large = max_exact + (np.log(nf / np.float32(max_exact)) / np.float32(math.log(REL_MAX_DIST / max_exact))
                         * np.float32(REL_BUCKETS - max_exact)).astype(np.int32)
    large = np.minimum(large, REL_BUCKETS - 1)
    return np.where(n < max_exact, n, large).astype(np.int32)


def _bias_tiles_kernel(table_ref, bucket_ref, o_ref):
    h = pl.program_id(0)
    bucket = bucket_ref[...]
    acc = jnp.zeros(bucket.shape, F32)
    for b in range(REL_BUCKETS):
        acc = jnp.where(bucket == b, table_ref[b, h], acc)
    o_ref[...] = acc


def _bias_tiles(table, tq):
    c = np.arange(tq)[:, None]
    r = np.arange(tq)[None, :]
    tiles = np.stack([_t5_bucket_np(d * tq + r - c) for d in (0, 1)])
    tiles = np.concatenate([tiles, tiles], axis=2)
    return pl.pallas_call(
        _bias_tiles_kernel,
        grid=(DIFF_HEADS,),
        in_specs=[pl.BlockSpec(memory_space=pltpu.SMEM),
                  pl.BlockSpec((2, tq, 2 * tq), lambda h: (0, 0, 0))],
        out_specs=pl.BlockSpec((None, 2, tq, 2 * tq), lambda h: (h, 0, 0, 0)),
        out_shape=jax.ShapeDtypeStruct((DIFF_HEADS, 2, tq, 2 * tq), F32),
        compiler_params=_cparams(("arbitrary",)),
        name="bias_tiles",
    )(table, jnp.asarray(tiles))


def _diff_attn_kernel(lam_ref, table_ref, qt_ref, k_ref, vt_ref, bias_ref, g_ref, o_ref,
                      s_ref, mx_ref, l_ref, acc_ref, *, tq):
    h0 = pl.program_id(1) * ATTN_HEADS_PER_STEP
    i = pl.program_id(2)
    heads = range(ATTN_HEADS_PER_STEP)
    hrows = lambda hh: slice(hh * HEAD_W, (hh + 1) * HEAD_W)
    q2t = []
    for hh in heads:
        qt = qt_ref[hrows(hh), :] * jnp.asarray(DIFF_QK_DIM ** -0.5, BF16)
        row = lax.broadcasted_iota(jnp.int32, qt.shape, 0)
        zero = jnp.zeros_like(qt)
        q2t.append(jnp.concatenate([jnp.where(row < DIFF_QK_DIM, qt, zero),
                                    jnp.where(row >= DIFF_QK_DIM, qt, zero)], axis=1))

    def fold(op, x):
        return functools.reduce(op, [x[SUBLANES * k:SUBLANES * (k + 1)] for k in range(x.shape[0] // SUBLANES)])

    def scores(j, n_keys, delta, causal):
        start = pl.multiple_of(j * tq, tq)
        for hh in heads:
            bias = table_ref[REL_BUCKETS - 1, h0 + hh] if delta is None else bias_ref[hh, delta]
            s = jnp.dot(k_ref[pl.ds(start, n_keys), hrows(hh)], q2t[hh], preferred_element_type=F32) + bias
            if causal:
                c = lax.broadcasted_iota(jnp.int32, s.shape, 0)
                r = lax.broadcasted_iota(jnp.int32, s.shape, 1) % tq
                s = jnp.where(c <= r, s, NEG_BIG)
            for k in range(n_keys // tq):
                s_ref[hh, j + k] = s[k * tq:(k + 1) * tq]
            mx_ref[hh] = jnp.maximum(mx_ref[hh], fold(jnp.maximum, s))

    mx_ref[...] = jnp.full(mx_ref.shape, NEG_BIG, F32)
    l_ref[...] = jnp.zeros(l_ref.shape, F32)
    acc_ref[...] = jnp.zeros(acc_ref.shape, F32)

    def far_pair(p, carry):
        scores(2 * p, 2 * tq, None, False)
        return carry

    n_far = jnp.maximum(i - 1, 0)
    lax.fori_loop(0, n_far // 2, far_pair, 0)

    @pl.when(n_far % 2 == 1)
    def _():
        scores(n_far - 1, tq, None, False)

    @pl.when(i >= 1)
    def _():
        scores(i - 1, tq, 1, False)

    scores(i, tq, 0, True)
    m = [jnp.max(mx_ref[hh], axis=0, keepdims=True) for hh in heads]

    def weigh(j, n_keys):
        start = pl.multiple_of(j * tq, tq)
        for hh in heads:
            s = jnp.concatenate([s_ref[hh, j + k] for k in range(n_keys // tq)], axis=0)
            e = jnp.exp(s - m[hh])
            l_ref[hh] += fold(jnp.add, e)
            acc_ref[hh] += jnp.dot(vt_ref[hrows(hh), pl.ds(start, n_keys)], e.astype(BF16),
                                   preferred_element_type=F32)

    def weigh_pair(p, carry):
        weigh(2 * p, 2 * tq)
        return carry

    n_blocks = i + 1
    lax.fori_loop(0, n_blocks // 2, weigh_pair, 0)

    @pl.when(n_blocks % 2 == 1)
    def _():
        weigh(i, tq)

    lv = lam_ref[...]
    lam = (jnp.exp(jnp.sum(lv[0:1] * lv[1:2], axis=-1, keepdims=True))
           - jnp.exp(jnp.sum(lv[2:3] * lv[3:4], axis=-1, keepdims=True)) + LAM_INIT)
    for hh in heads:
        l = jnp.sum(l_ref[hh], axis=0, keepdims=True)
        acc = acc_ref[hh]
        out_t = acc[:, :tq] / l[:, :tq] - lam * (acc[:, tq:] / l[:, tq:])
        o_ref[:, hrows(hh)] = (_rms(out_t.T, g_ref[...]) * (1.0 - LAM_INIT)).astype(o_ref.dtype)


def _diff_attn(lam, table, q_t, k, v_t, bias, g, batch, seq, tq):
    kern = functools.partial(_diff_attn_kernel, tq=tq)
    nq = seq // tq
    hp = ATTN_HEADS_PER_STEP
    hw = hp * HEAD_W
    return pl.pallas_call(
        kern,
        grid=(batch, DIFF_HEADS // hp, nq),
        in_specs=[pl.BlockSpec(lam.shape, lambda b, h, i: (0, 0)),
                  pl.BlockSpec(memory_space=pltpu.SMEM),
                  pl.BlockSpec((hw, tq), lambda b, h, i: (h, b * nq + i)),
                  pl.BlockSpec((seq, hw), lambda b, h, i: (b, h)),
                  pl.BlockSpec((hw, seq), lambda b, h, i: (h, b)),
                  pl.BlockSpec((hp, 2, tq, 2 * tq), lambda b, h, i: (h, 0, 0, 0)),
                  pl.BlockSpec((1, HEAD_W), lambda b, h, i: (0, 0))],
        out_specs=pl.BlockSpec((None, tq, hw), lambda b, h, i: (b, i, h)),
        out_shape=jax.ShapeDtypeStruct((batch, seq, DIFF_WIDTH), BF16),
        scratch_shapes=[pltpu.VMEM((hp, seq // tq, tq, 2 * tq), F32),
                        pltpu.VMEM((hp, SUBLANES, 2 * tq), F32),
                        pltpu.VMEM((hp, SUBLANES, 2 * tq), F32),
                        pltpu.VMEM((hp, HEAD_W, 2 * tq), F32)],
        compiler_params=_cparams(("parallel", "parallel", "arbitrary")),
        name="diff_attn",
    )(lam, table, q_t, k, v_t, bias, g)


def _hgrn_kernel(hq_ref, hf_ref, hi_ref, hg_ref, lb_ref, g_ref, o_ref,
                 qd_ref, kd_ref, cd_ref, oi_ref, upd_ref, prev_ref, *, seq):
    c_len = HGRN_CHUNK
    n_chunks = seq // c_len
    logits = lb_ref[...]
    ex = jnp.exp(logits - jnp.max(logits, axis=0, keepdims=True))
    lb = ex[0:1] / jnp.sum(ex, axis=0, keepdims=True)
    f = lb + (1.0 - lb) * jax.nn.sigmoid(hf_ref[...])
    log_f = jnp.log(f)
    kk = 1.0 - f
    hq = hq_ref[...]
    qq = hq * jax.nn.sigmoid(hq)
    vv = hi_ref[...]

    pos = lax.broadcasted_iota(jnp.int32, (seq, HEAD_W), 0) % c_len
    b = log_f
    sh = 1
    while sh < c_len:
        b = b + jnp.where(pos >= sh, pltpu.roll(b, sh, axis=0), 0.0)
        sh *= 2
    b3 = b.reshape(n_chunks, c_len, HEAD_W)
    b_last = b3[:, c_len - 1:c_len, :]
    q_dec = qq * jnp.exp(b)
    k_inv = kk * jnp.exp(-b)
    k_dec = kk * jnp.exp(b_last - b3).reshape(seq, HEAD_W)
    qd_ref[...] = q_dec.astype(BF16)
    kd_ref[...] = k_dec.astype(BF16)
    cd_ref[...] = jnp.exp(b_last.reshape(n_chunks, HEAD_W))

    blk = HEAD_W
    nb = seq // blk
    qd_b = q_dec.astype(BF16).reshape(nb, blk, HEAD_W)
    ki_b = k_inv.astype(BF16).reshape(nb, blk, HEAD_W)
    scores = jnp.einsum('nqd,nkd->nqk', qd_b, ki_b, preferred_element_type=F32)
    r = lax.broadcasted_iota(jnp.int32, (blk, blk), 0)
    c = lax.broadcasted_iota(jnp.int32, (blk, blk), 1)
    keep = (c <= r) & ((r // c_len) == (c // c_len))
    scores = jnp.where(keep[None], scores, 0.0)
    o_intra = jnp.einsum('nqk,nkv->nqv', scores.astype(BF16), vv.astype(BF16).reshape(nb, blk, HEAD_W),
                         preferred_element_type=F32).reshape(seq, HEAD_W)

    unroll = 16

    def rows_of(ci):
        return pl.ds(pl.multiple_of(ci * c_len, c_len), c_len)

    def updates(g, carry):
        for u in range(unroll):
            ci = g * unroll + u
            vc = hi_ref[rows_of(ci), :].astype(BF16)
            upd_ref[ci] = lax.dot_general(vc, kd_ref[rows_of(ci), :], (((0,), (0,)), ((), ())),
                                          preferred_element_type=F32)
        return carry

    lax.fori_loop(0, n_chunks // unroll, updates, 0)

    def recur(ci, st):
        prev_ref[ci] = st.astype(BF16)
        return st * cd_ref[pl.ds(ci, 1), :] + upd_ref[ci]

    lax.fori_loop(0, n_chunks, recur, jnp.zeros((HEAD_W, HEAD_W), F32))

    def reads(g, carry):
        for u in range(unroll):
            ci = g * unroll + u
            oi_ref[rows_of(ci), :] = lax.dot_general(qd_ref[rows_of(ci), :], prev_ref[ci],
                                                     (((1,), (1,)), ((), ())), preferred_element_type=F32)
        return carry

    lax.fori_loop(0, n_chunks // unroll, reads, 0)

    o = o_intra + oi_ref[...]
    hg = hg_ref[...]
    o_ref[...] = (_rms(o, g_ref[...]) * (hg * jax.nn.sigmoid(hg))).astype(o_ref.dtype)


def _hgrn(hin, lb, g, batch, seq):
    kern = functools.partial(_hgrn_kernel, seq=seq)
    col = lambda k: (lambda b, h: (b, 0, k * HGRN_HEADS + h))
    blk = (None, seq, HEAD_W)
    return pl.pallas_call(
        kern,
        grid=(batch, HGRN_HEADS),
        in_specs=[pl.BlockSpec(blk, col(0)), pl.BlockSpec(blk, col(1)),
                  pl.BlockSpec(blk, col(2)), pl.BlockSpec(blk, col(3)),
                  pl.BlockSpec((lb.shape[0], HEAD_W), lambda b, h: (0, h)),
                  pl.BlockSpec((1, HEAD_W), lambda b, h: (0, 0))],
        out_specs=pl.BlockSpec(blk, lambda b, h: (b, 0, h)),
        out_shape=jax.ShapeDtypeStruct((batch, seq, HGRN_WIDTH), BF16),
        scratch_shapes=[pltpu.VMEM((seq, HEAD_W), BF16), pltpu.VMEM((seq, HEAD_W), BF16),
                        pltpu.VMEM((seq // HGRN_CHUNK, HEAD_W), F32), pltpu.VMEM((seq, HEAD_W), F32),
                        pltpu.VMEM((seq // HGRN_CHUNK, HEAD_W, HEAD_W), F32),
                        pltpu.VMEM((seq // HGRN_CHUNK, HEAD_W, HEAD_W), BF16)],
        compiler_params=_cparams(("parallel", "parallel")),
        name="hgrn",
    )(hin, hin, hin, hin, lb, g)


def _out_proj_kernel(x_ref, a_ref, o_ref, wa_ref, wo_ref, g_ref, h_ref, hnt_ref):
    h = (x_ref[...]
         + jnp.dot(a_ref[...], wa_ref[...], preferred_element_type=F32)
         + jnp.dot(o_ref[...], wo_ref[...], preferred_element_type=F32))
    h_ref[...] = h
    hnt_ref[...] = _rms(h, g_ref[...]).T.astype(BF16)


def _out_proj(x2, a2, o2, wa, wo, g, tb):
    t, d = x2.shape
    return pl.pallas_call(
        _out_proj_kernel,
        grid=(t // tb,),
        in_specs=[pl.BlockSpec((tb, d), lambda i: (i, 0)),
                  pl.BlockSpec((tb, DIFF_WIDTH), lambda i: (i, 0)),
                  pl.BlockSpec((tb, HGRN_WIDTH), lambda i: (i, 0)),
                  pl.BlockSpec((DIFF_WIDTH, d), lambda i: (0, 0)),
                  pl.BlockSpec((HGRN_WIDTH, d), lambda i: (0, 0)),
                  pl.BlockSpec((1, d), lambda i: (0, 0))],
        out_specs=[pl.BlockSpec((tb, d), lambda i: (i, 0)),
                   pl.BlockSpec((d, tb), lambda i: (0, i))],
        out_shape=[jax.ShapeDtypeStruct((t, d), F32),
                   jax.ShapeDtypeStruct((d, t), BF16)],
        compiler_params=_cparams(("parallel",)),
        name="out_proj",
    )(x2, a2, o2, wa, wo, g)


SUBLANES = 8
ATTN_HEADS_PER_STEP = 4
TAU_GUARD = 2.0 ** -21


def _oddeven_merge(lo, hi, r):
    step = r * 2
    if step < hi - lo:
        yield from _oddeven_merge(lo, hi, step)
        yield from _oddeven_merge(lo + r, hi, step)
        yield from [(i, i + r) for i in range(lo + r, hi - r, step)]
    else:
        yield (lo, lo + r)


def _oddeven_merge_sort(lo, hi):
    if hi - lo >= 1:
        mid = lo + (hi - lo) // 2
        yield from _oddeven_merge_sort(lo, mid)
        yield from _oddeven_merge_sort(mid + 1, hi)
        yield from _oddeven_merge(lo, hi, 1)


def _bitonic_merge(n):
    d = n // 2
    while d >= 1:
        yield from [(i, i + d) for i in range(n) if (i // d) % 2 == 0]
        d //= 2


SORT16 = tuple(_oddeven_merge_sort(0, PEER_TOPK - 1))
MERGE16 = tuple(_bitonic_merge(PEER_TOPK))


def _bf16_row_tile(row, n_rows):
    packed_rows = 16
    tile16 = jnp.broadcast_to(row, (packed_rows, row.shape[1])).astype(BF16)
    return jnp.broadcast_to(tile16[None], (n_rows // packed_rows, packed_rows, row.shape[1])
                            ).reshape(n_rows, row.shape[1])


def _compare_exchange(v, pairs):
    for i, j in pairs:
        v[i], v[j] = jnp.maximum(v[i], v[j]), jnp.minimum(v[i], v[j])


def _top16_sorted(x):
    v = [x[SUBLANES * k:SUBLANES * (k + 1)] for k in range(PEER_TOPK)]
    _compare_exchange(v, SORT16)
    shift = SUBLANES // 2
    while shift >= 1:
        v = [jnp.maximum(v[k], pltpu.roll(v[PEER_TOPK - 1 - k], shift, axis=0)) for k in range(PEER_TOPK)]
        _compare_exchange(v, MERGE16)
        shift //= 2
    return v


def _rows_on_sublanes(rep, start):
    sub = lax.broadcasted_iota(jnp.int32, rep[0].shape, 0)
    out = rep[start]
    for s in range(1, SUBLANES):
        out = jnp.where(sub == s, rep[start + s], out)
    return out


def _pair_candidates(a_rep, b_rep):
    k = PEER_TOPK
    sub = lax.broadcasted_iota(jnp.int32, a_rep[0].shape, 0)
    b_lo = _rows_on_sublanes(b_rep, 0)
    b_hi = _rows_on_sublanes(b_rep, SUBLANES)
    a_hi = _rows_on_sublanes(a_rep, SUBLANES)
    pieces = [a_rep[0] + b_lo, a_rep[0] + b_hi, a_rep[1] + b_lo]
    for i in range(2, SUBLANES):
        pieces.append(jnp.where(sub < k // (i + 1), a_rep[i] + b_lo, -jnp.inf))
    pieces.append(a_hi + b_rep[0])
    return jnp.concatenate(pieces, axis=0)


def _kth_largest(x, k):
    for r in range(k):
        m = jnp.max(x, axis=0, keepdims=True)
        if r + 1 < k:
            x = jnp.where(x == m, -jnp.inf, x)
    return m


def _peer_route_kernel(hnt_ref, wqt_ref, keys_ref, rank_ref, alpha_ref, nsel_ref, beta_ref, qt_ref, s_ref):
    qt_ref[...] = jnp.dot(wqt_ref[...], hnt_ref[...], preferred_element_type=F32).astype(BF16)
    half = PEER_QUERY_DIM // 2
    lane_tiles = hnt_ref.shape[1] // LANES
    k = PEER_TOPK

    def head(h, carry):
        base = pl.multiple_of(h * PEER_QUERY_DIM, PEER_QUERY_DIM)
        for c in range(2):
            s = jnp.dot(keys_ref[h, c], qt_ref[pl.ds(base + c * half, half), :], preferred_element_type=F32)
            for lt in range(lane_tiles):
                s_ref[c, lt] = s[:, lt * LANES:(lt + 1) * LANES]

        def tile(lt, carry):
            cols = pl.ds(pl.multiple_of(lt * LANES, LANES), LANES)
            s1 = s_ref[0, lt]
            s2 = s_ref[1, lt]
            a_rep = _top16_sorted(s1)
            b_rep = _top16_sorted(s2)
            cands = _pair_candidates(a_rep, b_rep)
            thr = _kth_largest(cands, k)
            top = a_rep[0][0:1] + b_rep[0][0:1]
            z = jnp.sum(jnp.where(cands >= thr, jnp.exp(cands - top), 0.0), axis=0, keepdims=True)
            rank1 = []
            n_sel = []
            theta = [(thr - a_rep[j]) - (jnp.abs(thr) + jnp.abs(a_rep[j])) * TAU_GUARD for j in range(k)]
            for v in range(PEER_N_KEYS // SUBLANES):
                x1 = s1[SUBLANES * v:SUBLANES * (v + 1)]
                x2 = s2[SUBLANES * v:SUBLANES * (v + 1)]
                r1 = jnp.zeros_like(x1)
                ns = jnp.zeros_like(x2)
                for j in range(k):
                    r1 = jnp.where(a_rep[j] > x1, float(j + 1), r1)
                    ns = jnp.where(x2 >= theta[j], float(j + 1), ns)
                rank1.append(r1)
                n_sel.append(ns)
            rank_ref[h, :, cols] = jnp.concatenate(rank1, axis=0)
            alpha_ref[h, :, cols] = jnp.exp(s1 - a_rep[0][0:1]) * (0.5 / z)
            nsel_ref[h, lt] = jnp.concatenate(n_sel, axis=0).astype(BF16)
            beta_ref[h, lt] = jnp.exp(s2 - b_rep[0][0:1]).astype(BF16)
            return carry

        def tile_pair(p, carry):
            return tile(2 * p + 1, tile(2 * p, carry))

        return lax.fori_loop(0, lane_tiles // 2, tile_pair, carry)

    lax.fori_loop(0, PEER_HEADS, head, 0)


def _peer_route(hnt, wqt, keys, tb):
    d, t = hnt.shape
    nq = wqt.shape[0]
    oshape = jax.ShapeDtypeStruct((PEER_HEADS, PEER_N_KEYS, t), F32)
    ospec = pl.BlockSpec((PEER_HEADS, PEER_N_KEYS, tb), lambda i: (0, 0, i))
    tshape = jax.ShapeDtypeStruct((PEER_HEADS, t // LANES, PEER_N_KEYS, LANES), BF16)
    tspec = pl.BlockSpec((PEER_HEADS, tb // LANES, PEER_N_KEYS, LANES), lambda i: (0, i, 0, 0))
    return pl.pallas_call(
        _peer_route_kernel,
        grid=(t // tb,),
        in_specs=[pl.BlockSpec((d, tb), lambda i: (0, i)),
                  pl.BlockSpec((nq, d), lambda i: (0, 0)),
                  pl.BlockSpec(keys.shape, lambda i: (0, 0, 0, 0))],
        out_specs=[ospec, ospec, tspec, tspec],
        out_shape=[oshape, oshape, tshape, tshape],
        scratch_shapes=[pltpu.VMEM((nq, tb), BF16), pltpu.VMEM((2, tb // LANES, PEER_N_KEYS, LANES), F32)],
        compiler_params=_cparams(("parallel",)),
        name="peer_route",
    )(hnt, wqt, keys)


def _peer_dense_kernel(hnt_ref, u_ref, vt_ref, rank_ref, alpha_ref, nsel_ref, beta_ref, o_ref,
                       hid_ref, act_ref, *, te, tb, n_tiles):
    j = pl.program_id(1)
    cur = j % 2
    prev = 1 - cur
    rows_per_tile = te // PEER_N_KEYS
    lane_tiles = tb // LANES

    def hidden(hid_ref):
        hid = jnp.dot(u_ref[...], hnt_ref[...], preferred_element_type=F32)
        for lt in range(lane_tiles):
            hid_ref[lt] = hid[:, lt * LANES:(lt + 1) * LANES]

    def activations(hid_ref, act_ref):
        def lane_tile(lt, carry):
            cols = pl.ds(pl.multiple_of(lt * LANES, LANES), LANES)
            zero = jnp.zeros((PEER_N_KEYS, LANES), BF16)
            for r in range(rows_per_tile):
                w = None
                for h in range(PEER_HEADS):
                    sel = _bf16_row_tile(rank_ref[h, r:r + 1, cols], PEER_N_KEYS) < nsel_ref[h, lt]
                    term = jnp.where(sel, beta_ref[h, lt], zero) * _bf16_row_tile(alpha_ref[h, r:r + 1, cols],
                                                                                  PEER_N_KEYS)
                    w = term if w is None else w + term
                rows = slice(r * PEER_N_KEYS, (r + 1) * PEER_N_KEYS)
                hid_t = hid_ref[lt, rows, :]
                gelu2 = hid_t * (1.0 + lax.erf(hid_t * (2.0 ** -0.5)))
                act_ref[lt, rows, :] = gelu2.astype(BF16) * w
            return carry

        lax.fori_loop(0, lane_tiles, lane_tile, 0)

    def project(act_ref):
        act = jnp.concatenate([act_ref[lt] for lt in range(lane_tiles)], axis=1)
        o_ref[...] += jnp.dot(vt_ref[...], act, preferred_element_type=F32)

    def run(do_a, do_b, do_c):
        if do_a:
            hidden(hid_ref.at[cur])
        if do_b:
            activations(hid_ref.at[prev], act_ref.at[prev])
        if do_c:
            project(act_ref.at[cur])

    @pl.when(jnp.logical_and(j >= 2, j < n_tiles))
    def _():
        run(True, True, True)

    @pl.when(j == 0)
    def _():
        o_ref[...] = jnp.zeros(o_ref.shape, o_ref.dtype)
        run(True, False, False)

    @pl.when(j == 1)
    def _():
        run(True, True, False)

    @pl.when(j == n_tiles)
    def _():
        run(False, True, True)

    @pl.when(j == n_tiles + 1)
    def _():
        run(False, False, True)


def _peer_dense(hnt, u_bf16, v_bf16, rank1, alpha, n_sel, beta, tb, te):
    d, t = hnt.shape
    n_tiles = u_bf16.shape[0] // te
    assert n_tiles >= 2
    vt_tiles = v_bf16.reshape(n_tiles, te, d).transpose(0, 2, 1)
    kern = functools.partial(_peer_dense_kernel, te=te, tb=tb, n_tiles=n_tiles)
    last = n_tiles - 1
    rspec = pl.BlockSpec((PEER_HEADS, tb // LANES, PEER_N_KEYS, LANES), lambda i, j: (0, i, 0, 0))
    i1spec = pl.BlockSpec((PEER_HEADS, te // PEER_N_KEYS, tb), lambda i, j: (0, jnp.clip(j - 1, 0, last), i))
    stage_shape = (tb // LANES, te, LANES)
    return pl.pallas_call(
        kern,
        grid=(t // tb, n_tiles + 2),
        in_specs=[pl.BlockSpec((d, tb), lambda i, j: (0, i)),
                  pl.BlockSpec((te, d), lambda i, j: (jnp.minimum(j, last), 0)),
                  pl.BlockSpec((None, d, te), lambda i, j: (jnp.clip(j - 2, 0, last), 0, 0)),
                  i1spec, i1spec, rspec, rspec],
        out_specs=pl.BlockSpec((d, tb), lambda i, j: (0, i)),
        out_shape=jax.ShapeDtypeStruct((d, t), F32),
        scratch_shapes=[pltpu.VMEM((2,) + stage_shape, F32), pltpu.VMEM((2,) + stage_shape, BF16)],
        compiler_params=_cparams(("parallel", "arbitrary")),
        name="peer_dense",
    )(hnt, u_bf16, vt_tiles, rank1, alpha, n_sel, beta)


def _final_norm_kernel(h_ref, pt_ref, g_ref, o_ref):
    o_ref[...] = _rms(h_ref[...] + pt_ref[...].T, g_ref[...])


def _final_norm(h2, peer_t, g, tb):
    t, d = h2.shape
    return pl.pallas_call(
        _final_norm_kernel,
        grid=(t // tb,),
        in_specs=[pl.BlockSpec((tb, d), lambda i: (i, 0)),
                  pl.BlockSpec((d, tb), lambda i: (0, i)),
                  pl.BlockSpec((1, d), lambda i: (0, 0))],
        out_specs=pl.BlockSpec((tb, d), lambda i: (i, 0)),
        out_shape=jax.ShapeDtypeStruct((t, d), F32),
        compiler_params=_cparams(("parallel",)),
        name="final_norm",
    )(h2, peer_t, g)


def _largest_divisor(n, cap, multiple):
    best = multiple
    k = multiple
    while k <= min(n, cap):
        if n % k == 0:
            best = k
        k += multiple
    return best


def kernel(x, norm1_g, w_in, diff_lambda_q1, diff_lambda_k1, diff_lambda_q2, diff_lambda_k2, diff_subln_g,
           rel_bias_table, hgrn_lb_logits, hgrn_gnorm_g, w_out, norm2_g, peer_w_q, peer_sub_keys, peer_u,
           peer_v, final_norm_g):
    batch, seq, d = x.shape
    t = batch * seq
    assert d == D_MODEL and seq % 256 == 0 and w_in.shape[0] == 1
    tb_proj = _largest_divisor(t, 512, 256)
    tq = 256
    tb_route = _largest_divisor(t, 512, 256)
    tb_peer = _largest_divisor(t, 1024, 512)
    te_peer = 1024

    x2 = x.reshape(t, d)
    row = lambda v: v.reshape(1, -1).astype(F32)

    lam = jnp.stack([diff_lambda_q1[0], diff_lambda_k1[0], diff_lambda_q2[0], diff_lambda_k2[0]]).astype(F32)
    lb = hgrn_lb_logits.astype(F32)
    table = rel_bias_table.astype(F32)

    q_t, k, v_t, hin = _norm_proj(x2, row(norm1_g[0]), w_in[0].astype(BF16), tb_proj)
    bias = _bias_tiles(table, tq)
    a = _diff_attn(lam, table, q_t, k, v_t, bias, row(diff_subln_g[0]), batch, seq, tq)
    o = _hgrn(hin.reshape(batch, seq, HGRN_COLS), lb, row(hgrn_gnorm_g[0]), batch, seq)
    w_o = w_out[0].astype(BF16)
    h2, hnt = _out_proj(x2, a.reshape(t, DIFF_WIDTH), o.reshape(t, HGRN_WIDTH),
                        w_o[:DIFF_WIDTH], w_o[DIFF_WIDTH:], row(norm2_g[0]), tb_proj)
    rank1, alpha, n_sel, beta = _peer_route(hnt, peer_w_q[0].T.astype(BF16), peer_sub_keys[0].astype(BF16), tb_route)
    peer_t = _peer_dense(hnt, peer_u[0].astype(BF16), peer_v[0].astype(BF16), rank1, alpha, n_sel, beta,
                         tb_peer, te_peer)
    out = _final_norm(h2, peer_t, row(final_norm_g), tb_proj)
    return out.reshape(batch, seq, d)
```

```python
import functools
import math

import numpy as np
import jax
import jax.numpy as jnp
from jax import lax
from jax.experimental import pallas as pl
from jax.experimental.pallas import tpu as pltpu

F32 = jnp.float32
BF16 = jnp.bfloat16

D_MODEL = 1024
DIFF_HEADS = 4
DIFF_QK_DIM = 64
LANES = 128
HEAD_W = 128
DIFF_WIDTH = DIFF_HEADS * HEAD_W
HGRN_HEADS = 4
HGRN_WIDTH = HGRN_HEADS * HEAD_W
HGRN_CHUNK = 32
ATTN_COLS = 3 * DIFF_WIDTH
HGRN_COLS = 4 * HGRN_WIDTH
REL_BUCKETS = 32
REL_MAX_DIST = 128
PEER_HEADS = 8
PEER_N_KEYS = 128
PEER_QUERY_DIM = 256
PEER_TOPK = 16
NORM_EPS = 1e-6
LAM_INIT = 0.8 - 0.6 * math.exp(-0.3 * 0)
NEG_BIG = -1e30

VMEM_LIMIT_BYTES = 56 * 1024 * 1024


def _cparams(sem):
    return pltpu.CompilerParams(dimension_semantics=sem, vmem_limit_bytes=VMEM_LIMIT_BYTES)


def _rms(x, g):
    return x * lax.rsqrt(jnp.mean(x * x, axis=-1, keepdims=True) + NORM_EPS) * g


def _norm_proj_kernel(x_ref, g_ref, w_ref, qt_ref, k_ref, vt_ref, oh_ref):
    y = _rms(x_ref[...], g_ref[...])
    p = jnp.dot(y.astype(BF16), w_ref[...], preferred_element_type=F32)
    qt_ref[...] = p[:, :DIFF_WIDTH].T.astype(BF16)
    k_ref[...] = p[:, DIFF_WIDTH:2 * DIFF_WIDTH].astype(BF16)
    vt_ref[...] = p[:, 2 * DIFF_WIDTH:ATTN_COLS].T.astype(BF16)
    oh_ref[...] = p[:, ATTN_COLS:]


def _norm_proj(x2, g, w_bf16, tb):
    t, d = x2.shape
    n = w_bf16.shape[1]
    return pl.pallas_call(
        _norm_proj_kernel,
        grid=(t // tb,),
        in_specs=[pl.BlockSpec((tb, d), lambda i: (i, 0)),
                  pl.BlockSpec((1, d), lambda i: (0, 0)),
                  pl.BlockSpec((d, n), lambda i: (0, 0))],
        out_specs=[pl.BlockSpec((DIFF_WIDTH, tb), lambda i: (0, i)),
                   pl.BlockSpec((tb, DIFF_WIDTH), lambda i: (i, 0)),
                   pl.BlockSpec((DIFF_WIDTH, tb), lambda i: (0, i)),
                   pl.BlockSpec((tb, HGRN_COLS), lambda i: (i, 0))],
        out_shape=[jax.ShapeDtypeStruct((DIFF_WIDTH, t), BF16),
                   jax.ShapeDtypeStruct((t, DIFF_WIDTH), BF16),
                   jax.ShapeDtypeStruct((DIFF_WIDTH, t), BF16),
                   jax.ShapeDtypeStruct((t, HGRN_COLS), F32)],
        compiler_params=_cparams(("parallel",)),
        name="norm_proj",
    )(x2, g, w_bf16)


def _t5_bucket_np(n):
    n = np.maximum(n, 0).astype(np.int32)
    max_exact = REL_BUCKETS // 2
    nf = np.maximum(n, max_exact).astype(np.float32)
    large = max_exact + (np.log(nf / np.float32(max_exact)) / np.float32(math.log(REL_MAX_DIST / max_exact))
                         * np.float32(REL_BUCKETS - max_exact)).astype(np.int32)
    large = np.minimum(large, REL_BUCKETS - 1)
    return np.where(n < max_exact, n, large).astype(np.int32)


def _bias_tiles_kernel(table_ref, bucket_ref, o_ref):
    h = pl.program_id(0)
    bucket = bucket_ref[...]
    acc = jnp.zeros(bucket.shape, F32)
    for b in range(REL_BUCKETS):
        acc = jnp.where(bucket == b, table_ref[b, h], acc)
    o_ref[...] = acc


def _bias_tiles(table, tq):
    c = np.arange(tq)[:, None]
    r = np.arange(tq)[None, :]
    tiles = np.stack([_t5_bucket_np(d * tq + r - c) for d in (0, 1)])
    tiles = np.concatenate([tiles, tiles], axis=2)
    return pl.pallas_call(
        _bias_tiles_kernel,
        grid=(DIFF_HEADS,),
        in_specs=[pl.BlockSpec(memory_space=pltpu.SMEM),
                  pl.BlockSpec((2, tq, 2 * tq), lambda h: (0, 0, 0))],
        out_specs=pl.BlockSpec((None, 2, tq, 2 * tq), lambda h: (h, 0, 0, 0)),
        out_shape=jax.ShapeDtypeStruct((DIFF_HEADS, 2, tq, 2 * tq), F32),
        compiler_params=_cparams(("arbitrary",)),
        name="bias_tiles",
    )(table, jnp.asarray(tiles))


def _diff_attn_kernel(lam_ref, table_ref, qt_ref, k_ref, vt_ref, bias_ref, g_ref, o_ref,
                      s_ref, mx_ref, l_ref, acc_ref, *, tq):
    h0 = pl.program_id(1) * ATTN_HEADS_PER_STEP
    i = pl.program_id(2)
    heads = range(ATTN_HEADS_PER_STEP)
    hrows = lambda hh: slice(hh * HEAD_W, (hh + 1) * HEAD_W)
    q2t = []
    for hh in heads:
        qt = qt_ref[hrows(hh), :] * jnp.asarray(DIFF_QK_DIM ** -0.5, BF16)
        row = lax.broadcasted_iota(jnp.int32, qt.shape, 0)
        zero = jnp.zeros_like(qt)
        q2t.append(jnp.concatenate([jnp.where(row < DIFF_QK_DIM, qt, zero),
                                    jnp.where(row >= DIFF_QK_DIM, qt, zero)], axis=1))

    def fold(op, x):
        return functools.reduce(op, [x[SUBLANES * k:SUBLANES * (k + 1)] for k in range(x.shape[0] // SUBLANES)])

    def scores(j, n_keys, delta, causal):
        start = pl.multiple_of(j * tq, tq)
        for hh in heads:
            bias = table_ref[REL_BUCKETS - 1, h0 + hh] if delta is None else bias_ref[hh, delta]
            s = jnp.dot(k_ref[pl.ds(start, n_keys), hrows(hh)], q2t[hh], preferred_element_type=F32) + bias
            if causal:
                c = lax.broadcasted_iota(jnp.int32, s.shape, 0)
                r = lax.broadcasted_iota(jnp.int32, s.shape, 1) % tq
                s = jnp.where(c <= r, s, NEG_BIG)
            for k in range(n_keys // tq):
                s_ref[hh, j + k] = s[k * tq:(k + 1) * tq]
            mx_ref[hh] = jnp.maximum(mx_ref[hh], fold(jnp.maximum, s))

    mx_ref[...] = jnp.full(mx_ref.shape, NEG_BIG, F32)
    l_ref[...] = jnp.zeros(l_ref.shape, F32)
    acc_ref[...] = jnp.zeros(acc_ref.shape, F32)

    def far_pair(p, carry):
        scores(2 * p, 2 * tq, None, False)
        return carry

    n_far = jnp.maximum(i - 1, 0)
    lax.fori_loop(0, n_far // 2, far_pair, 0)

    @pl.when(n_far % 2 == 1)
    def _():
        scores(n_far - 1, tq, None, False)

    @pl.when(i >= 1)
    def _():
        scores(i - 1, tq, 1, False)

    scores(i, tq, 0, True)
    m = [jnp.max(mx_ref[hh], axis=0, keepdims=True) for hh in heads]

    def weigh(j, n_keys):
        start = pl.multiple_of(j * tq, tq)
        for hh in heads:
            s = jnp.concatenate([s_ref[hh, j + k] for k in range(n_keys // tq)], axis=0)
            e = jnp.exp(s - m[hh])
            l_ref[hh] += fold(jnp.add, e)
            acc_ref[hh] += jnp.dot(vt_ref[hrows(hh), pl.ds(start, n_keys)], e.astype(BF16),
                                   preferred_element_type=F32)

    def weigh_pair(p, carry):
        weigh(2 * p, 2 * tq)
        return carry

    n_blocks = i + 1
    lax.fori_loop(0, n_blocks // 2, weigh_pair, 0)

    @pl.when(n_blocks % 2 == 1)
    def _():
        weigh(i, tq)

    lv = lam_ref[...]
    lam = (jnp.exp(jnp.sum(lv[0:1] * lv[1:2], axis=-1, keepdims=True))
           - jnp.exp(jnp.sum(lv[2:3] * lv[3:4], axis=-1, keepdims=True)) + LAM_INIT)
    for hh in heads:
        l = jnp.sum(l_ref[hh], axis=0, keepdims=True)
        acc = acc_ref[hh]
        out_t = acc[:, :tq] / l[:, :tq] - lam * (acc[:, tq:] / l[:, tq:])
        o_ref[:, hrows(hh)] = (_rms(out_t.T, g_ref[...]) * (1.0 - LAM_INIT)).astype(o_ref.dtype)


def _diff_attn(lam, table, q_t, k, v_t, bias, g, batch, seq, tq):
    kern = functools.partial(_diff_attn_kernel, tq=tq)
    nq = seq // tq
    hp = ATTN_HEADS_PER_STEP
    hw = hp * HEAD_W
    return pl.pallas_call(
        kern,
        grid=(batch, DIFF_HEADS // hp, nq),
        in_specs=[pl.BlockSpec(lam.shape, lambda b, h, i: (0, 0)),
                  pl.BlockSpec(memory_space=pltpu.SMEM),
                  pl.BlockSpec((hw, tq), lambda b, h, i: (h, b * nq + i)),
                  pl.BlockSpec((seq, hw), lambda b, h, i: (b, h)),
                  pl.BlockSpec((hw, seq), lambda b, h, i: (h, b)),
                  pl.BlockSpec((hp, 2, tq, 2 * tq), lambda b, h, i: (h, 0, 0, 0)),
                  pl.BlockSpec((1, HEAD_W), lambda b, h, i: (0, 0))],
        out_specs=pl.BlockSpec((None, tq, hw), lambda b, h, i: (b, i, h)),
        out_shape=jax.ShapeDtypeStruct((batch, seq, DIFF_WIDTH), BF16),
        scratch_shapes=[pltpu.VMEM((hp, seq // tq, tq, 2 * tq), F32),
                        pltpu.VMEM((hp, SUBLANES, 2 * tq), F32),
                        pltpu.VMEM((hp, SUBLANES, 2 * tq), F32),
                        pltpu.VMEM((hp, HEAD_W, 2 * tq), F32)],
        compiler_params=_cparams(("parallel", "parallel", "arbitrary")),
        name="diff_attn",
    )(lam, table, q_t, k, v_t, bias, g)


def _hgrn_kernel(hq_ref, hf_ref, hi_ref, hg_ref, lb_ref, g_ref, o_ref,
                 qd_ref, kd_ref, cd_ref, oi_ref, upd_ref, prev_ref, *, seq):
    c_len = HGRN_CHUNK
    n_chunks = seq // c_len
    logits = lb_ref[...]
    ex = jnp.exp(logits - jnp.max(logits, axis=0, keepdims=True))
    lb = ex[0:1] / jnp.sum(ex, axis=0, keepdims=True)
    f = lb + (1.0 - lb) * jax.nn.sigmoid(hf_ref[...])
    log_f = jnp.log(f)
    kk = 1.0 - f
    hq = hq_ref[...]
    qq = hq * jax.nn.sigmoid(hq)
    vv = hi_ref[...]

    pos = lax.broadcasted_iota(jnp.int32, (seq, HEAD_W), 0) % c_len
    b = log_f
    sh = 1
    while sh < c_len:
        b = b + jnp.where(pos >= sh, pltpu.roll(b, sh, axis=0), 0.0)
        sh *= 2
    b3 = b.reshape(n_chunks, c_len, HEAD_W)
    b_last = b3[:, c_len - 1:c_len, :]
    q_dec = qq * jnp.exp(b)
    k_inv = kk * jnp.exp(-b)
    k_dec = kk * jnp.exp(b_last - b3).reshape(seq, HEAD_W)
    qd_ref[...] = q_dec.astype(BF16)
    kd_ref[...] = k_dec.astype(BF16)
    cd_ref[...] = jnp.exp(b_last.reshape(n_chunks, HEAD_W))

    blk = HEAD_W
    nb = seq // blk
    qd_b = q_dec.astype(BF16).reshape(nb, blk, HEAD_W)
    ki_b = k_inv.astype(BF16).reshape(nb, blk, HEAD_W)
    scores = jnp.einsum('nqd,nkd->nqk', qd_b, ki_b, preferred_element_type=F32)
    r = lax.broadcasted_iota(jnp.int32, (blk, blk), 0)
    c = lax.broadcasted_iota(jnp.int32, (blk, blk), 1)
    keep = (c <= r) & ((r // c_len) == (c // c_len))
    scores = jnp.where(keep[None], scores, 0.0)
    o_intra = jnp.einsum('nqk,nkv->nqv', scores.astype(BF16), vv.astype(BF16).reshape(nb, blk, HEAD_W),
                         preferred_element_type=F32).reshape(seq, HEAD_W)

    unroll = 16

    def rows_of(ci):
        return pl.ds(pl.multiple_of(ci * c_len, c_len), c_len)

    def updates(g, carry):
        for u in range(unroll):
            ci = g * unroll + u
            vc = hi_ref[rows_of(ci), :].astype(BF16)
            upd_ref[ci] = lax.dot_general(vc, kd_ref[rows_of(ci), :], (((0,), (0,)), ((), ())),
                                          preferred_element_type=F32)
        return carry

    lax.fori_loop(0, n_chunks // unroll, updates, 0)

    def recur(ci, st):
        prev_ref[ci] = st.astype(BF16)
        return st * cd_ref[pl.ds(ci, 1), :] + upd_ref[ci]

    lax.fori_loop(0, n_chunks, recur, jnp.zeros((HEAD_W, HEAD_W), F32))

    def reads(g, carry):
        for u in range(unroll):
            ci = g * unroll + u
            oi_ref[rows_of(ci), :] = lax.dot_general(qd_ref[rows_of(ci), :], prev_ref[ci],
                                                     (((1,), (1,)), ((), ())), preferred_element_type=F32)
        return carry

    lax.fori_loop(0, n_chunks // unroll, reads, 0)

    o = o_intra + oi_ref[...]
    hg = hg_ref[...]
    o_ref[...] = (_rms(o, g_ref[...]) * (hg * jax.nn.sigmoid(hg))).astype(o_ref.dtype)


def _hgrn(hin, lb, g, batch, seq):
    kern = functools.partial(_hgrn_kernel, seq=seq)
    col = lambda k: (lambda b, h: (b, 0, k * HGRN_HEADS + h))
    blk = (None, seq, HEAD_W)
    return pl.pallas_call(
        kern,
        grid=(batch, HGRN_HEADS),
        in_specs=[pl.BlockSpec(blk, col(0)), pl.BlockSpec(blk, col(1)),
                  pl.BlockSpec(blk, col(2)), pl.BlockSpec(blk, col(3)),
                  pl.BlockSpec((lb.shape[0], HEAD_W), lambda b, h: (0, h)),
                  pl.BlockSpec((1, HEAD_W), lambda b, h: (0, 0))],
        out_specs=pl.BlockSpec(blk, lambda b, h: (b, 0, h)),
        out_shape=jax.ShapeDtypeStruct((batch, seq, HGRN_WIDTH), BF16),
        scratch_shapes=[pltpu.VMEM((seq, HEAD_W), BF16), pltpu.VMEM((seq, HEAD_W), BF16),
                        pltpu.VMEM((seq // HGRN_CHUNK, HEAD_W), F32), pltpu.VMEM((seq, HEAD_W), F32),
                        pltpu.VMEM((seq // HGRN_CHUNK, HEAD_W, HEAD_W), F32),
                        pltpu.VMEM((seq // HGRN_CHUNK, HEAD_W, HEAD_W), BF16)],
        compiler_params=_cparams(("parallel", "parallel")),
        name="hgrn",
    )(hin, hin, hin, hin, lb, g)


def _out_proj_kernel(x_ref, a_ref, o_ref, wa_ref, wo_ref, g_ref, h_ref, hnt_ref):
    h = (x_ref[...]
         + jnp.dot(a_ref[...], wa_ref[...], preferred_element_type=F32)
         + jnp.dot(o_ref[...], wo_ref[...], preferred_element_type=F32))
    h_ref[...] = h
    hnt_ref[...] = _rms(h, g_ref[...]).T.astype(BF16)


def _out_proj(x2, a2, o2, wa, wo, g, tb):
    t, d = x2.shape
    return pl.pallas_call(
        _out_proj_kernel,
        grid=(t // tb,),
        in_specs=[pl.BlockSpec((tb, d), lambda i: (i, 0)),
                  pl.BlockSpec((tb, DIFF_WIDTH), lambda i: (i, 0)),
                  pl.BlockSpec((tb, HGRN_WIDTH), lambda i: (i, 0)),
                  pl.BlockSpec((DIFF_WIDTH, d), lambda i: (0, 0)),
                  pl.BlockSpec((HGRN_WIDTH, d), lambda i: (0, 0)),
                  pl.BlockSpec((1, d), lambda i: (0, 0))],
        out_specs=[pl.BlockSpec((tb, d), lambda i: (i, 0)),
                   pl.BlockSpec((d, tb), lambda i: (0, i))],
        out_shape=[jax.ShapeDtypeStruct((t, d), F32),
                   jax.ShapeDtypeStruct((d, t), BF16)],
        compiler_params=_cparams(("parallel",)),
        name="out_proj",
    )(x2, a2, o2, wa, wo, g)


SUBLANES = 8
ATTN_HEADS_PER_STEP = 4
TAU_GUARD = 2.0 ** -21


def _oddeven_merge(lo, hi, r):
    step = r * 2
    if step < hi - lo:
        yield from _oddeven_merge(lo, hi, step)
        yield from _oddeven_merge(lo + r, hi, step)
        yield from [(i, i + r) for i in range(lo + r, hi - r, step)]
    else:
        yield (lo, lo + r)


def _oddeven_merge_sort(lo, hi):
    if hi - lo >= 1:
        mid = lo + (hi - lo) // 2
        yield from _oddeven_merge_sort(lo, mid)
        yield from _oddeven_merge_sort(mid + 1, hi)
        yield from _oddeven_merge(lo, hi, 1)


def _bitonic_merge(n):
    d = n // 2
    while d >= 1:
        yield from [(i, i + d) for i in range(n) if (i // d) % 2 == 0]
        d //= 2


SORT16 = tuple(_oddeven_merge_sort(0, PEER_TOPK - 1))
MERGE16 = tuple(_bitonic_merge(PEER_TOPK))


def _bf16_row_tile(row, n_rows):
    packed_rows = 16
    tile16 = jnp.broadcast_to(row, (packed_rows, row.shape[1])).astype(BF16)
    return jnp.broadcast_to(tile16[None], (n_rows // packed_rows, packed_rows, row.shape[1])
                            ).reshape(n_rows, row.shape[1])


def _compare_exchange(v, pairs):
    for i, j in pairs:
        v[i], v[j] = jnp.maximum(v[i], v[j]), jnp.minimum(v[i], v[j])


def _top16_sorted(x):
    v = [x[SUBLANES * k:SUBLANES * (k + 1)] for k in range(PEER_TOPK)]
    _compare_exchange(v, SORT16)
    shift = SUBLANES // 2
    while shift >= 1:
        v = [jnp.maximum(v[k], pltpu.roll(v[PEER_TOPK - 1 - k], shift, axis=0)) for k in range(PEER_TOPK)]
        _compare_exchange(v, MERGE16)
        shift //= 2
    return v


def _rows_on_sublanes(rep, start):
    sub = lax.broadcasted_iota(jnp.int32, rep[0].shape, 0)
    out = rep[start]
    for s in range(1, SUBLANES):
        out = jnp.where(sub == s, rep[start + s], out)
    return out


def _pair_candidates(a_rep, b_rep):
    k = PEER_TOPK
    sub = lax.broadcasted_iota(jnp.int32, a_rep[0].shape, 0)
    b_lo = _rows_on_sublanes(b_rep, 0)
    b_hi = _rows_on_sublanes(b_rep, SUBLANES)
    a_hi = _rows_on_sublanes(a_rep, SUBLANES)
    pieces = [a_rep[0] + b_lo, a_rep[0] + b_hi, a_rep[1] + b_lo]
    for i in range(2, SUBLANES):
        pieces.append(jnp.where(sub < k // (i + 1), a_rep[i] + b_lo, -jnp.inf))
    pieces.append(a_hi + b_rep[0])
    return jnp.concatenate(pieces, axis=0)


def _prefix_count(test, levels):
    bits = []
    for depth in range(4):
        width = PEER_TOPK >> (depth + 1)
        pivot = _pick_level(levels, bits, width)
        bits.append(test(pivot))
    count = sum(jnp.where(b, float(PEER_TOPK >> (d + 1)), 0.0) for d, b in enumerate(bits))
    return jnp.where(test(levels[PEER_TOPK - 1]), float(PEER_TOPK), count)


def _pick_level(levels, bits, width):
    def rec(base, remaining, w):
        if not remaining:
            return levels[base + width - 1]
        return jnp.where(remaining[0], rec(base + w, remaining[1:], w // 2), rec(base, remaining[1:], w // 2))
    return rec(0, bits, PEER_TOPK // 2)


def _kth_largest(x, k):
    for r in range(k):
        m = jnp.max(x, axis=0, keepdims=True)
        if r + 1 < k:
            x = jnp.where(x == m, -jnp.inf, x)
    return m


def _peer_route_kernel(hnt_ref, wqt_ref, keys_ref, rank_ref, alpha_ref, nsel_ref, beta_ref, qt_ref, s_ref):
    qt_ref[...] = jnp.dot(wqt_ref[...], hnt_ref[...], preferred_element_type=F32).astype(BF16)
    half = PEER_QUERY_DIM // 2
    lane_tiles = hnt_ref.shape[1] // LANES
    k = PEER_TOPK

    def head(h, carry):
        base = pl.multiple_of(h * PEER_QUERY_DIM, PEER_QUERY_DIM)
        for c in range(2):
            s = jnp.dot(keys_ref[h, c], qt_ref[pl.ds(base + c * half, half), :], preferred_element_type=F32)
            for lt in range(lane_tiles):
                s_ref[c, lt] = s[:, lt * LANES:(lt + 1) * LANES]

        def tile(lt, carry):
            cols = pl.ds(pl.multiple_of(lt * LANES, LANES), LANES)
            s1 = s_ref[0, lt]
            s2 = s_ref[1, lt]
            a_rep = _top16_sorted(s1)
            b_rep = _top16_sorted(s2)
            cands = _pair_candidates(a_rep, b_rep)
            thr = _kth_largest(cands, k)
            top = a_rep[0][0:1] + b_rep[0][0:1]
            z = jnp.sum(jnp.where(cands >= thr, jnp.exp(cands - top), 0.0), axis=0, keepdims=True)
            rank1 = []
            n_sel = []
            theta = [(thr - a_rep[j]) - (jnp.abs(thr) + jnp.abs(a_rep[j])) * TAU_GUARD for j in range(k)]
            for v in range(PEER_N_KEYS // SUBLANES):
                x1 = s1[SUBLANES * v:SUBLANES * (v + 1)]
                x2 = s2[SUBLANES * v:SUBLANES * (v + 1)]
                rank1.append(_prefix_count(lambda a: a > x1, a_rep))
                n_sel.append(_prefix_count(lambda th: x2 >= th, theta))
            rank_ref[h, :, cols] = jnp.concatenate(rank1, axis=0)
            alpha_ref[h, :, cols] = jnp.exp(s1 - a_rep[0][0:1]) * (0.5 / z)
            nsel_ref[h, lt] = jnp.concatenate(n_sel, axis=0).astype(BF16)
            beta_ref[h, lt] = jnp.exp(s2 - b_rep[0][0:1]).astype(BF16)
            return carry

        def tile_pair(p, carry):
            return tile(2 * p + 1, tile(2 * p, carry))

        return lax.fori_loop(0, lane_tiles // 2, tile_pair, carry)

    lax.fori_loop(0, PEER_HEADS, head, 0)


def _peer_route(hnt, wqt, keys, tb):
    d, t = hnt.shape
    nq = wqt.shape[0]
    oshape = jax.ShapeDtypeStruct((PEER_HEADS, PEER_N_KEYS, t), F32)
    ospec = pl.BlockSpec((PEER_HEADS, PEER_N_KEYS, tb), lambda i: (0, 0, i))
    tshape = jax.ShapeDtypeStruct((PEER_HEADS, t // LANES, PEER_N_KEYS, LANES), BF16)
    tspec = pl.BlockSpec((PEER_HEADS, tb // LANES, PEER_N_KEYS, LANES), lambda i: (0, i, 0, 0))
    return pl.pallas_call(
        _peer_route_kernel,
        grid=(t // tb,),
        in_specs=[pl.BlockSpec((d, tb), lambda i: (0, i)),
                  pl.BlockSpec((nq, d), lambda i: (0, 0)),
                  pl.BlockSpec(keys.shape, lambda i: (0, 0, 0, 0))],
        out_specs=[ospec, ospec, tspec, tspec],
        out_shape=[oshape, oshape, tshape, tshape],
        scratch_shapes=[pltpu.VMEM((nq, tb), BF16), pltpu.VMEM((2, tb // LANES, PEER_N_KEYS, LANES), F32)],
        compiler_params=_cparams(("parallel",)),
        name="peer_route",
    )(hnt, wqt, keys)


def _peer_dense_kernel(hnt_ref, u_ref, vt_ref, rank_ref, alpha_ref, nsel_ref, beta_ref, o_ref,
                       hid_ref, act_ref, *, te, tb, n_tiles):
    j = pl.program_id(1)
    cur = j % 2
    prev = 1 - cur
    rows_per_tile = te // PEER_N_KEYS
    lane_tiles = tb // LANES

    def hidden(hid_ref):
        hid = jnp.dot(u_ref[...], hnt_ref[...], preferred_element_type=F32)
        for lt in range(lane_tiles):
            hid_ref[lt] = hid[:, lt * LANES:(lt + 1) * LANES]

    def activations(hid_ref, act_ref):
        def lane_tile(lt, carry):
            cols = pl.ds(pl.multiple_of(lt * LANES, LANES), LANES)
            zero = jnp.zeros((PEER_N_KEYS, LANES), BF16)
            for r in range(rows_per_tile):
                w = None
                for h in range(PEER_HEADS):
                    sel = _bf16_row_tile(rank_ref[h, r:r + 1, cols], PEER_N_KEYS) < nsel_ref[h, lt]
                    term = jnp.where(sel, beta_ref[h, lt], zero) * _bf16_row_tile(alpha_ref[h, r:r + 1, cols],
                                                                                  PEER_N_KEYS)
                    w = term if w is None else w + term
                rows = slice(r * PEER_N_KEYS, (r + 1) * PEER_N_KEYS)
                hid_t = hid_ref[lt, rows, :]
                gelu2 = hid_t * (1.0 + lax.erf(hid_t * (2.0 ** -0.5)))
                act_ref[lt, rows, :] = gelu2.astype(BF16) * w
            return carry

        lax.fori_loop(0, lane_tiles, lane_tile, 0)

    def project(act_ref):
        act = jnp.concatenate([act_ref[lt] for lt in range(lane_tiles)], axis=1)
        o_ref[...] += jnp.dot(vt_ref[...], act, preferred_element_type=F32)

    def run(do_a, do_b, do_c):
        if do_a:
            hidden(hid_ref.at[cur])
        if do_b:
            activations(hid_ref.at[prev], act_ref.at[prev])
        if do_c:
            project(act_ref.at[cur])

    @pl.when(jnp.logical_and(j >= 2, j < n_tiles))
    def _():
        run(True, True, True)

    @pl.when(j == 0)
    def _():
        o_ref[...] = jnp.zeros(o_ref.shape, o_ref.dtype)
        run(True, False, False)

    @pl.when(j == 1)
    def _():
        run(True, True, False)

    @pl.when(j == n_tiles)
    def _():
        run(False, True, True)

    @pl.when(j == n_tiles + 1)
    def _():
        run(False, False, True)


def _peer_dense(hnt, u_bf16, v_bf16, rank1, alpha, n_sel, beta, tb, te):
    d, t = hnt.shape
    n_tiles = u_bf16.shape[0] // te
    assert n_tiles >= 2
    vt_tiles = v_bf16.reshape(n_tiles, te, d).transpose(0, 2, 1)
    kern = functools.partial(_peer_dense_kernel, te=te, tb=tb, n_tiles=n_tiles)
    last = n_tiles - 1
    rspec = pl.BlockSpec((PEER_HEADS, tb // LANES, PEER_N_KEYS, LANES), lambda i, j: (0, i, 0, 0))
    i1spec = pl.BlockSpec((PEER_HEADS, te // PEER_N_KEYS, tb), lambda i, j: (0, jnp.clip(j - 1, 0, last), i))
    stage_shape = (tb // LANES, te, LANES)
    return pl.pallas_call(
        kern,
        grid=(t // tb, n_tiles + 2),
        in_specs=[pl.BlockSpec((d, tb), lambda i, j: (0, i)),
                  pl.BlockSpec((te, d), lambda i, j: (jnp.minimum(j, last), 0)),
                  pl.BlockSpec((None, d, te), lambda i, j: (jnp.clip(j - 2, 0, last), 0, 0)),
                  i1spec, i1spec, rspec, rspec],
        out_specs=pl.BlockSpec((d, tb), lambda i, j: (0, i)),
        out_shape=jax.ShapeDtypeStruct((d, t), F32),
        scratch_shapes=[pltpu.VMEM((2,) + stage_shape, F32), pltpu.VMEM((2,) + stage_shape, BF16)],
        compiler_params=_cparams(("parallel", "arbitrary")),
        name="peer_dense",
    )(hnt, u_bf16, vt_tiles, rank1, alpha, n_sel, beta)


def _final_norm_kernel(h_ref, pt_ref, g_ref, o_ref):
    o_ref[...] = _rms(h_ref[...] + pt_ref[...].T, g_ref[...])


def _final_norm(h2, peer_t, g, tb):
    t, d = h2.shape
    return pl.pallas_call(
        _final_norm_kernel,
        grid=(t // tb,),
        in_specs=[pl.BlockSpec((tb, d), lambda i: (i, 0)),
                  pl.BlockSpec((d, tb), lambda i: (0, i)),
                  pl.BlockSpec((1, d), lambda i: (0, 0))],
        out_specs=pl.BlockSpec((tb, d), lambda i: (i, 0)),
        out_shape=jax.ShapeDtypeStruct((t, d), F32),
        compiler_params=_cparams(("parallel",)),
        name="final_norm",
    )(h2, peer_t, g)


def _largest_divisor(n, cap, multiple):
    best = multiple
    k = multiple
    while k <= min(n, cap):
        if n % k == 0:
            best = k
        k += multiple
    return best


def kernel(x, norm1_g, w_in, diff_lambda_q1, diff_lambda_k1, diff_lambda_q2, diff_lambda_k2, diff_subln_g,
           rel_bias_table, hgrn_lb_logits, hgrn_gnorm_g, w_out, norm2_g, peer_w_q, peer_sub_keys, peer_u,
           peer_v, final_norm_g):
    batch, seq, d = x.shape
    t = batch * seq
    assert d == D_MODEL and seq % 256 == 0 and w_in.shape[0] == 1
    tb_proj = _largest_divisor(t, 512, 256)
    tq = 256
    tb_route = _largest_divisor(t, 512, 256)
    tb_peer = _largest_divisor(t, 1024, 512)
    te_peer = 1024

    x2 = x.reshape(t, d)
    row = lambda v: v.reshape(1, -1).astype(F32)

    lam = jnp.stack([diff_lambda_q1[0], diff_lambda_k1[0], diff_lambda_q2[0], diff_lambda_k2[0]]).astype(F32)
    lb = hgrn_lb_logits.astype(F32)
    table = rel_bias_table.astype(F32)

    q_t, k, v_t, hin = _norm_proj(x2, row(norm1_g[0]), w_in[0].astype(BF16), tb_proj)
    bias = _bias_tiles(table, tq)
    a = _diff_attn(lam, table, q_t, k, v_t, bias, row(diff_subln_g[0]), batch, seq, tq)
    o = _hgrn(hin.reshape(batch, seq, HGRN_COLS), lb, row(hgrn_gnorm_g[0]), batch, seq)
    w_o = w_out[0].astype(BF16)
    h2, hnt = _out_proj(x2, a.reshape(t, DIFF_WIDTH), o.reshape(t, HGRN_WIDTH),
                        w_o[:DIFF_WIDTH], w_o[DIFF_WIDTH:], row(norm2_g[0]), tb_proj)
    rank1, alpha, n_sel, beta = _peer_route(hnt, peer_w_q[0].T.astype(BF16), peer_sub_keys[0].astype(BF16), tb_route)
    peer_t = _peer_dense(hnt, peer_u[0].astype(BF16), peer_v[0].astype(BF16), rank1, alpha, n_sel, beta,
                         tb_peer, te_peer)
    out = _final_norm(h2, peer_t, row(final_norm_g), tb_proj)
    return out.reshape(batch, seq, d)
```

```python
import functools
import math

import numpy as np
import jax
import jax.numpy as jnp
from jax import lax
from jax.experimental import pallas as pl
from jax.experimental.pallas import tpu as pltpu

F32 = jnp.float32
BF16 = jnp.bfloat16

D_MODEL = 1024
DIFF_HEADS = 4
DIFF_QK_DIM = 64
LANES = 128
HEAD_W = 128
DIFF_WIDTH = DIFF_HEADS * HEAD_W
HGRN_HEADS = 4
HGRN_WIDTH = HGRN_HEADS * HEAD_W
HGRN_CHUNK = 32
ATTN_COLS = 3 * DIFF_WIDTH
HGRN_COLS = 4 * HGRN_WIDTH
REL_BUCKETS = 32
REL_MAX_DIST = 128
PEER_HEADS = 8
PEER_N_KEYS = 128
PEER_QUERY_DIM = 256
PEER_TOPK = 16
NORM_EPS = 1e-6
LAM_INIT = 0.8 - 0.6 * math.exp(-0.3 * 0)
NEG_BIG = -1e30

VMEM_LIMIT_BYTES = 58 * 1024 * 1024


def _cparams(sem):
    return pltpu.CompilerParams(dimension_semantics=sem, vmem_limit_bytes=VMEM_LIMIT_BYTES)


def _rms(x, g):
    return x * lax.rsqrt(jnp.mean(x * x, axis=-1, keepdims=True) + NORM_EPS) * g


def _norm_proj_kernel(x_ref, g_ref, w_ref, qt_ref, k_ref, vt_ref, oh_ref):
    y = _rms(x_ref[...], g_ref[...])
    p = jnp.dot(y.astype(BF16), w_ref[...], preferred_element_type=F32)
    qt_ref[...] = p[:, :DIFF_WIDTH].T.astype(BF16)
    k_ref[...] = p[:, DIFF_WIDTH:2 * DIFF_WIDTH].astype(BF16)
    vt_ref[...] = p[:, 2 * DIFF_WIDTH:ATTN_COLS].T.astype(BF16)
    oh_ref[...] = p[:, ATTN_COLS:]


def _norm_proj(x2, g, w_bf16, tb):
    t, d = x2.shape
    n = w_bf16.shape[1]
    return pl.pallas_call(
        _norm_proj_kernel,
        grid=(t // tb,),
        in_specs=[pl.BlockSpec((tb, d), lambda i: (i, 0)),
                  pl.BlockSpec((1, d), lambda i: (0, 0)),
                  pl.BlockSpec((d, n), lambda i: (0, 0))],
        out_specs=[pl.BlockSpec((DIFF_WIDTH, tb), lambda i: (0, i)),
                   pl.BlockSpec((tb, DIFF_WIDTH), lambda i: (i, 0)),
                   pl.BlockSpec((DIFF_WIDTH, tb), lambda i: (0, i)),
                   pl.BlockSpec((tb, HGRN_COLS), lambda i: (i, 0))],
        out_shape=[jax.ShapeDtypeStruct((DIFF_WIDTH, t), BF16),
                   jax.ShapeDtypeStruct((t, DIFF_WIDTH), BF16),
                   jax.ShapeDtypeStruct((DIFF_WIDTH, t), BF16),
                   jax.ShapeDtypeStruct((t, HGRN_COLS), F32)],
        compiler_params=_cparams(("parallel",)),
        name="norm_proj",
    )(x2, g, w_bf16)


def _t5_bucket_np(n):
    n = np.maximum(n, 0).astype(np.int32)
    max_exact = REL_BUCKETS // 2
    nf = np.maximum(n, max_exact).astype(np.float32)
    large = max_exact + (np.log(nf / np.float32(max_exact)) / np.float32(math.log(REL_MAX_DIST / max_exact))
                         * np.float32(REL_BUCKETS - max_exact)).astype(np.int32)
    large = np.minimum(large, REL_BUCKETS - 1)
    return np.where(n < max_exact, n, large).astype(np.int32)


def _bias_tiles_kernel(table_ref, bucket_ref, o_ref):
    h = pl.program_id(0)
    bucket = bucket_ref[...]
    acc = jnp.zeros(bucket.shape, F32)
    for b in range(REL_BUCKETS):
        acc = jnp.where(bucket == b, table_ref[b, h], acc)
    o_ref[...] = acc


def _bias_tiles(table, tq):
    c = np.arange(tq)[:, None]
    r = np.arange(tq)[None, :]
    tiles = np.stack([_t5_bucket_np(d * tq + r - c) for d in (0, 1)])
    tiles = np.concatenate([tiles, tiles], axis=2)
    return pl.pallas_call(
        _bias_tiles_kernel,
        grid=(DIFF_HEADS,),
        in_specs=[pl.BlockSpec(memory_space=pltpu.SMEM),
                  pl.BlockSpec((2, tq, 2 * tq), lambda h: (0, 0, 0))],
        out_specs=pl.BlockSpec((None, 2, tq, 2 * tq), lambda h: (h, 0, 0, 0)),
        out_shape=jax.ShapeDtypeStruct((DIFF_HEADS, 2, tq, 2 * tq), F32),
        compiler_params=_cparams(("arbitrary",)),
        name="bias_tiles",
    )(table, jnp.asarray(tiles))


def _diff_attn_kernel(lam_ref, table_ref, qt_ref, k_ref, vt_ref, bias_ref, g_ref, o_ref,
                      s_ref, mx_ref, l_ref, acc_ref, *, tq):
    h0 = pl.program_id(1) * ATTN_HEADS_PER_STEP
    i = pl.program_id(2)
    heads = range(ATTN_HEADS_PER_STEP)
    hrows = lambda hh: slice(hh * HEAD_W, (hh + 1) * HEAD_W)
    q2t = []
    for hh in heads:
        qt = qt_ref[hrows(hh), :] * jnp.asarray(DIFF_QK_DIM ** -0.5, BF16)
        row = lax.broadcasted_iota(jnp.int32, qt.shape, 0)
        zero = jnp.zeros_like(qt)
        q2t.append(jnp.concatenate([jnp.where(row < DIFF_QK_DIM, qt, zero),
                                    jnp.where(row >= DIFF_QK_DIM, qt, zero)], axis=1))

    def fold(op, x):
        return functools.reduce(op, [x[SUBLANES * k:SUBLANES * (k + 1)] for k in range(x.shape[0] // SUBLANES)])

    def scores(j, n_keys, delta, causal):
        start = pl.multiple_of(j * tq, tq)
        for hh in heads:
            bias = table_ref[REL_BUCKETS - 1, h0 + hh] if delta is None else bias_ref[hh, delta]
            s = jnp.dot(k_ref[pl.ds(start, n_keys), hrows(hh)], q2t[hh], preferred_element_type=F32) + bias
            if causal:
                c = lax.broadcasted_iota(jnp.int32, s.shape, 0)
                r = lax.broadcasted_iota(jnp.int32, s.shape, 1) % tq
                s = jnp.where(c <= r, s, NEG_BIG)
            for k in range(n_keys // tq):
                s_ref[hh, j + k] = s[k * tq:(k + 1) * tq]
            mx_ref[hh] = jnp.maximum(mx_ref[hh], fold(jnp.maximum, s))

    mx_ref[...] = jnp.full(mx_ref.shape, NEG_BIG, F32)
    l_ref[...] = jnp.zeros(l_ref.shape, F32)
    acc_ref[...] = jnp.zeros(acc_ref.shape, F32)

    def far_pair(p, carry):
        scores(2 * p, 2 * tq, None, False)
        return carry

    n_far = jnp.maximum(i - 1, 0)
    lax.fori_loop(0, n_far // 2, far_pair, 0)

    @pl.when(n_far % 2 == 1)
    def _():
        scores(n_far - 1, tq, None, False)

    @pl.when(i >= 1)
    def _():
        scores(i - 1, tq, 1, False)

    scores(i, tq, 0, True)
    m = [jnp.max(mx_ref[hh], axis=0, keepdims=True) for hh in heads]

    def weigh(j, n_keys):
        start = pl.multiple_of(j * tq, tq)
        for hh in heads:
            s = jnp.concatenate([s_ref[hh, j + k] for k in range(n_keys // tq)], axis=0)
            e = jnp.exp(s - m[hh])
            l_ref[hh] += fold(jnp.add, e)
            acc_ref[hh] += jnp.dot(vt_ref[hrows(hh), pl.ds(start, n_keys)], e.astype(BF16),
                                   preferred_element_type=F32)

    def weigh_pair(p, carry):
        weigh(2 * p, 2 * tq)
        return carry

    n_blocks = i + 1
    lax.fori_loop(0, n_blocks // 2, weigh_pair, 0)

    @pl.when(n_blocks % 2 == 1)
    def _():
        weigh(i, tq)

    lv = lam_ref[...]
    lam = (jnp.exp(jnp.sum(lv[0:1] * lv[1:2], axis=-1, keepdims=True))
           - jnp.exp(jnp.sum(lv[2:3] * lv[3:4], axis=-1, keepdims=True)) + LAM_INIT)
    for hh in heads:
        l = jnp.sum(l_ref[hh], axis=0, keepdims=True)
        acc = acc_ref[hh]
        out_t = acc[:, :tq] / l[:, :tq] - lam * (acc[:, tq:] / l[:, tq:])
        o_ref[:, hrows(hh)] = (_rms(out_t.T, g_ref[...]) * (1.0 - LAM_INIT)).astype(o_ref.dtype)


def _diff_attn(lam, table, q_t, k, v_t, bias, g, batch, seq, tq):
    kern = functools.partial(_diff_attn_kernel, tq=tq)
    nq = seq // tq
    hp = ATTN_HEADS_PER_STEP
    hw = hp * HEAD_W
    return pl.pallas_call(
        kern,
        grid=(batch, DIFF_HEADS // hp, nq),
        in_specs=[pl.BlockSpec(lam.shape, lambda b, h, i: (0, 0)),
                  pl.BlockSpec(memory_space=pltpu.SMEM),
                  pl.BlockSpec((hw, tq), lambda b, h, i: (h, b * nq + i)),
                  pl.BlockSpec((seq, hw), lambda b, h, i: (b, h)),
                  pl.BlockSpec((hw, seq), lambda b, h, i: (h, b)),
                  pl.BlockSpec((hp, 2, tq, 2 * tq), lambda b, h, i: (h, 0, 0, 0)),
                  pl.BlockSpec((1, HEAD_W), lambda b, h, i: (0, 0))],
        out_specs=pl.BlockSpec((None, tq, hw), lambda b, h, i: (b, i, h)),
        out_shape=jax.ShapeDtypeStruct((batch, seq, DIFF_WIDTH), BF16),
        scratch_shapes=[pltpu.VMEM((hp, seq // tq, tq, 2 * tq), F32),
                        pltpu.VMEM((hp, SUBLANES, 2 * tq), F32),
                        pltpu.VMEM((hp, SUBLANES, 2 * tq), F32),
                        pltpu.VMEM((hp, HEAD_W, 2 * tq), F32)],
        compiler_params=_cparams(("parallel", "parallel", "arbitrary")),
        name="diff_attn",
    )(lam, table, q_t, k, v_t, bias, g)


def _hgrn_kernel(hq_ref, hf_ref, hi_ref, hg_ref, lb_ref, g_ref, o_ref,
                 qd_ref, kd_ref, cd_ref, oi_ref, upd_ref, prev_ref, *, seq):
    c_len = HGRN_CHUNK
    n_chunks = seq // c_len
    logits = lb_ref[...]
    ex = jnp.exp(logits - jnp.max(logits, axis=0, keepdims=True))
    lb = ex[0:1] / jnp.sum(ex, axis=0, keepdims=True)
    f = lb + (1.0 - lb) * jax.nn.sigmoid(hf_ref[...])
    log_f = jnp.log(f)
    kk = 1.0 - f
    hq = hq_ref[...]
    qq = hq * jax.nn.sigmoid(hq)
    vv = hi_ref[...]

    pos = lax.broadcasted_iota(jnp.int32, (seq, HEAD_W), 0) % c_len
    b = log_f
    sh = 1
    while sh < c_len:
        b = b + jnp.where(pos >= sh, pltpu.roll(b, sh, axis=0), 0.0)
        sh *= 2
    b3 = b.reshape(n_chunks, c_len, HEAD_W)
    b_last = b3[:, c_len - 1:c_len, :]
    q_dec = qq * jnp.exp(b)
    k_inv = kk * jnp.exp(-b)
    k_dec = kk * jnp.exp(b_last - b3).reshape(seq, HEAD_W)
    qd_ref[...] = q_dec.astype(BF16)
    kd_ref[...] = k_dec.astype(BF16)
    cd_ref[...] = jnp.exp(b_last.reshape(n_chunks, HEAD_W))

    blk = HEAD_W
    nb = seq // blk
    qd_b = q_dec.astype(BF16).reshape(nb, blk, HEAD_W)
    ki_b = k_inv.astype(BF16).reshape(nb, blk, HEAD_W)
    scores = jnp.einsum('nqd,nkd->nqk', qd_b, ki_b, preferred_element_type=F32)
    r = lax.broadcasted_iota(jnp.int32, (blk, blk), 0)
    c = lax.broadcasted_iota(jnp.int32, (blk, blk), 1)
    keep = (c <= r) & ((r // c_len) == (c // c_len))
    scores = jnp.where(keep[None], scores, 0.0)
    o_intra = jnp.einsum('nqk,nkv->nqv', scores.astype(BF16), vv.astype(BF16).reshape(nb, blk, HEAD_W),
                         preferred_element_type=F32).reshape(seq, HEAD_W)

    unroll = 16

    def rows_of(ci):
        return pl.ds(pl.multiple_of(ci * c_len, c_len), c_len)

    def updates(g, carry):
        for u in range(unroll):
            ci = g * unroll + u
            vc = hi_ref[rows_of(ci), :].astype(BF16)
            upd_ref[ci] = lax.dot_general(vc, kd_ref[rows_of(ci), :], (((0,), (0,)), ((), ())),
                                          preferred_element_type=F32)
        return carry

    lax.fori_loop(0, n_chunks // unroll, updates, 0)

    def recur(ci, st):
        prev_ref[ci] = st.astype(BF16)
        return st * cd_ref[pl.ds(ci, 1), :] + upd_ref[ci]

    lax.fori_loop(0, n_chunks, recur, jnp.zeros((HEAD_W, HEAD_W), F32))

    def reads(g, carry):
        for u in range(unroll):
            ci = g * unroll + u
            oi_ref[rows_of(ci), :] = lax.dot_general(qd_ref[rows_of(ci), :], prev_ref[ci],
                                                     (((1,), (1,)), ((), ())), preferred_element_type=F32)
        return carry

    lax.fori_loop(0, n_chunks // unroll, reads, 0)

    o = o_intra + oi_ref[...]
    hg = hg_ref[...]
    o_ref[...] = (_rms(o, g_ref[...]) * (hg * jax.nn.sigmoid(hg))).astype(o_ref.dtype)


def _hgrn(hin, lb, g, batch, seq):
    kern = functools.partial(_hgrn_kernel, seq=seq)
    col = lambda k: (lambda b, h: (b, 0, k * HGRN_HEADS + h))
    blk = (None, seq, HEAD_W)
    return pl.pallas_call(
        kern,
        grid=(batch, HGRN_HEADS),
        in_specs=[pl.BlockSpec(blk, col(0)), pl.BlockSpec(blk, col(1)),
                  pl.BlockSpec(blk, col(2)), pl.BlockSpec(blk, col(3)),
                  pl.BlockSpec((lb.shape[0], HEAD_W), lambda b, h: (0, h)),
                  pl.BlockSpec((1, HEAD_W), lambda b, h: (0, 0))],
        out_specs=pl.BlockSpec(blk, lambda b, h: (b, 0, h)),
        out_shape=jax.ShapeDtypeStruct((batch, seq, HGRN_WIDTH), BF16),
        scratch_shapes=[pltpu.VMEM((seq, HEAD_W), BF16), pltpu.VMEM((seq, HEAD_W), BF16),
                        pltpu.VMEM((seq // HGRN_CHUNK, HEAD_W), F32), pltpu.VMEM((seq, HEAD_W), F32),
                        pltpu.VMEM((seq // HGRN_CHUNK, HEAD_W, HEAD_W), F32),
                        pltpu.VMEM((seq // HGRN_CHUNK, HEAD_W, HEAD_W), BF16)],
        compiler_params=_cparams(("parallel", "parallel")),
        name="hgrn",
    )(hin, hin, hin, hin, lb, g)


def _out_proj_kernel(x_ref, a_ref, o_ref, wa_ref, wo_ref, g_ref, h_ref, hnt_ref):
    h = (x_ref[...]
         + jnp.dot(a_ref[...], wa_ref[...], preferred_element_type=F32)
         + jnp.dot(o_ref[...], wo_ref[...], preferred_element_type=F32))
    h_ref[...] = h
    hnt_ref[...] = _rms(h, g_ref[...]).T.astype(BF16)


def _out_proj(x2, a2, o2, wa, wo, g, tb):
    t, d = x2.shape
    return pl.pallas_call(
        _out_proj_kernel,
        grid=(t // tb,),
        in_specs=[pl.BlockSpec((tb, d), lambda i: (i, 0)),
                  pl.BlockSpec((tb, DIFF_WIDTH), lambda i: (i, 0)),
                  pl.BlockSpec((tb, HGRN_WIDTH), lambda i: (i, 0)),
                  pl.BlockSpec((DIFF_WIDTH, d), lambda i: (0, 0)),
                  pl.BlockSpec((HGRN_WIDTH, d), lambda i: (0, 0)),
                  pl.BlockSpec((1, d), lambda i: (0, 0))],
        out_specs=[pl.BlockSpec((tb, d), lambda i: (i, 0)),
                   pl.BlockSpec((d, tb), lambda i: (0, i))],
        out_shape=[jax.ShapeDtypeStruct((t, d), F32),
                   jax.ShapeDtypeStruct((d, t), BF16)],
        compiler_params=_cparams(("parallel",)),
        name="out_proj",
    )(x2, a2, o2, wa, wo, g)


SUBLANES = 8
ATTN_HEADS_PER_STEP = 4
TAU_GUARD = 2.0 ** -21


def _oddeven_merge(lo, hi, r):
    step = r * 2
    if step < hi - lo:
        yield from _oddeven_merge(lo, hi, step)
        yield from _oddeven_merge(lo + r, hi, step)
        yield from [(i, i + r) for i in range(lo + r, hi - r, step)]
    else:
        yield (lo, lo + r)


def _oddeven_merge_sort(lo, hi):
    if hi - lo >= 1:
        mid = lo + (hi - lo) // 2
        yield from _oddeven_merge_sort(lo, mid)
        yield from _oddeven_merge_sort(mid + 1, hi)
        yield from _oddeven_merge(lo, hi, 1)


def _bitonic_merge(n):
    d = n // 2
    while d >= 1:
        yield from [(i, i + d) for i in range(n) if (i // d) % 2 == 0]
        d //= 2


SORT16 = tuple(_oddeven_merge_sort(0, PEER_TOPK - 1))
MERGE16 = tuple(_bitonic_merge(PEER_TOPK))


def _bf16_row_tile(row, n_rows):
    packed_rows = 16
    tile16 = jnp.broadcast_to(row, (packed_rows, row.shape[1])).astype(BF16)
    return jnp.broadcast_to(tile16[None], (n_rows // packed_rows, packed_rows, row.shape[1])
                            ).reshape(n_rows, row.shape[1])


def _compare_exchange(v, pairs):
    for i, j in pairs:
        v[i], v[j] = jnp.maximum(v[i], v[j]), jnp.minimum(v[i], v[j])


def _top16_sorted(x):
    v = [x[SUBLANES * k:SUBLANES * (k + 1)] for k in range(PEER_TOPK)]
    _compare_exchange(v, SORT16)
    shift = SUBLANES // 2
    while shift >= 1:
        v = [jnp.maximum(v[k], pltpu.roll(v[PEER_TOPK - 1 - k], shift, axis=0)) for k in range(PEER_TOPK)]
        _compare_exchange(v, MERGE16)
        shift //= 2
    return v


def _rows_on_sublanes(rep, start):
    sub = lax.broadcasted_iota(jnp.int32, rep[0].shape, 0)
    out = rep[start]
    for s in range(1, SUBLANES):
        out = jnp.where(sub == s, rep[start + s], out)
    return out


def _pair_candidates(a_rep, b_rep):
    k = PEER_TOPK
    sub = lax.broadcasted_iota(jnp.int32, a_rep[0].shape, 0)
    b_lo = _rows_on_sublanes(b_rep, 0)
    b_hi = _rows_on_sublanes(b_rep, SUBLANES)
    a_hi = _rows_on_sublanes(a_rep, SUBLANES)
    pieces = [a_rep[0] + b_lo, a_rep[0] + b_hi, a_rep[1] + b_lo]
    for i in range(2, SUBLANES):
        pieces.append(jnp.where(sub < k // (i + 1), a_rep[i] + b_lo, -jnp.inf))
    pieces.append(a_hi + b_rep[0])
    return jnp.concatenate(pieces, axis=0)


def _prefix_count(test, levels):
    bits = []
    for depth in range(4):
        width = PEER_TOPK >> (depth + 1)
        pivot = _pick_level(levels, bits, width)
        bits.append(test(pivot))
    count = sum(jnp.where(b, float(PEER_TOPK >> (d + 1)), 0.0) for d, b in enumerate(bits))
    return jnp.where(test(levels[PEER_TOPK - 1]), float(PEER_TOPK), count)


def _pick_level(levels, bits, width):
    def rec(base, remaining, w):
        if not remaining:
            return levels[base + width - 1]
        return jnp.where(remaining[0], rec(base + w, remaining[1:], w // 2), rec(base, remaining[1:], w // 2))
    return rec(0, bits, PEER_TOPK // 2)


def _kth_largest(x, k):
    for r in range(k):
        m = jnp.max(x, axis=0, keepdims=True)
        if r + 1 < k:
            x = jnp.where(x == m, -jnp.inf, x)
    return m


def _peer_route_kernel(hnt_ref, wqt_ref, keys_ref, rank_ref, alpha_ref, nsel_ref, beta_ref, qt_ref, s_ref):
    qt_ref[...] = jnp.dot(wqt_ref[...], hnt_ref[...], preferred_element_type=F32).astype(BF16)
    half = PEER_QUERY_DIM // 2
    lane_tiles = hnt_ref.shape[1] // LANES
    k = PEER_TOPK

    def head(h, carry):
        base = pl.multiple_of(h * PEER_QUERY_DIM, PEER_QUERY_DIM)
        for c in range(2):
            s = jnp.dot(keys_ref[h, c], qt_ref[pl.ds(base + c * half, half), :], preferred_element_type=F32)
            for lt in range(lane_tiles):
                s_ref[c, lt] = s[:, lt * LANES:(lt + 1) * LANES]

        def tile(lt, carry):
            cols = pl.ds(pl.multiple_of(lt * LANES, LANES), LANES)
            s1 = s_ref[0, lt]
            s2 = s_ref[1, lt]
            a_rep = _top16_sorted(s1)
            b_rep = _top16_sorted(s2)
            cands = _pair_candidates(a_rep, b_rep)
            thr = _kth_largest(cands, k)
            top = a_rep[0][0:1] + b_rep[0][0:1]
            z = jnp.sum(jnp.where(cands >= thr, jnp.exp(cands - top), 0.0), axis=0, keepdims=True)
            rank1 = []
            n_sel = []
            theta = [(thr - a_rep[j]) - (jnp.abs(thr) + jnp.abs(a_rep[j])) * TAU_GUARD for j in range(k)]
            for v in range(PEER_N_KEYS // SUBLANES):
                x1 = s1[SUBLANES * v:SUBLANES * (v + 1)]
                x2 = s2[SUBLANES * v:SUBLANES * (v + 1)]
                rank1.append(_prefix_count(lambda a: a > x1, a_rep))
                n_sel.append(_prefix_count(lambda th: x2 >= th, theta))
            rank_ref[h, :, cols] = jnp.concatenate(rank1, axis=0)
            alpha_ref[h, :, cols] = jnp.exp(s1 - a_rep[0][0:1]) * (0.5 / z)
            nsel_ref[h, lt] = jnp.concatenate(n_sel, axis=0).astype(BF16)
            beta_ref[h, lt] = jnp.exp(s2 - b_rep[0][0:1]).astype(BF16)
            return carry

        def tile_pair(p, carry):
            return tile(2 * p + 1, tile(2 * p, carry))

        return lax.fori_loop(0, lane_tiles // 2, tile_pair, carry)

    lax.fori_loop(0, PEER_HEADS, head, 0)


def _peer_route(hnt, wqt, keys, tb):
    d, t = hnt.shape
    nq = wqt.shape[0]
    oshape = jax.ShapeDtypeStruct((PEER_HEADS, PEER_N_KEYS, t), F32)
    ospec = pl.BlockSpec((PEER_HEADS, PEER_N_KEYS, tb), lambda i: (0, 0, i))
    tshape = jax.ShapeDtypeStruct((PEER_HEADS, t // LANES, PEER_N_KEYS, LANES), BF16)
    tspec = pl.BlockSpec((PEER_HEADS, tb // LANES, PEER_N_KEYS, LANES), lambda i: (0, i, 0, 0))
    return pl.pallas_call(
        _peer_route_kernel,
        grid=(t // tb,),
        in_specs=[pl.BlockSpec((d, tb), lambda i: (0, i)),
                  pl.BlockSpec((nq, d), lambda i: (0, 0)),
                  pl.BlockSpec(keys.shape, lambda i: (0, 0, 0, 0))],
        out_specs=[ospec, ospec, tspec, tspec],
        out_shape=[oshape, oshape, tshape, tshape],
        scratch_shapes=[pltpu.VMEM((nq, tb), BF16), pltpu.VMEM((2, tb // LANES, PEER_N_KEYS, LANES), F32)],
        compiler_params=_cparams(("parallel",)),
        name="peer_route",
    )(hnt, wqt, keys)


def _peer_dense_kernel(hnt_ref, u_ref, vt_ref, rank_ref, alpha_ref, nsel_ref, beta_ref, h_ref, g_ref, o_ref,
                       hid_ref, act_ref, acc_ref, *, te, tb, n_tiles):
    j = pl.program_id(1)
    cur = j % 2
    prev = 1 - cur
    rows_per_tile = te // PEER_N_KEYS
    lane_tiles = tb // LANES

    def hidden(hid_ref):
        hid = jnp.dot(u_ref[...], hnt_ref[...], preferred_element_type=F32)
        for lt in range(lane_tiles):
            hid_ref[lt] = hid[:, lt * LANES:(lt + 1) * LANES]

    def activations(hid_ref, act_ref):
        def lane_tile(lt, carry):
            cols = pl.ds(pl.multiple_of(lt * LANES, LANES), LANES)
            zero = jnp.zeros((PEER_N_KEYS, LANES), BF16)
            for r in range(rows_per_tile):
                w = None
                for h in range(PEER_HEADS):
                    sel = _bf16_row_tile(rank_ref[h, r:r + 1, cols], PEER_N_KEYS) < nsel_ref[h, lt]
                    term = jnp.where(sel, beta_ref[h, lt], zero) * _bf16_row_tile(alpha_ref[h, r:r + 1, cols],
                                                                                  PEER_N_KEYS)
                    w = term if w is None else w + term
                rows = slice(r * PEER_N_KEYS, (r + 1) * PEER_N_KEYS)
                hid_t = hid_ref[lt, rows, :]
                gelu2 = hid_t * (1.0 + lax.erf(hid_t * (2.0 ** -0.5)))
                act_ref[lt, rows, :] = gelu2.astype(BF16) * w
            return carry

        lax.fori_loop(0, lane_tiles, lane_tile, 0)

    def project(act_ref):
        act = jnp.concatenate([act_ref[lt] for lt in range(lane_tiles)], axis=1)
        acc_ref[...] += jnp.dot(vt_ref[...], act, preferred_element_type=F32)

    def finish():
        chunk = 2 * LANES
        for c in range(tb // chunk):
            rows = slice(c * chunk, (c + 1) * chunk)
            o_ref[rows, :] = _rms(h_ref[rows, :] + acc_ref[:, rows].T, g_ref[...])

    def run(do_a, do_b, do_c):
        if do_a:
            hidden(hid_ref.at[cur])
        if do_b:
            activations(hid_ref.at[prev], act_ref.at[prev])
        if do_c:
            project(act_ref.at[cur])

    @pl.when(jnp.logical_and(j >= 2, j < n_tiles))
    def _():
        run(True, True, True)

    @pl.when(j == 0)
    def _():
        acc_ref[...] = jnp.zeros(acc_ref.shape, acc_ref.dtype)
        run(True, False, False)

    @pl.when(j == 1)
    def _():
        run(True, True, False)

    @pl.when(j == n_tiles)
    def _():
        run(False, True, True)

    @pl.when(j == n_tiles + 1)
    def _():
        run(False, False, True)
        finish()


def _peer_dense(hnt, u_bf16, v_bf16, rank1, alpha, n_sel, beta, h2, final_g, tb, te):
    d, t = hnt.shape
    n_tiles = u_bf16.shape[0] // te
    assert n_tiles >= 2
    vt_tiles = v_bf16.reshape(n_tiles, te, d).transpose(0, 2, 1)
    kern = functools.partial(_peer_dense_kernel, te=te, tb=tb, n_tiles=n_tiles)
    last = n_tiles - 1
    rspec = pl.BlockSpec((PEER_HEADS, tb // LANES, PEER_N_KEYS, LANES), lambda i, j: (0, i, 0, 0))
    i1spec = pl.BlockSpec((PEER_HEADS, te // PEER_N_KEYS, tb), lambda i, j: (0, jnp.clip(j - 1, 0, last), i))
    stage_shape = (tb // LANES, te, LANES)
    return pl.pallas_call(
        kern,
        grid=(t // tb, n_tiles + 2),
        in_specs=[pl.BlockSpec((d, tb), lambda i, j: (0, i)),
                  pl.BlockSpec((te, d), lambda i, j: (jnp.minimum(j, last), 0)),
                  pl.BlockSpec((None, d, te), lambda i, j: (jnp.clip(j - 2, 0, last), 0, 0)),
                  i1spec, i1spec, rspec, rspec,
                  pl.BlockSpec((tb, d), lambda i, j: (i, 0)),
                  pl.BlockSpec((1, d), lambda i, j: (0, 0))],
        out_specs=pl.BlockSpec((tb, d), lambda i, j: (i, 0)),
        out_shape=jax.ShapeDtypeStruct((t, d), F32),
        scratch_shapes=[pltpu.VMEM((2,) + stage_shape, F32), pltpu.VMEM((2,) + stage_shape, BF16),
                        pltpu.VMEM((d, tb), F32)],
        compiler_params=_cparams(("parallel", "arbitrary")),
        name="peer_dense",
    )(hnt, u_bf16, vt_tiles, rank1, alpha, n_sel, beta, h2, final_g)


def _largest_divisor(n, cap, multiple):
    best = multiple
    k = multiple
    while k <= min(n, cap):
        if n % k == 0:
            best = k
        k += multiple
    return best


def kernel(x, norm1_g, w_in, diff_lambda_q1, diff_lambda_k1, diff_lambda_q2, diff_lambda_k2, diff_subln_g,
           rel_bias_table, hgrn_lb_logits, hgrn_gnorm_g, w_out, norm2_g, peer_w_q, peer_sub_keys, peer_u,
           peer_v, final_norm_g):
    batch, seq, d = x.shape
    t = batch * seq
    assert d == D_MODEL and seq % 256 == 0 and w_in.shape[0] == 1
    tb_proj = _largest_divisor(t, 512, 256)
    tq = 256
    tb_route = _largest_divisor(t, 512, 256)
    tb_peer = _largest_divisor(t, 1024, 512)
    te_peer = 1024

    x2 = x.reshape(t, d)
    row = lambda v: v.reshape(1, -1).astype(F32)

    lam = jnp.stack([diff_lambda_q1[0], diff_lambda_k1[0], diff_lambda_q2[0], diff_lambda_k2[0]]).astype(F32)
    lb = hgrn_lb_logits.astype(F32)
    table = rel_bias_table.astype(F32)

    q_t, k, v_t, hin = _norm_proj(x2, row(norm1_g[0]), w_in[0].astype(BF16), tb_proj)
    bias = _bias_tiles(table, tq)
    a = _diff_attn(lam, table, q_t, k, v_t, bias, row(diff_subln_g[0]), batch, seq, tq)
    o = _hgrn(hin.reshape(batch, seq, HGRN_COLS), lb, row(hgrn_gnorm_g[0]), batch, seq)
    w_o = w_out[0].astype(BF16)
    h2, hnt = _out_proj(x2, a.reshape(t, DIFF_WIDTH), o.reshape(t, HGRN_WIDTH),
                        w_o[:DIFF_WIDTH], w_o[DIFF_WIDTH:], row(norm2_g[0]), tb_proj)
    rank1, alpha, n_sel, beta = _peer_route(hnt, peer_w_q[0].T.astype(BF16), peer_sub_keys[0].astype(BF16), tb_route)
    out = _peer_dense(hnt, peer_u[0].astype(BF16), peer_v[0].astype(BF16), rank1, alpha, n_sel, beta,
                      h2, row(final_norm_g), tb_peer, te_peer)
    return out.reshape(batch, seq, d)
```

```python
import functools
import math

import numpy as np
import jax
import jax.numpy as jnp
from jax import lax
from jax.experimental import pallas as pl
from jax.experimental.pallas import tpu as pltpu

F32 = jnp.float32
BF16 = jnp.bfloat16

D_MODEL = 1024
DIFF_HEADS = 4
DIFF_QK_DIM = 64
LANES = 128
HEAD_W = 128
DIFF_WIDTH = DIFF_HEADS * HEAD_W
HGRN_HEADS = 4
HGRN_WIDTH = HGRN_HEADS * HEAD_W
HGRN_CHUNK = 32
ATTN_COLS = 3 * DIFF_WIDTH
HGRN_COLS = 4 * HGRN_WIDTH
REL_BUCKETS = 32
REL_MAX_DIST = 128
PEER_HEADS = 8
PEER_N_KEYS = 128
PEER_QUERY_DIM = 256
PEER_TOPK = 16
NORM_EPS = 1e-6
LAM_INIT = 0.8 - 0.6 * math.exp(-0.3 * 0)
NEG_BIG = -1e30

VMEM_LIMIT_BYTES = 58 * 1024 * 1024


def _cparams(sem):
    return pltpu.CompilerParams(dimension_semantics=sem, vmem_limit_bytes=VMEM_LIMIT_BYTES)


def _rms(x, g):
    return x * lax.rsqrt(jnp.mean(x * x, axis=-1, keepdims=True) + NORM_EPS) * g


def _norm_proj_kernel(x_ref, g_ref, w_ref, qt_ref, k_ref, vt_ref, oh_ref):
    y = _rms(x_ref[...], g_ref[...])
    p = jnp.dot(y.astype(BF16), w_ref[...], preferred_element_type=F32)
    qt_ref[...] = p[:, :DIFF_WIDTH].T.astype(BF16)
    k_ref[...] = p[:, DIFF_WIDTH:2 * DIFF_WIDTH].astype(BF16)
    vt_ref[...] = p[:, 2 * DIFF_WIDTH:ATTN_COLS].T.astype(BF16)
    oh_ref[...] = p[:, ATTN_COLS:]


def _norm_proj(x2, g, w_bf16, tb):
    t, d = x2.shape
    n = w_bf16.shape[1]
    return pl.pallas_call(
        _norm_proj_kernel,
        grid=(t // tb,),
        in_specs=[pl.BlockSpec((tb, d), lambda i: (i, 0)),
                  pl.BlockSpec((1, d), lambda i: (0, 0)),
                  pl.BlockSpec((d, n), lambda i: (0, 0))],
        out_specs=[pl.BlockSpec((DIFF_WIDTH, tb), lambda i: (0, i)),
                   pl.BlockSpec((tb, DIFF_WIDTH), lambda i: (i, 0)),
                   pl.BlockSpec((DIFF_WIDTH, tb), lambda i: (0, i)),
                   pl.BlockSpec((tb, HGRN_COLS), lambda i: (i, 0))],
        out_shape=[jax.ShapeDtypeStruct((DIFF_WIDTH, t), BF16),
                   jax.ShapeDtypeStruct((t, DIFF_WIDTH), BF16),
                   jax.ShapeDtypeStruct((DIFF_WIDTH, t), BF16),
                   jax.ShapeDtypeStruct((t, HGRN_COLS), F32)],
        compiler_params=_cparams(("parallel",)),
        name="norm_proj",
    )(x2, g, w_bf16)


def _t5_bucket_np(n):
    n = np.maximum(n, 0).astype(np.int32)
    max_exact = REL_BUCKETS // 2
    nf = np.maximum(n, max_exact).astype(np.float32)
    large = max_exact + (np.log(nf / np.float32(max_exact)) / np.float32(math.log(REL_MAX_DIST / max_exact))
                         * np.float32(REL_BUCKETS - max_exact)).astype(np.int32)
    large = np.minimum(large, REL_BUCKETS - 1)
    return np.where(n < max_exact, n, large).astype(np.int32)


def _bias_tiles_kernel(table_ref, bucket_ref, o_ref):
    h = pl.program_id(0)
    bucket = bucket_ref[...]
    acc = jnp.zeros(bucket.shape, F32)
    for b in range(REL_BUCKETS):
        acc = jnp.where(bucket == b, table_ref[b, h], acc)
    o_ref[...] = acc


def _bias_tiles(table, tq):
    c = np.arange(tq)[:, None]
    r = np.arange(tq)[None, :]
    tiles = np.stack([_t5_bucket_np(d * tq + r - c) for d in (0, 1)])
    tiles = np.concatenate([tiles, tiles], axis=2)
    return pl.pallas_call(
        _bias_tiles_kernel,
        grid=(DIFF_HEADS,),
        in_specs=[pl.BlockSpec(memory_space=pltpu.SMEM),
                  pl.BlockSpec((2, tq, 2 * tq), lambda h: (0, 0, 0))],
        out_specs=pl.BlockSpec((None, 2, tq, 2 * tq), lambda h: (h, 0, 0, 0)),
        out_shape=jax.ShapeDtypeStruct((DIFF_HEADS, 2, tq, 2 * tq), F32),
        compiler_params=_cparams(("arbitrary",)),
        name="bias_tiles",
    )(table, jnp.asarray(tiles))


def _diff_attn_kernel(lam_ref, table_ref, qt_ref, k_ref, vt_ref, bias_ref, g_ref, o_ref,
                      s_ref, mx_ref, l_ref, acc_ref, *, tq):
    h0 = pl.program_id(1) * ATTN_HEADS_PER_STEP
    i = pl.program_id(2)
    heads = range(ATTN_HEADS_PER_STEP)
    hrows = lambda hh: slice(hh * HEAD_W, (hh + 1) * HEAD_W)
    q2t = []
    for hh in heads:
        qt = qt_ref[hrows(hh), :] * jnp.asarray(DIFF_QK_DIM ** -0.5, BF16)
        row = lax.broadcasted_iota(jnp.int32, qt.shape, 0)
        zero = jnp.zeros_like(qt)
        q2t.append(jnp.concatenate([jnp.where(row < DIFF_QK_DIM, qt, zero),
                                    jnp.where(row >= DIFF_QK_DIM, qt, zero)], axis=1))

    def fold(op, x):
        return functools.reduce(op, [x[SUBLANES * k:SUBLANES * (k + 1)] for k in range(x.shape[0] // SUBLANES)])

    def scores(j, n_keys, delta, causal):
        start = pl.multiple_of(j * tq, tq)
        for hh in heads:
            bias = table_ref[REL_BUCKETS - 1, h0 + hh] if delta is None else bias_ref[hh, delta]
            s = jnp.dot(k_ref[pl.ds(start, n_keys), hrows(hh)], q2t[hh], preferred_element_type=F32) + bias
            if causal:
                c = lax.broadcasted_iota(jnp.int32, s.shape, 0)
                r = lax.broadcasted_iota(jnp.int32, s.shape, 1) % tq
                s = jnp.where(c <= r, s, NEG_BIG)
            for k in range(n_keys // tq):
                s_ref[hh, j + k] = s[k * tq:(k + 1) * tq]
            mx_ref[hh] = jnp.maximum(mx_ref[hh], fold(jnp.maximum, s))

    mx_ref[...] = jnp.full(mx_ref.shape, NEG_BIG, F32)
    l_ref[...] = jnp.zeros(l_ref.shape, F32)
    acc_ref[...] = jnp.zeros(acc_ref.shape, F32)

    def far_pair(p, carry):
        scores(2 * p, 2 * tq, None, False)
        return carry

    n_far = jnp.maximum(i - 1, 0)
    lax.fori_loop(0, n_far // 2, far_pair, 0)

    @pl.when(n_far % 2 == 1)
    def _():
        scores(n_far - 1, tq, None, False)

    @pl.when(i >= 1)
    def _():
        scores(i - 1, tq, 1, False)

    scores(i, tq, 0, True)
    m = [jnp.max(mx_ref[hh], axis=0, keepdims=True) for hh in heads]

    def weigh(j, n_keys):
        start = pl.multiple_of(j * tq, tq)
        for hh in heads:
            s = jnp.concatenate([s_ref[hh, j + k] for k in range(n_keys // tq)], axis=0)
            e = jnp.exp(s - m[hh])
            l_ref[hh] += fold(jnp.add, e)
            acc_ref[hh] += jnp.dot(vt_ref[hrows(hh), pl.ds(start, n_keys)], e.astype(BF16),
                                   preferred_element_type=F32)

    def weigh_pair(p, carry):
        weigh(2 * p, 2 * tq)
        return carry

    n_blocks = i + 1
    lax.fori_loop(0, n_blocks // 2, weigh_pair, 0)

    @pl.when(n_blocks % 2 == 1)
    def _():
        weigh(i, tq)

    lv = lam_ref[...]
    lam = (jnp.exp(jnp.sum(lv[0:1] * lv[1:2], axis=-1, keepdims=True))
           - jnp.exp(jnp.sum(lv[2:3] * lv[3:4], axis=-1, keepdims=True)) + LAM_INIT)
    for hh in heads:
        l = jnp.sum(l_ref[hh], axis=0, keepdims=True)
        acc = acc_ref[hh]
        out_t = acc[:, :tq] / l[:, :tq] - lam * (acc[:, tq:] / l[:, tq:])
        o_ref[:, hrows(hh)] = (_rms(out_t.T, g_ref[...]) * (1.0 - LAM_INIT)).astype(o_ref.dtype)


def _diff_attn(lam, table, q_t, k, v_t, bias, g, batch, seq, tq):
    kern = functools.partial(_diff_attn_kernel, tq=tq)
    nq = seq // tq
    hp = ATTN_HEADS_PER_STEP
    hw = hp * HEAD_W
    return pl.pallas_call(
        kern,
        grid=(batch, DIFF_HEADS // hp, nq),
        in_specs=[pl.BlockSpec(lam.shape, lambda b, h, i: (0, 0)),
                  pl.BlockSpec(memory_space=pltpu.SMEM),
                  pl.BlockSpec((hw, tq), lambda b, h, i: (h, b * nq + i)),
                  pl.BlockSpec((seq, hw), lambda b, h, i: (b, h)),
                  pl.BlockSpec((hw, seq), lambda b, h, i: (h, b)),
                  pl.BlockSpec((hp, 2, tq, 2 * tq), lambda b, h, i: (h, 0, 0, 0)),
                  pl.BlockSpec((1, HEAD_W), lambda b, h, i: (0, 0))],
        out_specs=pl.BlockSpec((None, tq, hw), lambda b, h, i: (b, i, h)),
        out_shape=jax.ShapeDtypeStruct((batch, seq, DIFF_WIDTH), BF16),
        scratch_shapes=[pltpu.VMEM((hp, seq // tq, tq, 2 * tq), F32),
                        pltpu.VMEM((hp, SUBLANES, 2 * tq), F32),
                        pltpu.VMEM((hp, SUBLANES, 2 * tq), F32),
                        pltpu.VMEM((hp, HEAD_W, 2 * tq), F32)],
        compiler_params=_cparams(("parallel", "parallel", "arbitrary")),
        name="diff_attn",
    )(lam, table, q_t, k, v_t, bias, g)


def _hgrn_kernel(hq_ref, hf_ref, hi_ref, hg_ref, lb_ref, g_ref, o_ref,
                 qd_ref, kd_ref, cd_ref, oi_ref, upd_ref, prev_ref, *, seq):
    c_len = HGRN_CHUNK
    n_chunks = seq // c_len
    logits = lb_ref[...]
    ex = jnp.exp(logits - jnp.max(logits, axis=0, keepdims=True))
    lb = ex[0:1] / jnp.sum(ex, axis=0, keepdims=True)
    f = lb + (1.0 - lb) * jax.nn.sigmoid(hf_ref[...])
    log_f = jnp.log(f)
    kk = 1.0 - f
    hq = hq_ref[...]
    qq = hq * jax.nn.sigmoid(hq)
    vv = hi_ref[...]

    pos = lax.broadcasted_iota(jnp.int32, (seq, HEAD_W), 0) % c_len
    b = log_f
    sh = 1
    while sh < c_len:
        b = b + jnp.where(pos >= sh, pltpu.roll(b, sh, axis=0), 0.0)
        sh *= 2
    b3 = b.reshape(n_chunks, c_len, HEAD_W)
    b_last = b3[:, c_len - 1:c_len, :]
    q_dec = qq * jnp.exp(b)
    k_inv = kk * jnp.exp(-b)
    k_dec = kk * jnp.exp(b_last - b3).reshape(seq, HEAD_W)
    qd_ref[...] = q_dec.astype(BF16)
    kd_ref[...] = k_dec.astype(BF16)
    cd_ref[...] = jnp.exp(b_last.reshape(n_chunks, HEAD_W))

    blk = HEAD_W
    nb = seq // blk
    qd_b = q_dec.astype(BF16).reshape(nb, blk, HEAD_W)
    ki_b = k_inv.astype(BF16).reshape(nb, blk, HEAD_W)
    scores = jnp.einsum('nqd,nkd->nqk', qd_b, ki_b, preferred_element_type=F32)
    r = lax.broadcasted_iota(jnp.int32, (blk, blk), 0)
    c = lax.broadcasted_iota(jnp.int32, (blk, blk), 1)
    keep = (c <= r) & ((r // c_len) == (c // c_len))
    scores = jnp.where(keep[None], scores, 0.0)
    o_intra = jnp.einsum('nqk,nkv->nqv', scores.astype(BF16), vv.astype(BF16).reshape(nb, blk, HEAD_W),
                         preferred_element_type=F32).reshape(seq, HEAD_W)

    unroll = 16

    def rows_of(ci):
        return pl.ds(pl.multiple_of(ci * c_len, c_len), c_len)

    def updates(g, carry):
        for u in range(unroll):
            ci = g * unroll + u
            vc = hi_ref[rows_of(ci), :].astype(BF16)
            upd_ref[ci] = lax.dot_general(vc, kd_ref[rows_of(ci), :], (((0,), (0,)), ((), ())),
                                          preferred_element_type=F32)
        return carry

    lax.fori_loop(0, n_chunks // unroll, updates, 0)

    def recur(ci, st):
        prev_ref[ci] = st.astype(BF16)
        return st * cd_ref[pl.ds(ci, 1), :] + upd_ref[ci]

    lax.fori_loop(0, n_chunks, recur, jnp.zeros((HEAD_W, HEAD_W), F32))

    def reads(g, carry):
        for u in range(unroll):
            ci = g * unroll + u
            oi_ref[rows_of(ci), :] = lax.dot_general(qd_ref[rows_of(ci), :], prev_ref[ci],
                                                     (((1,), (1,)), ((), ())), preferred_element_type=F32)
        return carry

    lax.fori_loop(0, n_chunks // unroll, reads, 0)

    o = o_intra + oi_ref[...]
    hg = hg_ref[...]
    o_ref[...] = (_rms(o, g_ref[...]) * (hg * jax.nn.sigmoid(hg))).astype(o_ref.dtype)


def _hgrn(hin, lb, g, batch, seq):
    kern = functools.partial(_hgrn_kernel, seq=seq)
    col = lambda k: (lambda b, h: (b, 0, k * HGRN_HEADS + h))
    blk = (None, seq, HEAD_W)
    return pl.pallas_call(
        kern,
        grid=(batch, HGRN_HEADS),
        in_specs=[pl.BlockSpec(blk, col(0)), pl.BlockSpec(blk, col(1)),
                  pl.BlockSpec(blk, col(2)), pl.BlockSpec(blk, col(3)),
                  pl.BlockSpec((lb.shape[0], HEAD_W), lambda b, h: (0, h)),
                  pl.BlockSpec((1, HEAD_W), lambda b, h: (0, 0))],
        out_specs=pl.BlockSpec(blk, lambda b, h: (b, 0, h)),
        out_shape=jax.ShapeDtypeStruct((batch, seq, HGRN_WIDTH), BF16),
        scratch_shapes=[pltpu.VMEM((seq, HEAD_W), BF16), pltpu.VMEM((seq, HEAD_W), BF16),
                        pltpu.VMEM((seq // HGRN_CHUNK, HEAD_W), F32), pltpu.VMEM((seq, HEAD_W), F32),
                        pltpu.VMEM((seq // HGRN_CHUNK, HEAD_W, HEAD_W), F32),
                        pltpu.VMEM((seq // HGRN_CHUNK, HEAD_W, HEAD_W), BF16)],
        compiler_params=_cparams(("parallel", "parallel")),
        name="hgrn",
    )(hin, hin, hin, hin, lb, g)


def _out_proj_kernel(x_ref, a_ref, o_ref, wa_ref, wo_ref, g_ref, h_ref, hnt_ref):
    h = (x_ref[...]
         + jnp.dot(a_ref[...], wa_ref[...], preferred_element_type=F32)
         + jnp.dot(o_ref[...], wo_ref[...], preferred_element_type=F32))
    h_ref[...] = h
    hnt_ref[...] = _rms(h, g_ref[...]).T.astype(BF16)


def _out_proj(x2, a2, o2, wa, wo, g, tb):
    t, d = x2.shape
    return pl.pallas_call(
        _out_proj_kernel,
        grid=(t // tb,),
        in_specs=[pl.BlockSpec((tb, d), lambda i: (i, 0)),
                  pl.BlockSpec((tb, DIFF_WIDTH), lambda i: (i, 0)),
                  pl.BlockSpec((tb, HGRN_WIDTH), lambda i: (i, 0)),
                  pl.BlockSpec((DIFF_WIDTH, d), lambda i: (0, 0)),
                  pl.BlockSpec((HGRN_WIDTH, d), lambda i: (0, 0)),
                  pl.BlockSpec((1, d), lambda i: (0, 0))],
        out_specs=[pl.BlockSpec((tb, d), lambda i: (i, 0)),
                   pl.BlockSpec((d, tb), lambda i: (0, i))],
        out_shape=[jax.ShapeDtypeStruct((t, d), F32),
                   jax.ShapeDtypeStruct((d, t), BF16)],
        compiler_params=_cparams(("parallel",)),
        name="out_proj",
    )(x2, a2, o2, wa, wo, g)


SUBLANES = 8
ATTN_HEADS_PER_STEP = 4
TAU_GUARD = 2.0 ** -21


def _oddeven_merge(lo, hi, r):
    step = r * 2
    if step < hi - lo:
        yield from _oddeven_merge(lo, hi, step)
        yield from _oddeven_merge(lo + r, hi, step)
        yield from [(i, i + r) for i in range(lo + r, hi - r, step)]
    else:
        yield (lo, lo + r)


def _oddeven_merge_sort(lo, hi):
    if hi - lo >= 1:
        mid = lo + (hi - lo) // 2
        yield from _oddeven_merge_sort(lo, mid)
        yield from _oddeven_merge_sort(mid + 1, hi)
        yield from _oddeven_merge(lo, hi, 1)


def _bitonic_merge(n):
    d = n // 2
    while d >= 1:
        yield from [(i, i + d) for i in range(n) if (i // d) % 2 == 0]
        d //= 2


SORT16 = tuple(_oddeven_merge_sort(0, PEER_TOPK - 1))
MERGE16 = tuple(_bitonic_merge(PEER_TOPK))


def _bf16_row_tile(row, n_rows):
    packed_rows = 16
    tile16 = jnp.broadcast_to(row, (packed_rows, row.shape[1])).astype(BF16)
    return jnp.broadcast_to(tile16[None], (n_rows // packed_rows, packed_rows, row.shape[1])
                            ).reshape(n_rows, row.shape[1])


def _compare_exchange(v, pairs):
    for i, j in pairs:
        v[i], v[j] = jnp.maximum(v[i], v[j]), jnp.minimum(v[i], v[j])


def _top16_sorted(x):
    v = [x[SUBLANES * k:SUBLANES * (k + 1)] for k in range(PEER_TOPK)]
    _compare_exchange(v, SORT16)
    shift = SUBLANES // 2
    while shift >= 1:
        v = [jnp.maximum(v[k], pltpu.roll(v[PEER_TOPK - 1 - k], shift, axis=0)) for k in range(PEER_TOPK)]
        _compare_exchange(v, MERGE16)
        shift //= 2
    return v


def _rows_on_sublanes(rep, start):
    sub = lax.broadcasted_iota(jnp.int32, rep[0].shape, 0)
    out = rep[start]
    for s in range(1, SUBLANES):
        out = jnp.where(sub == s, rep[start + s], out)
    return out


def _pair_candidates(a_rep, b_rep):
    k = PEER_TOPK
    sub = lax.broadcasted_iota(jnp.int32, a_rep[0].shape, 0)
    b_lo = _rows_on_sublanes(b_rep, 0)
    b_hi = _rows_on_sublanes(b_rep, SUBLANES)
    a_hi = _rows_on_sublanes(a_rep, SUBLANES)
    pieces = [a_rep[0] + b_lo, a_rep[0] + b_hi, a_rep[1] + b_lo]
    for i in range(2, SUBLANES):
        pieces.append(jnp.where(sub < k // (i + 1), a_rep[i] + b_lo, -jnp.inf))
    pieces.append(a_hi + b_rep[0])
    return jnp.concatenate(pieces, axis=0)


def _prefix_count(test, levels):
    bits = []
    for depth in range(4):
        width = PEER_TOPK >> (depth + 1)
        pivot = _pick_level(levels, bits, width)
        bits.append(test(pivot))
    count = sum(jnp.where(b, float(PEER_TOPK >> (d + 1)), 0.0) for d, b in enumerate(bits))
    return jnp.where(test(levels[PEER_TOPK - 1]), float(PEER_TOPK), count)


def _pick_level(levels, bits, width):
    def rec(base, remaining, w):
        if not remaining:
            return levels[base + width - 1]
        return jnp.where(remaining[0], rec(base + w, remaining[1:], w // 2), rec(base, remaining[1:], w // 2))
    return rec(0, bits, PEER_TOPK // 2)


def _kth_largest(x, k):
    for r in range(k):
        m = jnp.max(x, axis=0, keepdims=True)
        if r + 1 < k:
            x = jnp.where(x == m, -jnp.inf, x)
    return m


def _peer_route_kernel(hnt_ref, wqt_ref, keys_ref, rank_ref, alpha_ref, nsel_ref, beta_ref, qt_ref, s_ref):
    qt_ref[...] = jnp.dot(wqt_ref[...], hnt_ref[...], preferred_element_type=F32).astype(BF16)
    half = PEER_QUERY_DIM // 2
    lane_tiles = hnt_ref.shape[1] // LANES
    k = PEER_TOPK

    for hh in range(PEER_HEADS):
        for c in range(2):
            s = jnp.dot(keys_ref[hh, c], qt_ref[(2 * hh + c) * half:(2 * hh + c + 1) * half, :],
                        preferred_element_type=F32)
            for lt in range(lane_tiles):
                s_ref[hh, c, lt] = s[:, lt * LANES:(lt + 1) * LANES]

    def head(h, carry):
        def tile(lt, carry):
            cols = pl.ds(pl.multiple_of(lt * LANES, LANES), LANES)
            s1 = s_ref[h, 0, lt]
            s2 = s_ref[h, 1, lt]
            a_rep = _top16_sorted(s1)
            b_rep = _top16_sorted(s2)
            cands = _pair_candidates(a_rep, b_rep)
            thr = _kth_largest(cands, k)
            top = a_rep[0][0:1] + b_rep[0][0:1]
            z = jnp.sum(jnp.where(cands >= thr, jnp.exp(cands - top), 0.0), axis=0, keepdims=True)
            rank1 = []
            n_sel = []
            theta = [(thr - a_rep[j]) - (jnp.abs(thr) + jnp.abs(a_rep[j])) * TAU_GUARD for j in range(k)]
            for v in range(PEER_N_KEYS // SUBLANES):
                x1 = s1[SUBLANES * v:SUBLANES * (v + 1)]
                x2 = s2[SUBLANES * v:SUBLANES * (v + 1)]
                rank1.append(_prefix_count(lambda a: a > x1, a_rep))
                n_sel.append(_prefix_count(lambda th: x2 >= th, theta))
            rank_ref[h, :, cols] = jnp.concatenate(rank1, axis=0)
            alpha_ref[h, :, cols] = jnp.exp(s1 - a_rep[0][0:1]) * (0.5 / z)
            nsel_ref[h, lt] = jnp.concatenate(n_sel, axis=0).astype(BF16)
            beta_ref[h, lt] = jnp.exp(s2 - b_rep[0][0:1]).astype(BF16)
            return carry

        def tile_pair(p, carry):
            return tile(2 * p + 1, tile(2 * p, carry))

        return lax.fori_loop(0, lane_tiles // 2, tile_pair, carry)

    lax.fori_loop(0, PEER_HEADS, head, 0)


def _peer_route(hnt, wqt, keys, tb):
    d, t = hnt.shape
    nq = wqt.shape[0]
    oshape = jax.ShapeDtypeStruct((PEER_HEADS, PEER_N_KEYS, t), F32)
    ospec = pl.BlockSpec((PEER_HEADS, PEER_N_KEYS, tb), lambda i: (0, 0, i))
    tshape = jax.ShapeDtypeStruct((PEER_HEADS, t // LANES, PEER_N_KEYS, LANES), BF16)
    tspec = pl.BlockSpec((PEER_HEADS, tb // LANES, PEER_N_KEYS, LANES), lambda i: (0, i, 0, 0))
    return pl.pallas_call(
        _peer_route_kernel,
        grid=(t // tb,),
        in_specs=[pl.BlockSpec((d, tb), lambda i: (0, i)),
                  pl.BlockSpec((nq, d), lambda i: (0, 0)),
                  pl.BlockSpec(keys.shape, lambda i: (0, 0, 0, 0))],
        out_specs=[ospec, ospec, tspec, tspec],
        out_shape=[oshape, oshape, tshape, tshape],
        scratch_shapes=[pltpu.VMEM((nq, tb), BF16), pltpu.VMEM((PEER_HEADS, 2, tb // LANES, PEER_N_KEYS, LANES), F32)],
        compiler_params=_cparams(("parallel",)),
        name="peer_route",
    )(hnt, wqt, keys)


def _peer_dense_kernel(hnt_ref, u_ref, vt_ref, rank_ref, alpha_ref, nsel_ref, beta_ref, h_ref, g_ref, o_ref,
                       hid_ref, act_ref, acc_ref, *, te, tb, n_tiles):
    j = pl.program_id(1)
    cur = j % 2
    prev = 1 - cur
    rows_per_tile = te // PEER_N_KEYS
    lane_tiles = tb // LANES

    def hidden(hid_ref):
        hid = jnp.dot(u_ref[...], hnt_ref[...], preferred_element_type=F32)
        for lt in range(lane_tiles):
            hid_ref[lt] = hid[:, lt * LANES:(lt + 1) * LANES]

    def activations(hid_ref, act_ref):
        def lane_tile(lt, carry):
            cols = pl.ds(pl.multiple_of(lt * LANES, LANES), LANES)
            zero = jnp.zeros((PEER_N_KEYS, LANES), BF16)
            for r in range(rows_per_tile):
                w = None
                for h in range(PEER_HEADS):
                    sel = _bf16_row_tile(rank_ref[h, r:r + 1, cols], PEER_N_KEYS) < nsel_ref[h, lt]
                    term = jnp.where(sel, beta_ref[h, lt], zero) * _bf16_row_tile(alpha_ref[h, r:r + 1, cols],
                                                                                  PEER_N_KEYS)
                    w = term if w is None else w + term
                rows = slice(r * PEER_N_KEYS, (r + 1) * PEER_N_KEYS)
                hid_t = hid_ref[lt, rows, :]
                gelu2 = hid_t * (1.0 + lax.erf(hid_t * (2.0 ** -0.5)))
                act_ref[lt, rows, :] = gelu2.astype(BF16) * w
            return carry

        lax.fori_loop(0, lane_tiles, lane_tile, 0)

    def project(act_ref):
        act = jnp.concatenate([act_ref[lt] for lt in range(lane_tiles)], axis=1)
        acc_ref[...] += jnp.dot(vt_ref[...], act, preferred_element_type=F32)

    def finish():
        chunk = 2 * LANES
        for c in range(tb // chunk):
            rows = slice(c * chunk, (c + 1) * chunk)
            o_ref[rows, :] = _rms(h_ref[rows, :] + acc_ref[:, rows].T, g_ref[...])

    def run(do_a, do_b, do_c):
        if do_a:
            hidden(hid_ref.at[cur])
        if do_b:
            activations(hid_ref.at[prev], act_ref.at[prev])
        if do_c:
            project(act_ref.at[cur])

    @pl.when(jnp.logical_and(j >= 2, j < n_tiles))
    def _():
        run(True, True, True)

    @pl.when(j == 0)
    def _():
        acc_ref[...] = jnp.zeros(acc_ref.shape, acc_ref.dtype)
        run(True, False, False)

    @pl.when(j == 1)
    def _():
        run(True, True, False)

    @pl.when(j == n_tiles)
    def _():
        run(False, True, True)

    @pl.when(j == n_tiles + 1)
    def _():
        run(False, False, True)
        finish()


def _peer_dense(hnt, u_bf16, v_bf16, rank1, alpha, n_sel, beta, h2, final_g, tb, te):
    d, t = hnt.shape
    n_tiles = u_bf16.shape[0] // te
    assert n_tiles >= 2
    vt_tiles = v_bf16.reshape(n_tiles, te, d).transpose(0, 2, 1)
    kern = functools.partial(_peer_dense_kernel, te=te, tb=tb, n_tiles=n_tiles)
    last = n_tiles - 1
    rspec = pl.BlockSpec((PEER_HEADS, tb // LANES, PEER_N_KEYS, LANES), lambda i, j: (0, i, 0, 0))
    i1spec = pl.BlockSpec((PEER_HEADS, te // PEER_N_KEYS, tb), lambda i, j: (0, jnp.clip(j - 1, 0, last), i))
    stage_shape = (tb // LANES, te, LANES)
    return pl.pallas_call(
        kern,
        grid=(t // tb, n_tiles + 2),
        in_specs=[pl.BlockSpec((d, tb), lambda i, j: (0, i)),
                  pl.BlockSpec((te, d), lambda i, j: (jnp.minimum(j, last), 0)),
                  pl.BlockSpec((None, d, te), lambda i, j: (jnp.clip(j - 2, 0, last), 0, 0)),
                  i1spec, i1spec, rspec, rspec,
                  pl.BlockSpec((tb, d), lambda i, j: (i, 0)),
                  pl.BlockSpec((1, d), lambda i, j: (0, 0))],
        out_specs=pl.BlockSpec((tb, d), lambda i, j: (i, 0)),
        out_shape=jax.ShapeDtypeStruct((t, d), F32),
        scratch_shapes=[pltpu.VMEM((2,) + stage_shape, F32), pltpu.VMEM((2,) + stage_shape, BF16),
                        pltpu.VMEM((d, tb), F32)],
        compiler_params=_cparams(("parallel", "arbitrary")),
        name="peer_dense",
    )(hnt, u_bf16, vt_tiles, rank1, alpha, n_sel, beta, h2, final_g)


def _largest_divisor(n, cap, multiple):
    best = multiple
    k = multiple
    while k <= min(n, cap):
        if n % k == 0:
            best = k
        k += multiple
    return best


def kernel(x, norm1_g, w_in, diff_lambda_q1, diff_lambda_k1, diff_lambda_q2, diff_lambda_k2, diff_subln_g,
           rel_bias_table, hgrn_lb_logits, hgrn_gnorm_g, w_out, norm2_g, peer_w_q, peer_sub_keys, peer_u,
           peer_v, final_norm_g):
    batch, seq, d = x.shape
    t = batch * seq
    assert d == D_MODEL and seq % 256 == 0 and w_in.shape[0] == 1
    tb_proj = _largest_divisor(t, 512, 256)
    tq = 256
    tb_route = _largest_divisor(t, 512, 256)
    tb_peer = _largest_divisor(t, 1024, 512)
    te_peer = 1024

    x2 = x.reshape(t, d)
    row = lambda v: v.reshape(1, -1).astype(F32)

    lam = jnp.stack([diff_lambda_q1[0], diff_lambda_k1[0], diff_lambda_q2[0], diff_lambda_k2[0]]).astype(F32)
    lb = hgrn_lb_logits.astype(F32)
    table = rel_bias_table.astype(F32)

    q_t, k, v_t, hin = _norm_proj(x2, row(norm1_g[0]), w_in[0].astype(BF16), tb_proj)
    bias = _bias_tiles(table, tq)
    a = _diff_attn(lam, table, q_t, k, v_t, bias, row(diff_subln_g[0]), batch, seq, tq)
    o = _hgrn(hin.reshape(batch, seq, HGRN_COLS), lb, row(hgrn_gnorm_g[0]), batch, seq)
    w_o = w_out[0].astype(BF16)
    h2, hnt = _out_proj(x2, a.reshape(t, DIFF_WIDTH), o.reshape(t, HGRN_WIDTH),
                        w_o[:DIFF_WIDTH], w_o[DIFF_WIDTH:], row(norm2_g[0]), tb_proj)
    rank1, alpha, n_sel, beta = _peer_route(hnt, peer_w_q[0].T.astype(BF16), peer_sub_keys[0].astype(BF16), tb_route)
    out = _peer_dense(hnt, peer_u[0].astype(BF16), peer_v[0].astype(BF16), rank1, alpha, n_sel, beta,
                      h2, row(final_norm_g), tb_peer, te_peer)
    return out.reshape(batch, seq, d)
```
